```python
import math
import jax
import jax.numpy as jnp
from jax import lax
import numpy as np

D_MODEL = 4096
BATCH = 4
SEQ = 2048
DEPTH = 2
DEC_BATCH = 8
DEC_SEQ = 8
PAST_LEN = 16384
PAGE_SIZE = 128

MIX_WIDTH = D_MODEL
W_A = MIX_WIDTH // 4
W_B = MIX_WIDTH // 4
W_C = MIX_WIDTH // 4
W_D = MIX_WIDTH - W_A - W_B - W_C
H_A = 4
DH_A = W_A // H_A
MLSTM_CHUNK = 64
FORGET_BIAS_LO = 3.0
FORGET_BIAS_HI = 6.0
H_B = 8
DH_B = W_B // H_B
SB_BLOCK = 128
SB_BIAS_LO = -8.0
SB_BIAS_HI = -5.0
CONV_W = 31
S5_CH = 16
S5_G = W_D // S5_CH
S5_N = 64
N_MEM = 256
H_X = 4
DH_X = 128
W_X = H_X * DH_X
D_FF = 4 * D_MODEL
EPS = 1e-6
IN_SPLITS = (W_A, W_A, W_A, W_A, 2 * H_A, W_B, W_B, W_B, W_C, W_C, W_D)
IN_COLS = sum(IN_SPLITS)

kernel_name = 'hymba_mlstm_stickbreak_conv_s5_step'


def rmsnorm(x, g):
    xf = x.astype(jnp.float32)
    y = xf * lax.rsqrt(jnp.mean(xf * xf, axis=-1, keepdims=True) + EPS)
    return (y * g.astype(jnp.float32)).astype(x.dtype)


def layernorm(x, g, b):
    xf = x.astype(jnp.float32)
    xc = xf - jnp.mean(xf, axis=-1, keepdims=True)
    y = xc * lax.rsqrt(jnp.mean(xc * xc, axis=-1, keepdims=True) + EPS)
    return (y * g.astype(jnp.float32) + b.astype(jnp.float32)).astype(x.dtype)


def split_columns(proj):
    offs = []
    acc = 0
    for s in IN_SPLITS[:-1]:
        acc += s
        offs.append(acc)
    return jnp.split(proj, offs, axis=-1)


def mlstm_chunkwise(q, k, v, i_pre, f_pre, c0, n0, m0):
    f32 = jnp.float32
    bsz, length, heads, dh = q.shape
    cs = MLSTM_CHUNK if length % MLSTM_CHUNK == 0 else length
    nc = length // cs

    def to_chunks(a):
        a = a.astype(f32).reshape((bsz, nc, cs, heads) + a.shape[3:])
        return jnp.moveaxis(a, (1, 3), (0, 2))

    qc = to_chunks(q)
    kc = to_chunks(k.astype(f32) * (dh ** -0.5))
    vc = to_chunks(v)
    ic = to_chunks(i_pre)
    lfc = to_chunks(jax.nn.log_sigmoid(f_pre.astype(f32)))
    causal = jnp.tril(jnp.ones((cs, cs), dtype=bool))

    def chunk_step(carry, xs):
        c, n, m = carry
        qb, kb, vb, ib, lfb = xs
        fcum = jnp.cumsum(lfb, axis=-1)
        dmat = fcum[..., :, None] - fcum[..., None, :] + ib[..., None, :]
        dmat = jnp.where(causal, dmat, -jnp.inf)
        carry_log = fcum + m[..., None]
        m_t = jnp.maximum(jnp.max(dmat, axis=-1), carry_log)
        s = jnp.einsum('bhtd,bhsd->bhts', qb, kb) * jnp.exp(dmat - m_t[..., None])
        w_prev = jnp.exp(carry_log - m_t)
        num = jnp.einsum('bhts,bhsd->bhtd', s, vb) + w_prev[..., None] * jnp.einsum('bhtk,bhvk->bhtv', qb, c)
        den = jnp.sum(s, axis=-1) + w_prev * jnp.einsum('bhtk,bhk->bht', qb, n)
        h = num / jnp.maximum(jnp.abs(den), jnp.exp(-m_t))[..., None]
        m_new = m_t[..., -1]
        decay_prev = jnp.exp(fcum[..., -1] + m - m_new)
        w_end = jnp.exp(fcum[..., -1:] - fcum + ib - m_new[..., None])
        c_new = decay_prev[..., None, None] * c + jnp.einsum('bhs,bhsv,bhsk->bhvk', w_end, vb, kb)
        n_new = decay_prev[..., None] * n + jnp.einsum('bhs,bhsk->bhk', w_end, kb)
        return (c_new, n_new, m_new), h

    (c, n, m), hs = lax.scan(chunk_step, (c0.astype(f32), n0.astype(f32), m0.astype(f32)), (qc, kc, vc, ic, lfc))
    h = jnp.moveaxis(hs, (0, 2), (1, 3)).reshape(bsz, length, heads * dh)
    return h, (c, n, m)


def stick_breaking(q, k, v, bias, q_offset):
    f32 = jnp.float32
    bsz, lq, heads, dh = q.shape
    lk = k.shape[1]
    bs = SB_BLOCK if lq % SB_BLOCK == 0 else lq
    nb = lq // bs
    kf = k.astype(f32)
    vf = v.astype(f32)
    bf = bias.astype(f32)[None, :, None, None]
    qb = jnp.moveaxis((q.astype(f32) * (dh ** -0.5)).reshape(bsz, nb, bs, heads, dh), 1, 0)
    key_idx = jnp.arange(lk)

    def query_block(args):
        qblk, b = args
        q_idx = q_offset + b * bs + jnp.arange(bs)
        z = jnp.einsum('bqhd,bkhd->bhqk', qblk, kf) + bf
        mask = key_idx[None, :] < q_idx[:, None]
        log1m = jnp.where(mask, jax.nn.log_sigmoid(-z), 0.0)
        between = lax.cumsum(log1m, axis=3, reverse=True) - log1m
        a = jnp.where(mask, jnp.exp(jax.nn.log_sigmoid(z) + between), 0.0)
        return jnp.einsum('bhqk,bkhd->bqhd', a, vf)

    o = lax.map(query_block, (qb, jnp.arange(nb)))
    return jnp.moveaxis(o, 0, 1).reshape(bsz, lq, heads * dh)


def conformer_conv(a, gate, buf, w, b, ln_g, ln_b):
    u = a * jax.nn.sigmoid(gate)
    ext = jnp.concatenate([buf.astype(u.dtype), u], axis=1)
    y = lax.conv_general_dilated(ext, w.astype(ext.dtype)[:, None, :], window_strides=(1,), padding='VALID',
                                 dimension_numbers=('NWC', 'WIO', 'NWC'), feature_group_count=u.shape[-1])
    y = layernorm(y + b.astype(y.dtype), ln_g, ln_b)
    return jax.nn.silu(y), ext[:, -(CONV_W - 1):]


def s5_ssm(u, h0_re, h0_im, a_re, a_im, log_dt, b_re, b_im, c_re, c_im, d):
    f32 = jnp.float32
    bsz, length, _ = u.shape
    uf = u.astype(f32).reshape(bsz, length, S5_G, S5_CH)
    dt = jnp.exp(log_dt.astype(f32))[:, None]
    ar = a_re.astype(f32)
    ai = a_im.astype(f32)
    mag = jnp.exp(ar * dt)
    abar_re = mag * jnp.cos(ai * dt)
    abar_im = mag * jnp.sin(ai * dt)
    zr = abar_re - 1.0
    zi = abar_im
    den = ar * ar + ai * ai
    coef_re = (zr * ar + zi * ai) / den
    coef_im = (zi * ar - zr * ai) / den
    br = b_re.astype(f32)
    bi = b_im.astype(f32)
    bbar_re = coef_re[..., None] * br - coef_im[..., None] * bi
    bbar_im = coef_re[..., None] * bi + coef_im[..., None] * br
    bu_re = jnp.einsum('blgc,gnc->blgn', uf, bbar_re)
    bu_im = jnp.einsum('blgc,gnc->blgn', uf, bbar_im)
    h0r = h0_re.astype(f32)
    h0i = h0_im.astype(f32)
    bu_re = bu_re.at[:, 0].add(abar_re * h0r - abar_im * h0i)
    bu_im = bu_im.at[:, 0].add(abar_re * h0i + abar_im * h0r)
    a_r = jnp.broadcast_to(abar_re, bu_re.shape)
    a_i = jnp.broadcast_to(abar_im, bu_im.shape)

    def combine(e1, e2):
        a1r, a1i, b1r, b1i = e1
        a2r, a2i, b2r, b2i = e2
        return (a1r * a2r - a1i * a2i, a1r * a2i + a1i * a2r,
                a2r * b1r - a2i * b1i + b2r, a2r * b1i + a2i * b1r + b2i)

    _, _, hr, hi = lax.associative_scan(combine, (a_r, a_i, bu_re, bu_im), axis=1)
    y = (jnp.einsum('blgn,gcn->blgc', hr, c_re.astype(f32)) - jnp.einsum('blgn,gcn->blgc', hi, c_im.astype(f32))
         + d.astype(f32).reshape(S5_G, S5_CH) * uf)
    return y.reshape(bsz, length, W_D), hr[:, -1], hi[:, -1]


def memory_kv(mem, g, w_k, w_v):
    bsz, nm, _ = mem.shape
    m = rmsnorm(mem, g)
    k = jnp.einsum('bmd,de->bme', m, w_k).reshape(bsz, nm, H_X, DH_X)
    v = jnp.einsum('bmd,de->bme', m, w_v).reshape(bsz, nm, H_X, DH_X)
    return k, v


def cross_attend(h, mem_k, mem_v, w_q, w_o):
    bsz, length, _ = h.shape
    q = jnp.einsum('bld,de->ble', h, w_q).reshape(bsz, length, H_X, DH_X)
    s = jnp.einsum('blhd,bmhd->bhlm', q.astype(jnp.float32), mem_k.astype(jnp.float32)) * (DH_X ** -0.5)
    p = jax.nn.softmax(s, axis=-1)
    o = jnp.einsum('bhlm,bmhd->blhd', p, mem_v.astype(jnp.float32)).reshape(bsz, length, W_X).astype(h.dtype)
    return jnp.einsum('ble,ed->bld', o, w_o)


def decoder_layer(x, mem_k, mem_v, past_k, past_v, mlstm_state, conv_buf, s5_state, p):
    f32 = jnp.float32
    bsz, length, _ = x.shape
    dt = x.dtype
    h = rmsnorm(x, p['norm_mix_pre'])
    proj = jnp.einsum('bld,dc->blc', h, p['w_in'])
    qa, ka, va, oa, ga, qb, kb, vb, ca, cg, ud = split_columns(proj)
    gates = ga.astype(f32) + p['b_mlstm_gates'].astype(f32)
    ra = lambda t: t.reshape(bsz, length, H_A, DH_A)
    h_a, mlstm_new = mlstm_chunkwise(ra(qa), ra(ka), ra(va), gates[..., :H_A], gates[..., H_A:], *mlstm_state)
    y_a = (jax.nn.sigmoid(oa.astype(f32)) * h_a).astype(dt)
    kb_h = kb.reshape(bsz, length, H_B, DH_B)
    vb_h = vb.reshape(bsz, length, H_B, DH_B)
    if past_k is None:
        keys, vals, off = kb_h, vb_h, 0
    else:
        keys = jnp.concatenate([past_k.astype(dt), kb_h], axis=1)
        vals = jnp.concatenate([past_v.astype(dt), vb_h], axis=1)
        off = past_k.shape[1]
    y_b = stick_breaking(qb.reshape(bsz, length, H_B, DH_B), keys, vals, p['sb_bias'], off).astype(dt)
    y_c, conv_new = conformer_conv(ca, cg, conv_buf, p['conv_w'], p['conv_b'], p['conv_ln_g'], p['conv_ln_b'])
    y_c = y_c.astype(dt)
    y_s, s5_re, s5_im = s5_ssm(ud, s5_state[0], s5_state[1], p['s5_a_re'], p['s5_a_im'], p['s5_log_dt'],
                               p['s5_b_re'], p['s5_b_im'], p['s5_c_re'], p['s5_c_im'], p['s5_d'])
    g = jax.nn.gelu(y_s)
    y_d = (g * jax.nn.sigmoid(g @ p['s5_gate_w'].astype(f32) + p['s5_gate_b'].astype(f32))).astype(dt)
    gg = p['g_group']
    o1 = W_A
    o2 = W_A + W_B
    o3 = W_A + W_B + W_C
    mixed = jnp.concatenate([rmsnorm(y_a, gg[:o1]), rmsnorm(y_b, gg[o1:o2]),
                             rmsnorm(y_c, gg[o2:o3]), rmsnorm(y_d, gg[o3:])], axis=-1)
    x = x + rmsnorm(jnp.einsum('blc,cd->bld', mixed, p['w_out']), p['norm_mix_post'])
    hx = rmsnorm(x, p['norm_x_pre'])
    x = x + rmsnorm(cross_attend(hx, mem_k, mem_v, p['w_xq'], p['w_xo']), p['norm_x_post'])
    hf = rmsnorm(x, p['norm_ffn_pre'])
    uf = jnp.square(jax.nn.relu(jnp.einsum('bld,df->blf', hf, p['w_up'])))
    x = x + rmsnorm(jnp.einsum('blf,fd->bld', uf, p['w_down']), p['norm_ffn_post'])
    return x, kb_h, vb_h, mlstm_new, conv_new, (s5_re, s5_im)


def setup_inputs(seed: int = 0) -> dict:
    key = jax.random.key(seed)
    keys = iter(jax.random.split(key, 64))
    f32 = jnp.float32

    def nrm(shape, scale=1.0):
        return jax.random.normal(next(keys), shape, f32) * scale

    def gain(shape):
        return 1.0 + nrm(shape, 0.02)

    n_pages = PAST_LEN // PAGE_SIZE
    n_pool = (DEC_BATCH * n_pages * 5) // 4
    x_prompt = nrm((BATCH, SEQ, D_MODEL))
    x_sample = nrm((DEC_BATCH, DEC_SEQ, D_MODEL))
    cache_sb_k = nrm((DEPTH, n_pool, PAGE_SIZE, H_B, DH_B))
    cache_sb_v = nrm((DEPTH, n_pool, PAGE_SIZE, H_B, DH_B))
    page_table = jax.random.permutation(next(keys), n_pool)[:DEC_BATCH * n_pages].reshape(DEC_BATCH, n_pages).astype(jnp.int32)
    state_mlstm_c = nrm((DEPTH, DEC_BATCH, H_A, DH_A, DH_A), 0.3)
    state_mlstm_n = nrm((DEPTH, DEC_BATCH, H_A, DH_A), 0.3)
    state_mlstm_m = nrm((DEPTH, DEC_BATCH, H_A), 0.5)
    state_conv = nrm((DEPTH, DEC_BATCH, CONV_W - 1, W_C))
    state_s5_re = nrm((DEPTH, DEC_BATCH, S5_G, S5_N), 0.3)
    state_s5_im = nrm((DEPTH, DEC_BATCH, S5_G, S5_N), 0.3)
    cache_mem_k = nrm((DEPTH, DEC_BATCH, N_MEM, H_X, DH_X))
    cache_mem_v = nrm((DEPTH, DEC_BATCH, N_MEM, H_X, DH_X))
    mem_prompt = nrm((BATCH, N_MEM, D_MODEL))
    forget_bias = jnp.broadcast_to(jnp.linspace(FORGET_BIAS_LO, FORGET_BIAS_HI, H_A, dtype=f32), (DEPTH, H_A))
    b_mlstm_gates = jnp.concatenate([jnp.zeros((DEPTH, H_A), f32), forget_bias], axis=-1) + nrm((DEPTH, 2 * H_A), 0.01)
    sb_bias = jnp.broadcast_to(jnp.linspace(SB_BIAS_LO, SB_BIAS_HI, H_B, dtype=f32), (DEPTH, H_B)) + nrm((DEPTH, H_B), 0.01)
    s5_a_re = -0.5 + nrm((DEPTH, S5_G, S5_N), 0.01)
    s5_a_im = jnp.pi * jnp.arange(S5_N, dtype=f32) + nrm((DEPTH, S5_G, S5_N), 0.01)
    s5_log_dt = jax.random.uniform(next(keys), (DEPTH, S5_G), f32, math.log(1e-3), math.log(1e-1))
    return {
        'x_prompt': x_prompt,
        'x_sample': x_sample,
        'cache_sb_k': cache_sb_k,
        'cache_sb_v': cache_sb_v,
        'page_table': page_table,
        'state_mlstm_c': state_mlstm_c,
        'state_mlstm_n': state_mlstm_n,
        'state_mlstm_m': state_mlstm_m,
        'state_conv': state_conv,
        'state_s5_re': state_s5_re,
        'state_s5_im': state_s5_im,
        'cache_mem_k': cache_mem_k,
        'cache_mem_v': cache_mem_v,
        'mem_prompt': mem_prompt,
        'norm_mix_pre': gain((DEPTH, D_MODEL)),
        'w_in': nrm((DEPTH, D_MODEL, IN_COLS), D_MODEL ** -0.5),
        'b_mlstm_gates': b_mlstm_gates,
        'sb_bias': sb_bias,
        'conv_w': nrm((DEPTH, CONV_W, W_C), CONV_W ** -0.5),
        'conv_b': nrm((DEPTH, W_C), 0.01),
        'conv_ln_g': gain((DEPTH, W_C)),
        'conv_ln_b': nrm((DEPTH, W_C), 0.01),
        's5_a_re': s5_a_re,
        's5_a_im': s5_a_im,
        's5_log_dt': s5_log_dt,
        's5_b_re': nrm((DEPTH, S5_G, S5_N, S5_CH), (2 * S5_CH) ** -0.5),
        's5_b_im': nrm((DEPTH, S5_G, S5_N, S5_CH), (2 * S5_CH) ** -0.5),
        's5_c_re': nrm((DEPTH, S5_G, S5_CH, S5_N), S5_N ** -0.5),
        's5_c_im': nrm((DEPTH, S5_G, S5_CH, S5_N), S5_N ** -0.5),
        's5_d': nrm((DEPTH, W_D)),
        's5_gate_w': nrm((DEPTH, W_D, W_D), W_D ** -0.5),
        's5_gate_b': nrm((DEPTH, W_D), 0.01),
        'g_group': gain((DEPTH, MIX_WIDTH)),
        'w_out': nrm((DEPTH, MIX_WIDTH, D_MODEL), MIX_WIDTH ** -0.5),
        'norm_mix_post': gain((DEPTH, D_MODEL)),
        'norm_mem': gain((DEPTH, D_MODEL)),
        'norm_x_pre': gain((DEPTH, D_MODEL)),
        'w_xq': nrm((DEPTH, D_MODEL, W_X), D_MODEL ** -0.5),
        'w_xk': nrm((DEPTH, D_MODEL, W_X), D_MODEL ** -0.5),
        'w_xv': nrm((DEPTH, D_MODEL, W_X), D_MODEL ** -0.5),
        'w_xo': nrm((DEPTH, W_X, D_MODEL), W_X ** -0.5),
        'norm_x_post': gain((DEPTH, D_MODEL)),
        'norm_ffn_pre': gain((DEPTH, D_MODEL)),
        'w_up': nrm((DEPTH, D_MODEL, D_FF), D_MODEL ** -0.5),
        'w_down': nrm((DEPTH, D_FF, D_MODEL), D_FF ** -0.5),
        'norm_ffn_post': gain((DEPTH, D_MODEL)),
    }


def reference(x_prompt, x_sample, cache_sb_k, cache_sb_v, page_table, state_mlstm_c, state_mlstm_n, state_mlstm_m,
              state_conv, state_s5_re, state_s5_im, cache_mem_k, cache_mem_v, mem_prompt,
              norm_mix_pre, w_in, b_mlstm_gates, sb_bias, conv_w, conv_b, conv_ln_g, conv_ln_b,
              s5_a_re, s5_a_im, s5_log_dt, s5_b_re, s5_b_im, s5_c_re, s5_c_im, s5_d, s5_gate_w, s5_gate_b,
              g_group, w_out, norm_mix_post, norm_mem, norm_x_pre, w_xq, w_xk, w_xv, w_xo, norm_x_post,
              norm_ffn_pre, w_up, w_down, norm_ffn_post):
    f32 = jnp.float32
    n_b = x_prompt.shape[0]
    n_db = x_sample.shape[0]
    past = page_table.shape[1] * PAGE_SIZE
    xp = x_prompt
    xs = x_sample
    sbk_p, sbv_p, sbk_s, sbv_s = [], [], [], []
    mc_p, mn_p, mm_p, mc_s, mn_s, mm_s = [], [], [], [], [], []
    cv_p, cv_s = [], []
    sr_p, si_p, sr_s, si_s = [], [], [], []
    mk_p, mv_p = [], []
    for l in range(DEPTH):
        p = {
            'norm_mix_pre': norm_mix_pre[l], 'w_in': w_in[l], 'b_mlstm_gates': b_mlstm_gates[l],
            'sb_bias': sb_bias[l],
            'conv_w': conv_w[l], 'conv_b': conv_b[l], 'conv_ln_g': conv_ln_g[l], 'conv_ln_b': conv_ln_b[l],
            's5_a_re': s5_a_re[l], 's5_a_im': s5_a_im[l], 's5_log_dt': s5_log_dt[l],
            's5_b_re': s5_b_re[l], 's5_b_im': s5_b_im[l], 's5_c_re': s5_c_re[l], 's5_c_im': s5_c_im[l],
            's5_d': s5_d[l], 's5_gate_w': s5_gate_w[l], 's5_gate_b': s5_gate_b[l],
            'g_group': g_group[l], 'w_out': w_out[l], 'norm_mix_post': norm_mix_post[l],
            'norm_x_pre': norm_x_pre[l], 'w_xq': w_xq[l], 'w_xo': w_xo[l], 'norm_x_post': norm_x_post[l],
            'norm_ffn_pre': norm_ffn_pre[l], 'w_up': w_up[l], 'w_down': w_down[l], 'norm_ffn_post': norm_ffn_post[l],
        }
        mk, mv = memory_kv(mem_prompt, norm_mem[l], w_xk[l], w_xv[l])
        zero_mlstm = (jnp.zeros((n_b, H_A, DH_A, DH_A), f32), jnp.zeros((n_b, H_A, DH_A), f32), jnp.zeros((n_b, H_A), f32))
        zero_conv = jnp.zeros((n_b, CONV_W - 1, W_C), xp.dtype)
        zero_s5 = (jnp.zeros((n_b, S5_G, S5_N), f32), jnp.zeros((n_b, S5_G, S5_N), f32))
        xp, kp, vp, mst_p, cnv_p, s5_p = decoder_layer(xp, mk, mv, None, None, zero_mlstm, zero_conv, zero_s5, p)
        pk = cache_sb_k[l][page_table].reshape(n_db, past, H_B, DH_B)
        pv = cache_sb_v[l][page_table].reshape(n_db, past, H_B, DH_B)
        xs, ks_new, vs_new, mst_s, cnv_s, s5_s = decoder_layer(
            xs, cache_mem_k[l], cache_mem_v[l], pk, pv,
            (state_mlstm_c[l], state_mlstm_n[l], state_mlstm_m[l]), state_conv[l],
            (state_s5_re[l], state_s5_im[l]), p)
        sbk_p.append(kp); sbv_p.append(vp); sbk_s.append(ks_new); sbv_s.append(vs_new)
        mc_p.append(mst_p[0]); mn_p.append(mst_p[1]); mm_p.append(mst_p[2])
        mc_s.append(mst_s[0]); mn_s.append(mst_s[1]); mm_s.append(mst_s[2])
        cv_p.append(cnv_p); cv_s.append(cnv_s)
        sr_p.append(s5_p[0]); si_p.append(s5_p[1]); sr_s.append(s5_s[0]); si_s.append(s5_s[1])
        mk_p.append(mk); mv_p.append(mv)
    st = lambda lst: jnp.stack(lst, axis=0)
    return (xp, xs, st(sbk_p), st(sbv_p), st(sbk_s), st(sbv_s),
            st(mc_p), st(mn_p), st(mm_p), st(mc_s), st(mn_s), st(mm_s),
            st(cv_p), st(cv_s), st(sr_p), st(si_p), st(sr_s), st(si_s),
            st(mk_p), st(mv_p))
```

```python
import functools

import jax
import jax.numpy as jnp
from jax import lax
from jax.experimental import pallas as pl
from jax.experimental.pallas import tpu as pltpu

F32 = jnp.float32
BF16 = jnp.bfloat16
EPS = 1e-6

H_A = 4
H_B = 8
CONV_W = 31
S5_CH = 16
S5_N = 64
H_X = 4
MLSTM_CHUNK_MAX = 256
SB_TILE = 256
CONV_TILE = 64
S5_TILE = 256
S5_GROUP_BLOCK = 16

LANES = 128
SUBLANES = 8
MIB = 1024 * 1024

SEC = 1024
QA, KA, VA, OA, QB, KB, VB, CA, CG, UD = range(10)


def _params(semantics, vmem_mib):
    return pltpu.CompilerParams(dimension_semantics=semantics, vmem_limit_bytes=vmem_mib * MIB)


def _row_tile(m, cap, mult=16):
    t = min(m, cap)
    while m % t or t % mult:
        t -= 1
    return t


def _log_sigmoid_pair(z):
    sp = jnp.log1p(jnp.exp(-jnp.abs(z)))
    return jnp.minimum(z, 0.0) - sp, -(jnp.maximum(z, 0.0) + sp)


def _split_bf16(x, terms):
    out = []
    r = x
    for _ in range(terms):
        p = r.astype(BF16)
        out.append(p)
        r = r - p.astype(F32)
    return out


def _dot(a, b):
    return jnp.dot(a, b, preferred_element_type=F32)


def _dot_nt(a, b):
    return lax.dot_general(a, b, (((1,), (1,)), ((), ())), preferred_element_type=F32)


def _dot_tn(a, b):
    return lax.dot_general(a, b, (((0,), (0,)), ((), ())), preferred_element_type=F32)


def _iota2(shape, dim):
    return lax.broadcasted_iota(jnp.int32, shape, dim)


def _rmsnorm_body(x_ref, g_ref, o_ref):
    x = x_ref[...]
    r = lax.rsqrt(jnp.mean(x * x, axis=-1, keepdims=True) + EPS)
    o_ref[...] = ((x * r) * g_ref[...]).astype(o_ref.dtype)


def rmsnorm_bf16(x, g):
    m, d = x.shape
    tr = _row_tile(m, 256)
    return pl.pallas_call(
        _rmsnorm_body,
        out_shape=jax.ShapeDtypeStruct((m, d), BF16),
        grid=(m // tr,),
        in_specs=[pl.BlockSpec((tr, d), lambda i: (i, 0)), pl.BlockSpec((1, d), lambda i: (0, 0))],
        out_specs=pl.BlockSpec((tr, d), lambda i: (i, 0)),
        compiler_params=_params(("parallel",), 32),
        name="rmsnorm",
    )(x, g.reshape(1, d))


def _rmsnorm_gates_body(x_ref, g_ref, wg_ref, o_ref, gates_ref):
    x = x_ref[...]
    r = lax.rsqrt(jnp.mean(x * x, axis=-1, keepdims=True) + EPS)
    h = ((x * r) * g_ref[...]).astype(BF16)
    o_ref[...] = h
    gates_ref[...] = _dot(h, wg_ref[...])


def rmsnorm_gates(x, g, wg):
    m, d = x.shape
    tr = _row_tile(m, 256)
    return pl.pallas_call(
        _rmsnorm_gates_body,
        out_shape=(jax.ShapeDtypeStruct((m, d), BF16), jax.ShapeDtypeStruct((m, LANES), F32)),
        grid=(m // tr,),
        in_specs=[pl.BlockSpec((tr, d), lambda i: (i, 0)), pl.BlockSpec((1, d), lambda i: (0, 0)),
                  pl.BlockSpec((d, LANES), lambda i: (0, 0))],
        out_specs=(pl.BlockSpec((tr, d), lambda i: (i, 0)), pl.BlockSpec((tr, LANES), lambda i: (i, 0))),
        compiler_params=_params(("parallel",), 32),
        name="rmsnorm_gates",
    )(x, g.reshape(1, d), wg)


def _matmul_body(a_ref, w_ref, o_ref):
    o_ref[...] = _dot(a_ref[...], w_ref[...]).astype(o_ref.dtype)


def matmul(a, w, out_dtype, tm_cap=1024, tn_cap=1024):
    m, k = a.shape
    n = w.shape[1]
    tm = _row_tile(m, tm_cap)
    tn = _row_tile(n, tn_cap, LANES)
    return pl.pallas_call(
        _matmul_body,
        out_shape=jax.ShapeDtypeStruct((m, n), out_dtype),
        grid=(m // tm, n // tn),
        in_specs=[pl.BlockSpec((tm, k), lambda i, j: (i, 0)), pl.BlockSpec((k, tn), lambda i, j: (0, j))],
        out_specs=pl.BlockSpec((tm, tn), lambda i, j: (i, j)),
        compiler_params=_params(("parallel", "arbitrary"), 56),
        name="matmul",
    )(a, w)


def _norm_residual(y, g_ref, res_ref, o_ref):
    r = lax.rsqrt(jnp.mean(y * y, axis=-1, keepdims=True) + EPS)
    o_ref[...] = res_ref[...] + (y * r) * g_ref[...]


def _matmul_norm_res_body(a_ref, w_ref, g_ref, res_ref, o_ref, acc_ref, *, nk):
    k = pl.program_id(1)

    @pl.when(k == 0)
    def _():
        acc_ref[...] = jnp.zeros_like(acc_ref)

    acc_ref[...] += _dot(a_ref[...].astype(BF16), w_ref[...])

    @pl.when(k == nk - 1)
    def _():
        _norm_residual(acc_ref[...], g_ref, res_ref, o_ref)


def matmul_norm_res(a, w, g, res, tm_cap=512, tk_cap=512):
    m, kdim = a.shape
    n = w.shape[1]
    tm = _row_tile(m, tm_cap)
    tk = _row_tile(kdim, tk_cap, LANES)
    nk = kdim // tk
    once = pl.Buffered(1)
    return pl.pallas_call(
        functools.partial(_matmul_norm_res_body, nk=nk),
        out_shape=jax.ShapeDtypeStruct((m, n), F32),
        grid=(m // tm, nk),
        in_specs=[pl.BlockSpec((tm, tk), lambda i, k: (i, k)),
                  pl.BlockSpec((tk, n), lambda i, k: (k, 0)),
                  pl.BlockSpec((1, n), lambda i, k: (0, 0)),
                  pl.BlockSpec((tm, n), lambda i, k: (i, 0), pipeline_mode=once)],
        out_specs=pl.BlockSpec((tm, n), lambda i, k: (i, 0), pipeline_mode=once),
        scratch_shapes=[pltpu.VMEM((tm, n), F32)],
        compiler_params=_params(("parallel", "arbitrary"), 56),
        name="matmul_norm_res",
    )(a, w, g.reshape(1, n), res)


def _ffn_body(h_ref, wu_ref, wd_ref, g_ref, res_ref, o_ref, acc_ref, *, nf):
    f = pl.program_id(1)

    @pl.when(f == 0)
    def _():
        acc_ref[...] = jnp.zeros_like(acc_ref)

    u = _dot(h_ref[...], wu_ref[...])
    u = jnp.square(jnp.maximum(u, 0.0)).astype(BF16)
    acc_ref[...] += _dot(u, wd_ref[...])

    @pl.when(f == nf - 1)
    def _():
        _norm_residual(acc_ref[...], g_ref, res_ref, o_ref)


def ffn(h, w_up, w_down, g, res, tm_cap=512, tf=512):
    m, d = h.shape
    dff = w_up.shape[1]
    tm = _row_tile(m, tm_cap)
    nf = dff // tf
    once = pl.Buffered(1)
    return pl.pallas_call(
        functools.partial(_ffn_body, nf=nf),
        out_shape=jax.ShapeDtypeStruct((m, d), F32),
        grid=(m // tm, nf),
        in_specs=[pl.BlockSpec((tm, d), lambda i, f: (i, 0), pipeline_mode=once),
                  pl.BlockSpec((d, tf), lambda i, f: (0, f)),
                  pl.BlockSpec((tf, d), lambda i, f: (f, 0)),
                  pl.BlockSpec((1, d), lambda i, f: (0, 0)),
                  pl.BlockSpec((tm, d), lambda i, f: (i, 0), pipeline_mode=once)],
        out_specs=pl.BlockSpec((tm, d), lambda i, f: (i, 0), pipeline_mode=once),
        scratch_shapes=[pltpu.VMEM((tm, d), F32)],
        compiler_params=_params(("parallel", "arbitrary"), 58),
        name="ffn",
    )(h, w_up, w_down, g.reshape(1, d), res)


def _group_norm_body(a_ref, b_ref, c_ref, d_ref, g_ref, o_ref):
    w = a_ref.shape[1]
    for i, ref in enumerate((a_ref, b_ref, c_ref, d_ref)):
        y = ref[...]
        r = lax.rsqrt(jnp.mean(y * y, axis=-1, keepdims=True) + EPS)
        o_ref[:, i * w:(i + 1) * w] = ((y * r) * g_ref[:, i * w:(i + 1) * w]).astype(o_ref.dtype)


def group_norm_concat(ya, yb, yc, yd, g):
    m, w = ya.shape
    tr = _row_tile(m, 256)
    spec = pl.BlockSpec((tr, w), lambda i: (i, 0))
    return pl.pallas_call(
        _group_norm_body,
        out_shape=jax.ShapeDtypeStruct((m, 4 * w), BF16),
        grid=(m // tr,),
        in_specs=[spec, spec, spec, spec, pl.BlockSpec((1, 4 * w), lambda i: (0, 0))],
        out_specs=pl.BlockSpec((tr, 4 * w), lambda i: (i, 0)),
        compiler_params=_params(("parallel",), 32),
        name="group_norm_concat",
    )(ya, yb, yc, yd, g.reshape(1, 4 * w))


def _mlstm_body(q_ref, k_ref, v_ref, o_ref, gates_ref, bias_ref, c0_ref, n0_ref, m0_ref,
                y_ref, c_ref, n_ref, m_ref, *, cs, dh):
    step = pl.program_id(1)

    @pl.when(step == 0)
    def _():
        c_ref[...] = c0_ref[...]
        n_ref[...] = n0_ref[...]
        m_ref[...] = m0_ref[...]

    g = gates_ref[...] + bias_ref[...]
    col = _iota2(g.shape, 1)
    x = jnp.where(col >= H_A, _log_sigmoid_pair(g)[0], g)
    eye = jnp.where(_iota2((LANES, LANES), 0) == _iota2((LANES, LANES), 1), 1.0, 0.0).astype(BF16)
    rr = _iota2((cs, cs), 0)
    cc = _iota2((cs, cs), 1)
    lower = jnp.where(rr >= cc, 1.0, 0.0).astype(BF16)
    upper = jnp.where(rr <= cc, 1.0, 0.0).astype(BF16)
    xs = _split_bf16(x, 3)
    x_rows = sum(_dot_nt(eye, p) for p in xs)
    fc_cols = sum(_dot(lower, p) for p in xs)
    fc_rows = sum(_dot(p, upper) for p in _split_bf16(x_rows, 3))
    causal = rr >= cc
    scale = dh ** -0.5

    for h in range(H_A):
        sl = slice(h * dh, (h + 1) * dh)
        q = q_ref[:, sl]
        k = k_ref[:, sl] * scale
        v = v_ref[:, sl]
        qb = q.astype(BF16)
        kb = k.astype(BF16)
        i_col = x[:, h:h + 1]
        i_row = x_rows[h:h + 1, :]
        f_col = fc_cols[:, H_A + h:H_A + h + 1]
        f_row = fc_rows[H_A + h:H_A + h + 1, :]
        m_prev = m_ref[0, :, h:h + 1]
        c_prev = c_ref[0, h]
        n_prev = n_ref[0, h:h + 1, :]

        dmat = jnp.where(causal, f_col - f_row + i_row, -jnp.inf)
        carry_log = f_col + m_prev
        m_t = jnp.maximum(jnp.max(dmat, axis=-1, keepdims=True), carry_log)
        s = _dot_nt(qb, kb) * jnp.exp(dmat - m_t)
        w_prev = jnp.exp(carry_log - m_t)
        num = _dot(s.astype(BF16), v.astype(BF16)) + w_prev * _dot_nt(qb, c_prev.astype(BF16))
        den = jnp.sum(s, axis=-1, keepdims=True) + w_prev * jnp.sum(q * n_prev, axis=-1, keepdims=True)
        hid = num / jnp.maximum(jnp.abs(den), jnp.exp(-m_t))
        y_ref[:, sl] = jax.nn.sigmoid(o_ref[:, sl]) * hid

        m_new = m_t[cs - 1:cs, :]
        f_last = f_col[cs - 1:cs, :]
        decay = jnp.exp(f_last + m_prev - m_new)
        w_end = jnp.exp(f_last - f_col + i_col - m_new)
        c_ref[0, h] = decay * c_prev + _dot_tn((w_end * v).astype(BF16), kb)
        n_ref[0, h:h + 1, :] = decay * n_prev + jnp.sum(w_end * k, axis=0, keepdims=True)
        m_ref[0, :, h:h + 1] = m_new


def mlstm(proj, gates, bias, c0, n0, m0, bsz, length):
    dh = SEC // H_A
    cs = _row_tile(length, MLSTM_CHUNK_MAX, SUBLANES)
    nc = length // cs
    rows = bsz * length

    def sec(s):
        return pl.BlockSpec((cs, SEC), lambda b, c, s=s: (b * nc + c, s))

    state_c = pl.BlockSpec((1, H_A, dh, dh), lambda b, c: (b, 0, 0, 0))
    state_n = pl.BlockSpec((1, H_A, dh), lambda b, c: (b, 0, 0))
    state_m = pl.BlockSpec((1, 1, H_A), lambda b, c: (b, 0, 0))
    y, c, n, m = pl.pallas_call(
        functools.partial(_mlstm_body, cs=cs, dh=dh),
        out_shape=(jax.ShapeDtypeStruct((rows, SEC), F32),
                   jax.ShapeDtypeStruct((bsz, H_A, dh, dh), F32),
                   jax.ShapeDtypeStruct((bsz, H_A, dh), F32),
                   jax.ShapeDtypeStruct((bsz, 1, H_A), F32)),
        grid=(bsz, nc),
        in_specs=[sec(QA), sec(KA), sec(VA), sec(OA),
                  pl.BlockSpec((cs, LANES), lambda b, c: (b * nc + c, 0)),
                  pl.BlockSpec((1, LANES), lambda b, c: (0, 0)),
                  state_c, state_n, state_m],
        out_specs=(pl.BlockSpec((cs, SEC), lambda b, c: (b * nc + c, 0)), state_c, state_n, state_m),
        compiler_params=_params(("parallel", "arbitrary"), 48),
        name="mlstm",
    )(proj, proj, proj, proj, gates, bias, c0, n0, m0.reshape(bsz, 1, H_A))
    return y, c, n, m.reshape(bsz, H_A)


def _sb_tile(q, kj, vj, bias, carry, acc, strict_upper, mask):
    z = _dot_nt(q, kj) + bias
    log_b, log_1mb = _log_sigmoid_pair(z)
    if mask is not None:
        log_1mb = jnp.where(mask, log_1mb, 0.0)
    hi, lo = _split_bf16(log_1mb, 2)
    between = _dot(hi, strict_upper) + _dot(lo, strict_upper) + carry
    a = jnp.exp(log_b + between)
    if mask is not None:
        a = jnp.where(mask, a, 0.0)
    acc = acc + _dot(a.astype(BF16), vj)
    carry = carry + jnp.sum(log_1mb, axis=-1, keepdims=True)
    return carry, acc


def _strict_upper(n):
    return jnp.where(_iota2((n, n), 0) > _iota2((n, n), 1), 1.0, 0.0).astype(BF16)


def _sb_prompt_body(bias_ref, q_ref, k_ref, v_ref, y_ref, *, t, dh):
    h = pl.program_id(1)
    i = pl.program_id(2)
    bias = bias_ref[h]
    q = (q_ref[...] * (dh ** -0.5)).astype(BF16)
    su = _strict_upper(t)
    diag_mask = _iota2((t, t), 1) < _iota2((t, t), 0)

    def tile(j, carry, acc, mask):
        start = pl.multiple_of(j * t, t)
        kj = k_ref[pl.ds(start, t), :].astype(BF16)
        vj = v_ref[pl.ds(start, t), :].astype(BF16)
        return _sb_tile(q, kj, vj, bias, carry, acc, su, mask)

    carry, acc = tile(i, jnp.zeros((t, 1), F32), jnp.zeros((t, dh), F32), diag_mask)

    def body(n, state):
        return tile(i - 1 - n, state[0], state[1], None)

    carry, acc = lax.fori_loop(0, i, body, (carry, acc))
    y_ref[...] = acc


def sb_prompt(proj, sb_bias, bsz, length):
    dh = SEC // H_B
    t = _row_tile(length, SB_TILE, SUBLANES)
    nq = length // t
    cpb = SEC // dh
    return pl.pallas_call(
        functools.partial(_sb_prompt_body, t=t, dh=dh),
        out_shape=jax.ShapeDtypeStruct((bsz * length, SEC), F32),
        grid=(bsz, H_B, nq),
        in_specs=[pl.BlockSpec(memory_space=pltpu.SMEM),
                  pl.BlockSpec((t, dh), lambda b, h, i: (b * nq + i, QB * cpb + h)),
                  pl.BlockSpec((length, dh), lambda b, h, i: (b, KB * cpb + h)),
                  pl.BlockSpec((length, dh), lambda b, h, i: (b, VB * cpb + h))],
        out_specs=pl.BlockSpec((t, dh), lambda b, h, i: (b * nq + i, h)),
        compiler_params=_params(("parallel", "parallel", "arbitrary"), 32),
        name="sb_prompt",
    )(sb_bias, proj, proj, proj)


def _sb_sample_body(pt_ref, q_ref, kn_ref, vn_ref, bias_ref, kp_ref, vp_ref, y_ref,
                    q2_ref, acc_ref, carry_ref, *, lq, dh, page):
    del pt_ref
    j = pl.program_id(1)
    rows = H_B * lq
    cols = page * H_B
    bias = bias_ref[...]

    @pl.when(j == 0)
    def _():
        q = q_ref[...] * (dh ** -0.5)
        q2 = jnp.concatenate([q[:, h * dh:(h + 1) * dh] for h in range(H_B)], axis=0).astype(BF16)
        q2_ref[...] = q2
        q_wide = jnp.concatenate([q2] * H_B, axis=1)
        same_head = (_iota2((rows, SEC), 0) // lq) == (_iota2((rows, SEC), 1) // dh)
        q_bd = jnp.where(same_head, q_wide, jnp.zeros_like(q_wide))
        fill = jnp.zeros((LANES - lq, SEC), F32)
        kn = jnp.concatenate([kn_ref[...], fill], axis=0).astype(BF16)
        vn = jnp.concatenate([vn_ref[...], fill], axis=0).astype(BF16)
        z = _dot_nt(q_bd, kn) + bias
        log_b, log_1mb = _log_sigmoid_pair(z)
        mask = _iota2((rows, LANES), 1) < (_iota2((rows, LANES), 0) % lq)
        log_1mb = jnp.where(mask, log_1mb, 0.0)
        hi, lo = _split_bf16(log_1mb, 2)
        su = _strict_upper(LANES)
        a = jnp.where(mask, jnp.exp(log_b + _dot(hi, su) + _dot(lo, su)), 0.0)
        full = _dot(a.astype(BF16), vn)
        acc_ref[...] = jnp.concatenate(
            [full[h * lq:(h + 1) * lq, h * dh:(h + 1) * dh] for h in range(H_B)], axis=0)
        carry_ref[...] = jnp.sum(log_1mb, axis=-1, keepdims=True)

    k2 = kp_ref[...].reshape(cols, dh).astype(BF16)
    v2 = vp_ref[...].reshape(cols, dh).astype(BF16)
    z = _dot_nt(q2_ref[...], k2) + bias
    valid = (_iota2((rows, cols), 1) % H_B) == (_iota2((rows, cols), 0) // lq)
    log_b, log_1mb = _log_sigmoid_pair(z)
    log_1mb = jnp.where(valid, log_1mb, 0.0)
    nblk = cols // LANES
    blocks = [log_1mb[:, c * LANES:(c + 1) * LANES] for c in range(nblk)]
    stacked = jnp.concatenate(blocks, axis=0)
    hi, lo = _split_bf16(stacked, 2)
    su = _strict_upper(LANES)
    within = _dot(hi, su) + _dot(lo, su)
    carry = carry_ref[...]
    between = [None] * nblk
    for c in reversed(range(nblk)):
        between[c] = within[c * rows:(c + 1) * rows, :] + carry
        carry = carry + jnp.sum(blocks[c], axis=-1, keepdims=True)
    between = jnp.concatenate(between, axis=1)
    a = jnp.where(valid, jnp.exp(log_b + between), 0.0)
    acc_ref[...] += _dot(a.astype(BF16), v2)
    carry_ref[...] = carry

    @pl.when(j == pl.num_programs(1) - 1)
    def _():
        acc = acc_ref[...]
        y_ref[...] = jnp.concatenate([acc[h * lq:(h + 1) * lq, :] for h in range(H_B)], axis=1)


def sb_sample(proj, sb_bias, cache_k, cache_v, page_table, layer, bsz, length):
    dh = SEC // H_B
    n_pages = page_table.shape[1]
    page = cache_k.shape[2]
    rows = H_B * length
    bias_rows = jnp.repeat(sb_bias.astype(F32), length).reshape(rows, 1)

    def new_sec(s):
        return pl.BlockSpec((length, SEC), lambda b, j, pt, s=s: (b, s))

    def past(b, j, pt):
        return (layer, pt[b, n_pages - 1 - j], 0, 0, 0)

    page_spec = pl.BlockSpec((None, None, page, H_B, dh), past)
    grid_spec = pltpu.PrefetchScalarGridSpec(
        num_scalar_prefetch=1,
        grid=(bsz, n_pages),
        in_specs=[new_sec(QB), new_sec(KB), new_sec(VB),
                  pl.BlockSpec((rows, 1), lambda b, j, pt: (0, 0)),
                  page_spec, page_spec],
        out_specs=pl.BlockSpec((length, SEC), lambda b, j, pt: (b, 0)),
        scratch_shapes=[pltpu.VMEM((rows, dh), BF16), pltpu.VMEM((rows, dh), F32),
                        pltpu.VMEM((rows, 1), F32)],
    )
    return pl.pallas_call(
        functools.partial(_sb_sample_body, lq=length, dh=dh, page=page),
        out_shape=jax.ShapeDtypeStruct((bsz * length, SEC), F32),
        grid_spec=grid_spec,
        compiler_params=_params(("parallel", "arbitrary"), 32),
        name="sb_sample",
    )(page_table, proj, proj, proj, bias_rows, cache_k, cache_v)


def _conv_body(a_ref, g_ref, buf_ref, w_ref, b_ref, lg_ref, lb_ref, y_ref, new_ref, ext_ref, *, tl):
    hist = CONV_W - 1
    pad = 32 - hist
    step = pl.program_id(1)

    @pl.when(step == 0)
    def _():
        ext_ref[0:SUBLANES, :] = jnp.zeros((SUBLANES, SEC), F32)
        ext_ref[pad:32, :] = buf_ref[0]

    ext_ref[32:32 + tl, :] = a_ref[...] * jax.nn.sigmoid(g_ref[...])
    cols = []
    for c in range(SEC // LANES):
        lanes = slice(c * LANES, (c + 1) * LANES)
        slab = ext_ref[:, lanes]
        acc = jnp.zeros((tl, LANES), F32)
        for j in range(CONV_W):
            acc = acc + w_ref[j:j + 1, lanes] * slab[pad + j:pad + j + tl, :]
        cols.append(acc)
    y = jnp.concatenate(cols, axis=1) + b_ref[...]
    yc = y - jnp.mean(y, axis=-1, keepdims=True)
    yn = yc * lax.rsqrt(jnp.mean(yc * yc, axis=-1, keepdims=True) + EPS)
    yn = yn * lg_ref[...] + lb_ref[...]
    y_ref[...] = yn * jax.nn.sigmoid(yn)
    new_ref[0] = ext_ref[tl + pad:tl + 32, :]
    if tl >= 32:
        ext_ref[0:32, :] = ext_ref[tl:tl + 32, :]


def conformer_conv(proj, buf, w, b, ln_g, ln_b, bsz, length):
    tl = _row_tile(length, CONV_TILE, SUBLANES)
    nt = length // tl
    assert nt == 1 or tl >= 32
    hist = CONV_W - 1
    vec = pl.BlockSpec((1, SEC), lambda bb, t: (0, 0))
    return pl.pallas_call(
        functools.partial(_conv_body, tl=tl),
        out_shape=(jax.ShapeDtypeStruct((bsz * length, SEC), F32),
                   jax.ShapeDtypeStruct((bsz, hist, SEC), F32)),
        grid=(bsz, nt),
        in_specs=[pl.BlockSpec((tl, SEC), lambda bb, t: (bb * nt + t, CA)),
                  pl.BlockSpec((tl, SEC), lambda bb, t: (bb * nt + t, CG)),
                  pl.BlockSpec((1, hist, SEC), lambda bb, t: (bb, 0, 0)),
                  pl.BlockSpec((CONV_W, SEC), lambda bb, t: (0, 0)),
                  vec, vec, vec],
        out_specs=(pl.BlockSpec((tl, SEC), lambda bb, t: (bb * nt + t, 0)),
                   pl.BlockSpec((1, hist, SEC), lambda bb, t: (bb, 0, 0))),
        scratch_shapes=[pltpu.VMEM((32 + tl, SEC), F32)],
        compiler_params=_params(("parallel", "arbitrary"), 32),
        name="conformer_conv",
    )(proj, proj, buf, w, b.reshape(1, SEC), ln_g.reshape(1, SEC), ln_b.reshape(1, SEC))


def _s5_body(u_ref, h0r_ref, h0i_ref, ar_ref, ai_ref, br_ref, bi_ref, cr_ref, ci_ref, d_ref,
             gw_ref, gb_ref, y_ref, hr_ref, hi_ref, sr_ref, si_ref, *, tl):
    step = pl.program_id(1)
    nblk = br_ref.shape[0]
    cin = br_ref.shape[1]
    nst = br_ref.shape[2]

    @pl.when(step == 0)
    def _():
        hr_ref[...] = h0r_ref[...]
        hi_ref[...] = h0i_ref[...]

    u = u_ref[...]
    ub = u.astype(BF16)
    for blk in range(nblk):
        ublk = ub[:, blk * cin:(blk + 1) * cin]
        sr_ref[:, blk * nst:(blk + 1) * nst] = _dot(ublk, br_ref[blk])
        si_ref[:, blk * nst:(blk + 1) * nst] = _dot(ublk, bi_ref[blk])

    for blk in range(nblk):
        lanes = slice(blk * nst, (blk + 1) * nst)
        ar = ar_ref[:, lanes]
        ai = ai_ref[:, lanes]

        def scan_step(t, state, lanes=lanes, ar=ar, ai=ai):
            hr, hi = state
            row = pl.ds(t, 1)
            nr = ar * hr - ai * hi + sr_ref[row, lanes]
            ni = ar * hi + ai * hr + si_ref[row, lanes]
            sr_ref[row, lanes] = nr
            si_ref[row, lanes] = ni
            return nr, ni

        hr, hi = lax.fori_loop(0, tl, scan_step, (hr_ref[0, :, lanes], hi_ref[0, :, lanes]),
                               unroll=SUBLANES)
        hr_ref[0, :, lanes] = hr
        hi_ref[0, :, lanes] = hi

    ys = []
    for blk in range(nblk):
        lanes = slice(blk * nst, (blk + 1) * nst)
        ys.append(_dot(sr_ref[:, lanes].astype(BF16), cr_ref[blk])
                  - _dot(si_ref[:, lanes].astype(BF16), ci_ref[blk]))
    y = jnp.concatenate(ys, axis=1) + d_ref[...] * u
    g = jax.nn.gelu(y)
    gate = _dot(g.astype(BF16), gw_ref[...]) + gb_ref[...]
    y_ref[...] = g * jax.nn.sigmoid(gate)


def s5_discretize(a_re, a_im, log_dt, b_re, b_im, c_re, c_im):
    ngrp, nst = a_re.shape
    dt = jnp.exp(log_dt.astype(F32))[:, None]
    ar = a_re.astype(F32)
    ai = a_im.astype(F32)
    mag = jnp.exp(ar * dt)
    abar_re = mag * jnp.cos(ai * dt)
    abar_im = mag * jnp.sin(ai * dt)
    zr = abar_re - 1.0
    zi = abar_im
    den = ar * ar + ai * ai
    coef_re = (zr * ar + zi * ai) / den
    coef_im = (zi * ar - zr * ai) / den
    br = b_re.astype(F32)
    bi = b_im.astype(F32)
    bbar_re = coef_re[..., None] * br - coef_im[..., None] * bi
    bbar_im = coef_re[..., None] * bi + coef_im[..., None] * br
    gb = S5_GROUP_BLOCK
    nblk = ngrp // gb
    eye = jnp.eye(gb, dtype=F32)

    def in_blocks(bbar):
        t = bbar.reshape(nblk, gb, nst, S5_CH)
        return jnp.einsum('bgnc,gh->bgchn', t, eye).reshape(nblk, gb * S5_CH, gb * nst).astype(BF16)

    def out_blocks(c):
        t = c.astype(F32).reshape(nblk, gb, S5_CH, nst)
        return jnp.einsum('bgcn,gh->bgnhc', t, eye).reshape(nblk, gb * nst, gb * S5_CH).astype(BF16)

    return (abar_re.reshape(1, ngrp * nst), abar_im.reshape(1, ngrp * nst),
            in_blocks(bbar_re), in_blocks(bbar_im), out_blocks(c_re), out_blocks(c_im))


def s5(proj, h0_re, h0_im, disc, d, gate_w, gate_b, bsz, length):
    abar_re, abar_im, b_re, b_im, c_re, c_im = disc
    nstate = abar_re.shape[1]
    tl = _row_tile(length, S5_TILE, SUBLANES)
    nt = length // tl
    state = pl.BlockSpec((1, 1, nstate), lambda b, t: (b, 0, 0))

    def whole(a):
        return pl.BlockSpec(a.shape, lambda b, t, nd=a.ndim: (0,) * nd)

    d2 = d.reshape(1, SEC)
    gb2 = gate_b.reshape(1, SEC)
    y, hr, hi = pl.pallas_call(
        functools.partial(_s5_body, tl=tl),
        out_shape=(jax.ShapeDtypeStruct((bsz * length, SEC), F32),
                   jax.ShapeDtypeStruct((bsz, 1, nstate), F32),
                   jax.ShapeDtypeStruct((bsz, 1, nstate), F32)),
        grid=(bsz, nt),
        in_specs=[pl.BlockSpec((tl, SEC), lambda b, t: (b * nt + t, UD)), state, state,
                  whole(abar_re), whole(abar_im), whole(b_re), whole(b_im), whole(c_re), whole(c_im),
                  whole(d2), whole(gate_w), whole(gb2)],
        out_specs=(pl.BlockSpec((tl, SEC), lambda b, t: (b * nt + t, 0)), state, state),
        scratch_shapes=[pltpu.VMEM((tl, nstate), F32), pltpu.VMEM((tl, nstate), F32)],
        compiler_params=_params(("parallel", "arbitrary"), 48),
        name="s5",
    )(proj, h0_re.reshape(bsz, 1, nstate), h0_im.reshape(bsz, 1, nstate),
      abar_re, abar_im, b_re, b_im, c_re, c_im, d2, gate_w, gb2)
    return y, hr, hi


def _cross_body(q_ref, k_ref, v_ref, o_ref, *, dh):
    scale = dh ** -0.5
    for h in range(H_X):
        sl = slice(h * dh, (h + 1) * dh)
        s = _dot_nt(q_ref[:, sl].astype(BF16), k_ref[:, sl].astype(BF16)) * scale
        e = jnp.exp(s - jnp.max(s, axis=-1, keepdims=True))
        p = e / jnp.sum(e, axis=-1, keepdims=True)
        o_ref[:, sl] = _dot(p.astype(BF16), v_ref[:, sl].astype(BF16)).astype(o_ref.dtype)


def cross_attend(q, mem_k, mem_v, k_col, v_col, bsz, length, n_mem):
    wx = q.shape[1]
    tq = _row_tile(length, 512, SUBLANES)
    nq = length // tq
    return pl.pallas_call(
        functools.partial(_cross_body, dh=wx // H_X),
        out_shape=jax.ShapeDtypeStruct((bsz * length, wx), F32),
        grid=(bsz, nq),
        in_specs=[pl.BlockSpec((tq, wx), lambda b, i: (b * nq + i, 0)),
                  pl.BlockSpec((n_mem, wx), lambda b, i: (b, k_col)),
                  pl.BlockSpec((n_mem, wx), lambda b, i: (b, v_col))],
        out_specs=pl.BlockSpec((tq, wx), lambda b, i: (b * nq + i, 0)),
        compiler_params=_params(("parallel", "arbitrary"), 32),
        name="cross_attend",
    )(q, mem_k, mem_v)


def _decoder_layer(x, bsz, length, mem_k, mem_v, k_col, v_col, n_mem, sb_fn, mlstm_state, conv_buf,
                   s5_state, p):
    h, gates = rmsnorm_gates(x, p['norm_mix_pre'], p['w_gates'])
    proj = matmul(h, p['w_in'], F32)
    y_a, c_new, n_new, m_new = mlstm(proj, gates, p['gate_bias'], *mlstm_state, bsz, length)
    y_b = sb_fn(proj)
    y_c, conv_new = conformer_conv(proj, conv_buf, p['conv_w'], p['conv_b'], p['conv_ln_g'],
                                   p['conv_ln_b'], bsz, length)
    y_d, s5_re, s5_im = s5(proj, s5_state[0], s5_state[1], p['s5_disc'], p['s5_d'], p['s5_gate_w'],
                           p['s5_gate_b'], bsz, length)
    mixed = group_norm_concat(y_a, y_b, y_c, y_d, p['g_group'])
    x = matmul_norm_res(mixed, p['w_out'], p['norm_mix_post'], x)
    q = matmul(rmsnorm_bf16(x, p['norm_x_pre']), p['w_xq'], F32)
    o = cross_attend(q, mem_k, mem_v, k_col, v_col, bsz, length, n_mem)
    x = matmul_norm_res(o, p['w_xo'], p['norm_x_post'], x)
    x = ffn(rmsnorm_bf16(x, p['norm_ffn_pre']), p['w_up'], p['w_down'], p['norm_ffn_post'], x)
    dh_b = SEC // H_B
    k_new = proj[:, KB * SEC:(KB + 1) * SEC].reshape(bsz, length, H_B, dh_b)
    v_new = proj[:, VB * SEC:(VB + 1) * SEC].reshape(bsz, length, H_B, dh_b)
    return x, k_new, v_new, (c_new, n_new, m_new), conv_new, (s5_re, s5_im)


def kernel(x_prompt, x_sample, cache_sb_k, cache_sb_v, page_table, state_mlstm_c, state_mlstm_n, state_mlstm_m, state_conv, state_s5_re, state_s5_im, cache_mem_k, cache_mem_v, mem_prompt, norm_mix_pre, w_in, b_mlstm_gates, sb_bias, conv_w, conv_b, conv_ln_g, conv_ln_b, s5_a_re, s5_a_im, s5_log_dt, s5_b_re, s5_b_im, s5_c_re, s5_c_im, s5_d, s5_gate_w, s5_gate_b, g_group, w_out, norm_mix_post, norm_mem, norm_x_pre, w_xq, w_xk, w_xv, w_xo, norm_x_post, norm_ffn_pre, w_up, w_down, norm_ffn_post):
    n_b, seq, d_model = x_prompt.shape
    n_db, dec_seq, _ = x_sample.shape
    depth = w_in.shape[0]
    n_mem = mem_prompt.shape[1]
    w_x = w_xq.shape[2]
    dh_a = SEC // H_A
    ngrp = s5_a_re.shape[1]
    n_gate = 2 * H_A
    gate_lo = 4 * SEC

    xp = x_prompt.reshape(n_b * seq, d_model)
    xs = x_sample.reshape(n_db * dec_seq, d_model)
    mem2d = mem_prompt.reshape(n_b * n_mem, d_model)
    outs = [[] for _ in range(18)]
    for l in range(depth):
        wl = w_in[l]
        p = {
            'norm_mix_pre': norm_mix_pre[l],
            'w_in': jnp.concatenate([wl[:, :gate_lo], wl[:, gate_lo + n_gate:]], axis=1).astype(BF16),
            'w_gates': jnp.pad(wl[:, gate_lo:gate_lo + n_gate], ((0, 0), (0, LANES - n_gate))).astype(BF16),
            'gate_bias': jnp.pad(b_mlstm_gates[l].astype(F32), (0, LANES - n_gate)).reshape(1, LANES),
            'conv_w': conv_w[l], 'conv_b': conv_b[l], 'conv_ln_g': conv_ln_g[l], 'conv_ln_b': conv_ln_b[l],
            's5_disc': s5_discretize(s5_a_re[l], s5_a_im[l], s5_log_dt[l], s5_b_re[l], s5_b_im[l],
                                     s5_c_re[l], s5_c_im[l]),
            's5_d': s5_d[l], 's5_gate_w': s5_gate_w[l].astype(BF16), 's5_gate_b': s5_gate_b[l],
            'g_group': g_group[l], 'w_out': w_out[l].astype(BF16), 'norm_mix_post': norm_mix_post[l],
            'norm_x_pre': norm_x_pre[l], 'w_xq': w_xq[l].astype(BF16), 'w_xo': w_xo[l].astype(BF16),
            'norm_x_post': norm_x_post[l], 'norm_ffn_pre': norm_ffn_pre[l],
            'w_up': w_up[l].astype(BF16), 'w_down': w_down[l].astype(BF16),
            'norm_ffn_post': norm_ffn_post[l],
        }
        w_kv = jnp.concatenate([w_xk[l], w_xv[l]], axis=1).astype(BF16)
        mem_kv = matmul(rmsnorm_bf16(mem2d, norm_mem[l]), w_kv, F32, tm_cap=512)
        zero_mlstm = (jnp.zeros((n_b, H_A, dh_a, dh_a), F32), jnp.zeros((n_b, H_A, dh_a), F32),
                      jnp.zeros((n_b, H_A), F32))
        zero_s5 = (jnp.zeros((n_b, ngrp, S5_N), F32), jnp.zeros((n_b, ngrp, S5_N), F32))
        sb_p = functools.partial(sb_prompt, sb_bias=sb_bias[l], bsz=n_b, length=seq)
        xp, kp, vp, mst_p, cnv_p, s5_p = _decoder_layer(
            xp, n_b, seq, mem_kv, mem_kv, 0, 1, n_mem, sb_p, zero_mlstm,
            jnp.zeros((n_b, CONV_W - 1, SEC), F32), zero_s5, p)
        sb_s = functools.partial(sb_sample, sb_bias=sb_bias[l], cache_k=cache_sb_k, cache_v=cache_sb_v,
                                 page_table=page_table, layer=l, bsz=n_db, length=dec_seq)
        xs, ks, vs, mst_s, cnv_s, s5_s = _decoder_layer(
            xs, n_db, dec_seq, cache_mem_k[l].reshape(n_db * n_mem, w_x),
            cache_mem_v[l].reshape(n_db * n_mem, w_x), 0, 0, n_mem, sb_s,
            (state_mlstm_c[l], state_mlstm_n[l], state_mlstm_m[l]), state_conv[l],
            (state_s5_re[l], state_s5_im[l]), p)
        mk = mem_kv[:, :w_x].reshape(n_b, n_mem, H_X, w_x // H_X)
        mv = mem_kv[:, w_x:].reshape(n_b, n_mem, H_X, w_x // H_X)
        s5_shape = lambda a: a.reshape(a.shape[0], ngrp, S5_N)
        layer_out = (kp, vp, ks, vs, mst_p[0], mst_p[1], mst_p[2], mst_s[0], mst_s[1], mst_s[2],
                     cnv_p, cnv_s, s5_shape(s5_p[0]), s5_shape(s5_p[1]), s5_shape(s5_s[0]),
                     s5_shape(s5_s[1]), mk, mv)
        for acc, val in zip(outs, layer_out):
            acc.append(val)
    stacked = tuple(jnp.stack(vals, axis=0) for vals in outs)
    return (xp.reshape(n_b, seq, d_model), xs.reshape(n_db, dec_seq, d_model)) + stacked
```

```python
import functools

import jax
import jax.numpy as jnp
from jax import lax
from jax.experimental import pallas as pl
from jax.experimental.pallas import tpu as pltpu

F32 = jnp.float32
BF16 = jnp.bfloat16
EPS = 1e-6

H_A = 4
H_B = 8
CONV_W = 31
S5_CH = 16
S5_N = 64
H_X = 4
MLSTM_CHUNK_MAX = 256
SB_TILE = 256
SB_HEADS_PER_STEP = 4
SB_PAGES_PER_STEP = 8
CONV_TILE = 64
S5_TILE = 256
S5_GROUP_BLOCK = 16

LANES = 128
SUBLANES = 8
MIB = 1024 * 1024

SEC = 1024
QA, KA, VA, OA, QB, KB, VB, CA, CG, UD = range(10)


def _params(semantics, vmem_mib):
    return pltpu.CompilerParams(dimension_semantics=semantics, vmem_limit_bytes=vmem_mib * MIB)


def _row_tile(m, cap, mult=16):
    t = min(m, cap)
    while m % t or t % mult:
        t -= 1
    return t


def _log_sigmoid_pair(z):
    sp = jnp.log1p(jnp.exp(-jnp.abs(z)))
    return jnp.minimum(z, 0.0) - sp, -(jnp.maximum(z, 0.0) + sp)


def _split_bf16(x, terms):
    out = []
    r = x
    for _ in range(terms):
        p = r.astype(BF16)
        out.append(p)
        r = r - p.astype(F32)
    return out


def _dot(a, b):
    return jnp.dot(a, b, preferred_element_type=F32)


def _dot_nt(a, b):
    return lax.dot_general(a, b, (((1,), (1,)), ((), ())), preferred_element_type=F32)


def _dot_tn(a, b):
    return lax.dot_general(a, b, (((0,), (0,)), ((), ())), preferred_element_type=F32)


def _iota2(shape, dim):
    return lax.broadcasted_iota(jnp.int32, shape, dim)


def _rmsnorm_body(x_ref, g_ref, o_ref):
    x = x_ref[...]
    r = lax.rsqrt(jnp.mean(x * x, axis=-1, keepdims=True) + EPS)
    o_ref[...] = ((x * r) * g_ref[...]).astype(o_ref.dtype)


def rmsnorm_bf16(x, g):
    m, d = x.shape
    tr = _row_tile(m, 256)
    return pl.pallas_call(
        _rmsnorm_body,
        out_shape=jax.ShapeDtypeStruct((m, d), BF16),
        grid=(m // tr,),
        in_specs=[pl.BlockSpec((tr, d), lambda i: (i, 0)), pl.BlockSpec((1, d), lambda i: (0, 0))],
        out_specs=pl.BlockSpec((tr, d), lambda i: (i, 0)),
        compiler_params=_params(("parallel",), 32),
        name="rmsnorm",
    )(x, g.reshape(1, d))


def _rmsnorm_gates_body(x_ref, g_ref, wg_ref, o_ref, gates_ref):
    x = x_ref[...]
    r = lax.rsqrt(jnp.mean(x * x, axis=-1, keepdims=True) + EPS)
    h = ((x * r) * g_ref[...]).astype(BF16)
    o_ref[...] = h
    gates_ref[...] = _dot(h, wg_ref[...])


def rmsnorm_gates(x, g, wg):
    m, d = x.shape
    tr = _row_tile(m, 256)
    return pl.pallas_call(
        _rmsnorm_gates_body,
        out_shape=(jax.ShapeDtypeStruct((m, d), BF16), jax.ShapeDtypeStruct((m, LANES), F32)),
        grid=(m // tr,),
        in_specs=[pl.BlockSpec((tr, d), lambda i: (i, 0)), pl.BlockSpec((1, d), lambda i: (0, 0)),
                  pl.BlockSpec((d, LANES), lambda i: (0, 0))],
        out_specs=(pl.BlockSpec((tr, d), lambda i: (i, 0)), pl.BlockSpec((tr, LANES), lambda i: (i, 0))),
        compiler_params=_params(("parallel",), 32),
        name="rmsnorm_gates",
    )(x, g.reshape(1, d), wg)


def _matmul_body(a_ref, w_ref, o_ref):
    o_ref[...] = _dot(a_ref[...], w_ref[...]).astype(o_ref.dtype)


def matmul(a, w, out_dtype, tm_cap=1024, tn_cap=1024):
    m, k = a.shape
    n = w.shape[1]
    tm = _row_tile(m, tm_cap)
    tn = _row_tile(n, tn_cap, LANES)
    return pl.pallas_call(
        _matmul_body,
        out_shape=jax.ShapeDtypeStruct((m, n), out_dtype),
        grid=(m // tm, n // tn),
        in_specs=[pl.BlockSpec((tm, k), lambda i, j: (i, 0)), pl.BlockSpec((k, tn), lambda i, j: (0, j))],
        out_specs=pl.BlockSpec((tm, tn), lambda i, j: (i, j)),
        compiler_params=_params(("parallel", "arbitrary"), 56),
        name="matmul",
    )(a, w)


def _norm_residual(y, g_ref, res_ref, o_ref):
    r = lax.rsqrt(jnp.mean(y * y, axis=-1, keepdims=True) + EPS)
    o_ref[...] = res_ref[...] + (y * r) * g_ref[...]


def _matmul_norm_res_body(a_ref, w_ref, g_ref, res_ref, o_ref):
    _norm_residual(_dot(a_ref[...].astype(BF16), w_ref[...]), g_ref, res_ref, o_ref)


def matmul_norm_res(a, w, g, res):
    m, kdim = a.shape
    n = w.shape[1]
    w_bytes = kdim * n * w.dtype.itemsize
    tm = _row_tile(m, 256 if w_bytes > 16 * MIB else 512)
    return pl.pallas_call(
        _matmul_norm_res_body,
        out_shape=jax.ShapeDtypeStruct((m, n), F32),
        grid=(m // tm,),
        in_specs=[pl.BlockSpec((tm, kdim), lambda i: (i, 0)),
                  pl.BlockSpec((kdim, n), lambda i: (0, 0), pipeline_mode=pl.Buffered(1)),
                  pl.BlockSpec((1, n), lambda i: (0, 0)),
                  pl.BlockSpec((tm, n), lambda i: (i, 0))],
        out_specs=pl.BlockSpec((tm, n), lambda i: (i, 0)),
        compiler_params=_params(("parallel",), 60),
        name="matmul_norm_res",
    )(a, w, g.reshape(1, n), res)


def _ffn_body(h_ref, wu_ref, wd_ref, g_ref, res_ref, o_ref, acc_ref, *, nf):
    f = pl.program_id(1)

    @pl.when(f == 0)
    def _():
        acc_ref[...] = jnp.zeros_like(acc_ref)

    u = _dot(h_ref[...], wu_ref[...])
    u = jnp.square(jnp.maximum(u, 0.0)).astype(BF16)
    acc_ref[...] += _dot(u, wd_ref[...])

    @pl.when(f == nf - 1)
    def _():
        _norm_residual(acc_ref[...], g_ref, res_ref, o_ref)


def ffn(h, w_up, w_down, g, res, tm_cap=512, tf=512):
    m, d = h.shape
    dff = w_up.shape[1]
    tm = _row_tile(m, tm_cap)
    nf = dff // tf
    once = pl.Buffered(1)
    return pl.pallas_call(
        functools.partial(_ffn_body, nf=nf),
        out_shape=jax.ShapeDtypeStruct((m, d), F32),
        grid=(m // tm, nf),
        in_specs=[pl.BlockSpec((tm, d), lambda i, f: (i, 0), pipeline_mode=once),
                  pl.BlockSpec((d, tf), lambda i, f: (0, f)),
                  pl.BlockSpec((tf, d), lambda i, f: (f, 0)),
                  pl.BlockSpec((1, d), lambda i, f: (0, 0)),
                  pl.BlockSpec((tm, d), lambda i, f: (i, 0), pipeline_mode=once)],
        out_specs=pl.BlockSpec((tm, d), lambda i, f: (i, 0), pipeline_mode=once),
        scratch_shapes=[pltpu.VMEM((tm, d), F32)],
        compiler_params=_params(("parallel", "arbitrary"), 58),
        name="ffn",
    )(h, w_up, w_down, g.reshape(1, d), res)


def _cast_body(x_ref, o_ref):
    o_ref[...] = x_ref[...].astype(o_ref.dtype)


def cast_bf16(x, layer):
    _, r, c = x.shape
    tr = _row_tile(r, 512)
    tc = _row_tile(c, 4096, LANES)
    return pl.pallas_call(
        _cast_body,
        out_shape=jax.ShapeDtypeStruct((r, c), BF16),
        grid=(r // tr, c // tc),
        in_specs=[pl.BlockSpec((None, tr, tc), lambda i, j: (layer, i, j))],
        out_specs=pl.BlockSpec((tr, tc), lambda i, j: (i, j)),
        compiler_params=_params(("parallel", "parallel"), 40),
        name="cast_bf16",
    )(x)


def _head_split_body(x_ref, o_ref, *, nh, dh):
    rows = x_ref.shape[0]
    for h in range(nh):
        o_ref[pl.ds(h, rows, stride=nh), :] = x_ref[:, h * dh:(h + 1) * dh]


def head_split(proj, section, nh):
    rows = proj.shape[0]
    dh = SEC // nh
    tr = _row_tile(rows, 512, SUBLANES)
    return pl.pallas_call(
        functools.partial(_head_split_body, nh=nh, dh=dh),
        out_shape=jax.ShapeDtypeStruct((rows * nh, dh), F32),
        grid=(rows // tr,),
        in_specs=[pl.BlockSpec((tr, SEC), lambda i: (i, section))],
        out_specs=pl.BlockSpec((tr * nh, dh), lambda i: (i, 0)),
        compiler_params=_params(("parallel",), 32),
        name="head_split",
    )(proj)


def _group_norm_body(a_ref, b_ref, c_ref, d_ref, g_ref, o_ref):
    w = a_ref.shape[1]
    for i, ref in enumerate((a_ref, b_ref, c_ref, d_ref)):
        y = ref[...]
        r = lax.rsqrt(jnp.mean(y * y, axis=-1, keepdims=True) + EPS)
        o_ref[:, i * w:(i + 1) * w] = ((y * r) * g_ref[:, i * w:(i + 1) * w]).astype(o_ref.dtype)


def group_norm_concat(ya, yb, yc, yd, g):
    m, w = ya.shape
    tr = _row_tile(m, 256)
    spec = pl.BlockSpec((tr, w), lambda i: (i, 0))
    return pl.pallas_call(
        _group_norm_body,
        out_shape=jax.ShapeDtypeStruct((m, 4 * w), BF16),
        grid=(m // tr,),
        in_specs=[spec, spec, spec, spec, pl.BlockSpec((1, 4 * w), lambda i: (0, 0))],
        out_specs=pl.BlockSpec((tr, 4 * w), lambda i: (i, 0)),
        compiler_params=_params(("parallel",), 32),
        name="group_norm_concat",
    )(ya, yb, yc, yd, g.reshape(1, 4 * w))


def _mlstm_body(q_ref, k_ref, v_ref, o_ref, gates_ref, bias_ref, c0_ref, n0_ref, m0_ref,
                y_ref, c_ref, n_ref, m_ref, *, cs, dh):
    step = pl.program_id(1)

    @pl.when(step == 0)
    def _():
        c_ref[...] = c0_ref[...]
        n_ref[...] = n0_ref[...]
        m_ref[...] = m0_ref[...]

    g = gates_ref[...] + bias_ref[...]
    col = _iota2(g.shape, 1)
    x = jnp.where(col >= H_A, _log_sigmoid_pair(g)[0], g)
    eye = jnp.where(_iota2((LANES, LANES), 0) == _iota2((LANES, LANES), 1), 1.0, 0.0).astype(BF16)
    rr = _iota2((cs, cs), 0)
    cc = _iota2((cs, cs), 1)
    lower = jnp.where(rr >= cc, 1.0, 0.0).astype(BF16)
    upper = jnp.where(rr <= cc, 1.0, 0.0).astype(BF16)
    xs = _split_bf16(x, 3)
    x_rows = sum(_dot_nt(eye, p) for p in xs)
    fc_cols = sum(_dot(lower, p) for p in xs)
    fc_rows = sum(_dot(p, upper) for p in _split_bf16(x_rows, 3))
    causal = rr >= cc
    scale = dh ** -0.5

    for h in range(H_A):
        sl = slice(h * dh, (h + 1) * dh)
        q = q_ref[:, sl]
        k = k_ref[:, sl] * scale
        v = v_ref[:, sl]
        qb = q.astype(BF16)
        kb = k.astype(BF16)
        i_col = x[:, h:h + 1]
        i_row = x_rows[h:h + 1, :]
        f_col = fc_cols[:, H_A + h:H_A + h + 1]
        f_row = fc_rows[H_A + h:H_A + h + 1, :]
        m_prev = m_ref[0, :, h:h + 1]
        c_prev = c_ref[0, h]
        n_prev = n_ref[0, h:h + 1, :]

        dmat = jnp.where(causal, f_col - f_row + i_row, -jnp.inf)
        carry_log = f_col + m_prev
        m_t = jnp.maximum(jnp.max(dmat, axis=-1, keepdims=True), carry_log)
        s = _dot_nt(qb, kb) * jnp.exp(dmat - m_t)
        w_prev = jnp.exp(carry_log - m_t)
        num = _dot(s.astype(BF16), v.astype(BF16)) + w_prev * _dot_nt(qb, c_prev.astype(BF16))
        den = jnp.sum(s, axis=-1, keepdims=True) + w_prev * jnp.sum(q * n_prev, axis=-1, keepdims=True)
        hid = num / jnp.maximum(jnp.abs(den), jnp.exp(-m_t))
        y_ref[:, sl] = jax.nn.sigmoid(o_ref[:, sl]) * hid

        m_new = m_t[cs - 1:cs, :]
        f_last = f_col[cs - 1:cs, :]
        decay = jnp.exp(f_last + m_prev - m_new)
        w_end = jnp.exp(f_last - f_col + i_col - m_new)
        c_ref[0, h] = decay * c_prev + _dot_tn((w_end * v).astype(BF16), kb)
        n_ref[0, h:h + 1, :] = decay * n_prev + jnp.sum(w_end * k, axis=0, keepdims=True)
        m_ref[0, :, h:h + 1] = m_new


def mlstm(proj, gates, bias, c0, n0, m0, bsz, length):
    dh = SEC // H_A
    cs = _row_tile(length, MLSTM_CHUNK_MAX, SUBLANES)
    nc = length // cs
    rows = bsz * length

    def sec(s):
        return pl.BlockSpec((cs, SEC), lambda b, c, s=s: (b * nc + c, s))

    state_c = pl.BlockSpec((1, H_A, dh, dh), lambda b, c: (b, 0, 0, 0))
    state_n = pl.BlockSpec((1, H_A, dh), lambda b, c: (b, 0, 0))
    state_m = pl.BlockSpec((1, 1, H_A), lambda b, c: (b, 0, 0))
    y, c, n, m = pl.pallas_call(
        functools.partial(_mlstm_body, cs=cs, dh=dh),
        out_shape=(jax.ShapeDtypeStruct((rows, SEC), F32),
                   jax.ShapeDtypeStruct((bsz, H_A, dh, dh), F32),
                   jax.ShapeDtypeStruct((bsz, H_A, dh), F32),
                   jax.ShapeDtypeStruct((bsz, 1, H_A), F32)),
        grid=(bsz, nc),
        in_specs=[sec(QA), sec(KA), sec(VA), sec(OA),
                  pl.BlockSpec((cs, LANES), lambda b, c: (b * nc + c, 0)),
                  pl.BlockSpec((1, LANES), lambda b, c: (0, 0)),
                  state_c, state_n, state_m],
        out_specs=(pl.BlockSpec((cs, SEC), lambda b, c: (b * nc + c, 0)), state_c, state_n, state_m),
        compiler_params=_params(("parallel", "arbitrary"), 48),
        name="mlstm",
    )(proj, proj, proj, proj, gates, bias, c0, n0, m0.reshape(bsz, 1, H_A))
    return y, c, n, m.reshape(bsz, H_A)


def _softplus(z):
    return jnp.maximum(z, 0.0) + jnp.log(1.0 + jnp.exp(-jnp.abs(z)))


def _suffix_sum(sp, strict_upper2):
    hi, lo = _split_bf16(sp, 2)
    return _dot(jnp.concatenate([hi, lo], axis=1), strict_upper2)


def _strict_upper2(n):
    j = _iota2((2 * n, n), 0)
    j = jnp.where(j >= n, j - n, j)
    return jnp.where(j > _iota2((2 * n, n), 1), 1.0, 0.0).astype(BF16)


def _sb_prompt_body(bias_ref, q_ref, k_ref, v_ref, y_ref, *, t, dh, hpb):
    g = pl.program_id(1)
    i = pl.program_id(2)
    su2 = _strict_upper2(t)
    diag_mask = _iota2((t, t), 1) < _iota2((t, t), 0)
    heads = [slice(hh * dh, (hh + 1) * dh) for hh in range(hpb)]
    bias = [bias_ref[g * hpb + hh] for hh in range(hpb)]
    q = [(q_ref[:, sl] * (dh ** -0.5)).astype(BF16) for sl in heads]

    def tile(j, state, mask):
        rows = pl.ds(pl.multiple_of(j * t, t), t)
        z = [_dot_nt(q[hh], k_ref[rows, sl].astype(BF16)) + bias[hh] for hh, sl in enumerate(heads)]
        sp = [_softplus(x) if mask is None else jnp.where(mask, _softplus(x), 0.0) for x in z]
        later = [_suffix_sum(x, su2) + state[hh][0] for hh, x in enumerate(sp)]
        a = [jnp.exp((x - p) - w) for x, p, w in zip(z, sp, later)]
        if mask is not None:
            a = [jnp.where(mask, x, 0.0) for x in a]
        acc = [state[hh][1] + _dot(a[hh].astype(BF16), v_ref[rows, sl].astype(BF16))
               for hh, sl in enumerate(heads)]
        carry = [state[hh][0] + jnp.sum(sp[hh], axis=-1, keepdims=True) for hh in range(hpb)]
        return tuple(zip(carry, acc))

    zero = tuple((jnp.zeros((t, 1), F32), jnp.zeros((t, dh), F32)) for _ in heads)
    state = tile(i, zero, diag_mask)
    state = lax.fori_loop(0, i, lambda n, st: tile(i - 1 - n, st, None), state)
    for hh, sl in enumerate(heads):
        y_ref[:, sl] = state[hh][1]


def sb_prompt(proj, sb_bias, bsz, length):
    dh = SEC // H_B
    hpb = SB_HEADS_PER_STEP
    wide = hpb * dh
    t = _row_tile(length, SB_TILE, SUBLANES)
    nq = length // t
    cpb = SEC // wide
    return pl.pallas_call(
        functools.partial(_sb_prompt_body, t=t, dh=dh, hpb=hpb),
        out_shape=jax.ShapeDtypeStruct((bsz * length, SEC), F32),
        grid=(bsz, H_B // hpb, nq),
        in_specs=[pl.BlockSpec(memory_space=pltpu.SMEM),
                  pl.BlockSpec((t, wide), lambda b, g, i: (b * nq + i, QB * cpb + g)),
                  pl.BlockSpec((length, wide), lambda b, g, i: (b, KB * cpb + g)),
                  pl.BlockSpec((length, wide), lambda b, g, i: (b, VB * cpb + g))],
        out_specs=pl.BlockSpec((t, wide), lambda b, g, i: (b * nq + i, g)),
        compiler_params=_params(("parallel", "parallel", "arbitrary"), 40),
        name="sb_prompt",
    )(sb_bias, proj, proj, proj)


def _sb_sample_body(pt_ref, q_ref, kn_ref, vn_ref, bias_ref, *refs, lq, dh, page, ppb):
    del pt_ref
    page_refs = refs[:2 * ppb]
    y_ref, q2_ref, acc_ref, carry_ref = refs[2 * ppb:]
    j = pl.program_id(1)
    rows = H_B * lq
    bias = bias_ref[...]

    @pl.when(j == 0)
    def _():
        q = q_ref[...] * (dh ** -0.5)
        q2 = jnp.concatenate([q[:, h * dh:(h + 1) * dh] for h in range(H_B)], axis=0).astype(BF16)
        q2_ref[...] = q2
        q_wide = jnp.concatenate([q2] * H_B, axis=1)
        same_head = (_iota2((rows, SEC), 0) // lq) == (_iota2((rows, SEC), 1) // dh)
        q_bd = jnp.where(same_head, q_wide, jnp.zeros_like(q_wide))
        fill = jnp.zeros((LANES - lq, SEC), F32)
        kn = jnp.concatenate([kn_ref[...], fill], axis=0).astype(BF16)
        vn = jnp.concatenate([vn_ref[...], fill], axis=0).astype(BF16)
        z = _dot_nt(q_bd, kn) + bias
        mask = _iota2((rows, LANES), 1) < (_iota2((rows, LANES), 0) % lq)
        sp = jnp.where(mask, _softplus(z), 0.0)
        a = jnp.where(mask, jnp.exp((z - sp) - _suffix_sum(sp, _strict_upper2(LANES))), 0.0)
        full = _dot(a.astype(BF16), vn)
        acc_ref[...] = jnp.concatenate(
            [full[h * lq:(h + 1) * lq, h * dh:(h + 1) * dh] for h in range(H_B)], axis=0)
        carry_ref[...] = jnp.sum(sp, axis=-1, keepdims=True)

    q2 = q2_ref[...]
    su2 = _strict_upper2(page)
    def head_rows(ref, h):
        return ref[pl.ds(h, page, stride=H_B), :].astype(BF16)

    def own_rows(per_head):
        return jnp.concatenate([x[h * lq:(h + 1) * lq, :] for h, x in enumerate(per_head)], axis=0)

    z = [own_rows([_dot_nt(q2, head_rows(kp, h)) for h in range(H_B)]) + bias
         for kp in page_refs[:ppb]]
    sp = [_softplus(x) for x in z]
    later = [_suffix_sum(x, su2) for x in sp]
    carry = carry_ref[...]
    acc = acc_ref[...]
    for x, p, w, vp in zip(z, sp, later, page_refs[ppb:]):
        a = jnp.exp((x - p) - (w + carry)).astype(BF16)
        acc = acc + own_rows([_dot(a, head_rows(vp, h)) for h in range(H_B)])
        carry = carry + jnp.sum(p, axis=-1, keepdims=True)
    acc_ref[...] = acc
    carry_ref[...] = carry

    @pl.when(j == pl.num_programs(1) - 1)
    def _():
        acc = acc_ref[...]
        y_ref[...] = jnp.concatenate([acc[h * lq:(h + 1) * lq, :] for h in range(H_B)], axis=1)


def sb_sample(proj, sb_bias, cache_k, cache_v, page_table, layer, bsz, length):
    dh = SEC // H_B
    n_pages = page_table.shape[1]
    page = cache_k.shape[2]
    rows = H_B * length
    bias_rows = jnp.repeat(sb_bias.astype(F32), length).reshape(rows, 1)

    def new_sec(s):
        return pl.BlockSpec((length, SEC), lambda b, j, pt, s=s: (b, s))

    ppb = SB_PAGES_PER_STEP
    while n_pages % ppb:
        ppb -= 1

    def past(r):
        return pl.BlockSpec((None, None, page * H_B, dh),
                            lambda b, j, pt, r=r: (layer, pt[b, n_pages - 1 - (j * ppb + r)], 0, 0))

    page_specs = [past(r) for r in range(ppb)]
    grid_spec = pltpu.PrefetchScalarGridSpec(
        num_scalar_prefetch=1,
        grid=(bsz, n_pages // ppb),
        in_specs=[new_sec(QB), new_sec(KB), new_sec(VB),
                  pl.BlockSpec((rows, 1), lambda b, j, pt: (0, 0))] + page_specs + page_specs,
        out_specs=pl.BlockSpec((length, SEC), lambda b, j, pt: (b, 0)),
        scratch_shapes=[pltpu.VMEM((rows, dh), BF16), pltpu.VMEM((rows, dh), F32),
                        pltpu.VMEM((rows, 1), F32)],
    )
    k2d = cache_k.reshape(cache_k.shape[0], cache_k.shape[1], page * H_B, dh)
    v2d = cache_v.reshape(cache_v.shape[0], cache_v.shape[1], page * H_B, dh)
    return pl.pallas_call(
        functools.partial(_sb_sample_body, lq=length, dh=dh, page=page, ppb=ppb),
        out_shape=jax.ShapeDtypeStruct((bsz * length, SEC), F32),
        grid_spec=grid_spec,
        compiler_params=_params(("parallel", "arbitrary"), 40),
        name="sb_sample",
    )(page_table, proj, proj, proj, bias_rows, *([k2d] * ppb), *([v2d] * ppb))


def _conv_body(a_ref, g_ref, buf_ref, w_ref, b_ref, lg_ref, lb_ref, y_ref, new_ref, ext_ref, *, tl):
    hist = CONV_W - 1
    pad = 32 - hist
    step = pl.program_id(1)

    @pl.when(step == 0)
    def _():
        ext_ref[0:SUBLANES, :] = jnp.zeros((SUBLANES, SEC), F32)
        ext_ref[pad:32, :] = buf_ref[0]

    ext_ref[32:32 + tl, :] = a_ref[...] * jax.nn.sigmoid(g_ref[...])
    cols = []
    for c in range(SEC // LANES):
        lanes = slice(c * LANES, (c + 1) * LANES)
        slab = ext_ref[:, lanes]
        acc = jnp.zeros((tl, LANES), F32)
        for j in range(CONV_W):
            acc = acc + w_ref[j:j + 1, lanes] * slab[pad + j:pad + j + tl, :]
        cols.append(acc)
    y = jnp.concatenate(cols, axis=1) + b_ref[...]
    yc = y - jnp.mean(y, axis=-1, keepdims=True)
    yn = yc * lax.rsqrt(jnp.mean(yc * yc, axis=-1, keepdims=True) + EPS)
    yn = yn * lg_ref[...] + lb_ref[...]
    y_ref[...] = yn * jax.nn.sigmoid(yn)
    new_ref[0] = ext_ref[tl + pad:tl + 32, :]
    if tl >= 32:
        ext_ref[0:32, :] = ext_ref[tl:tl + 32, :]


def conformer_conv(proj, buf, w, b, ln_g, ln_b, bsz, length):
    tl = _row_tile(length, CONV_TILE, SUBLANES)
    nt = length // tl
    assert nt == 1 or tl >= 32
    hist = CONV_W - 1
    vec = pl.BlockSpec((1, SEC), lambda bb, t: (0, 0))
    return pl.pallas_call(
        functools.partial(_conv_body, tl=tl),
        out_shape=(jax.ShapeDtypeStruct((bsz * length, SEC), F32),
                   jax.ShapeDtypeStruct((bsz, hist, SEC), F32)),
        grid=(bsz, nt),
        in_specs=[pl.BlockSpec((tl, SEC), lambda bb, t: (bb * nt + t, CA)),
                  pl.BlockSpec((tl, SEC), lambda bb, t: (bb * nt + t, CG)),
                  pl.BlockSpec((1, hist, SEC), lambda bb, t: (bb, 0, 0)),
                  pl.BlockSpec((CONV_W, SEC), lambda bb, t: (0, 0)),
                  vec, vec, vec],
        out_specs=(pl.BlockSpec((tl, SEC), lambda bb, t: (bb * nt + t, 0)),
                   pl.BlockSpec((1, hist, SEC), lambda bb, t: (bb, 0, 0))),
        scratch_shapes=[pltpu.VMEM((32 + tl, SEC), F32)],
        compiler_params=_params(("parallel", "arbitrary"), 32),
        name="conformer_conv",
    )(proj, proj, buf, w, b.reshape(1, SEC), ln_g.reshape(1, SEC), ln_b.reshape(1, SEC))


def _s5_body(u_ref, h0r_ref, h0i_ref, ar_ref, ai_ref, br_ref, bi_ref, cr_ref, ci_ref, d_ref,
             gw_ref, gb_ref, y_ref, hr_ref, hi_ref, sr_ref, si_ref, *, tl):
    step = pl.program_id(1)
    nblk = br_ref.shape[0]
    cin = br_ref.shape[1]
    nst = br_ref.shape[2]

    @pl.when(step == 0)
    def _():
        hr_ref[...] = h0r_ref[...]
        hi_ref[...] = h0i_ref[...]

    u = u_ref[...]
    ub = u.astype(BF16)
    for blk in range(nblk):
        ublk = ub[:, blk * cin:(blk + 1) * cin]
        sr_ref[:, blk * nst:(blk + 1) * nst] = _dot(ublk, br_ref[blk])
        si_ref[:, blk * nst:(blk + 1) * nst] = _dot(ublk, bi_ref[blk])

    for blk in range(nblk):
        lanes = slice(blk * nst, (blk + 1) * nst)
        ar = ar_ref[:, lanes]
        ai = ai_ref[:, lanes]

        def scan_step(t, state, lanes=lanes, ar=ar, ai=ai):
            hr, hi = state
            row = pl.ds(t, 1)
            nr = ar * hr - ai * hi + sr_ref[row, lanes]
            ni = ar * hi + ai * hr + si_ref[row, lanes]
            sr_ref[row, lanes] = nr
            si_ref[row, lanes] = ni
            return nr, ni

        hr, hi = lax.fori_loop(0, tl, scan_step, (hr_ref[0, :, lanes], hi_ref[0, :, lanes]),
                               unroll=SUBLANES)
        hr_ref[0, :, lanes] = hr
        hi_ref[0, :, lanes] = hi

    ys = []
    for blk in range(nblk):
        lanes = slice(blk * nst, (blk + 1) * nst)
        ys.append(_dot(sr_ref[:, lanes].astype(BF16), cr_ref[blk])
                  - _dot(si_ref[:, lanes].astype(BF16), ci_ref[blk]))
    y = jnp.concatenate(ys, axis=1) + d_ref[...] * u
    g = jax.nn.gelu(y)
    gate = _dot(g.astype(BF16), gw_ref[...]) + gb_ref[...]
    y_ref[...] = g * jax.nn.sigmoid(gate)


def s5_discretize(a_re, a_im, log_dt, b_re, b_im, c_re, c_im):
    ngrp, nst = a_re.shape
    dt = jnp.exp(log_dt.astype(F32))[:, None]
    ar = a_re.astype(F32)
    ai = a_im.astype(F32)
    mag = jnp.exp(ar * dt)
    abar_re = mag * jnp.cos(ai * dt)
    abar_im = mag * jnp.sin(ai * dt)
    zr = abar_re - 1.0
    zi = abar_im
    den = ar * ar + ai * ai
    coef_re = (zr * ar + zi * ai) / den
    coef_im = (zi * ar - zr * ai) / den
    br = b_re.astype(F32)
    bi = b_im.astype(F32)
    bbar_re = coef_re[..., None] * br - coef_im[..., None] * bi
    bbar_im = coef_re[..., None] * bi + coef_im[..., None] * br
    gb = S5_GROUP_BLOCK
    nblk = ngrp // gb
    eye = jnp.eye(gb, dtype=F32)

    def in_blocks(bbar):
        t = bbar.reshape(nblk, gb, nst, S5_CH)
        return jnp.einsum('bgnc,gh->bgchn', t, eye).reshape(nblk, gb * S5_CH, gb * nst).astype(BF16)

    def out_blocks(c):
        t = c.astype(F32).reshape(nblk, gb, S5_CH, nst)
        return jnp.einsum('bgcn,gh->bgnhc', t, eye).reshape(nblk, gb * nst, gb * S5_CH).astype(BF16)

    return (abar_re.reshape(1, ngrp * nst), abar_im.reshape(1, ngrp * nst),
            in_blocks(bbar_re), in_blocks(bbar_im), out_blocks(c_re), out_blocks(c_im))


def s5(proj, h0_re, h0_im, disc, d, gate_w, gate_b, bsz, length):
    abar_re, abar_im, b_re, b_im, c_re, c_im = disc
    nstate = abar_re.shape[1]
    tl = _row_tile(length, S5_TILE, SUBLANES)
    nt = length // tl
    state = pl.BlockSpec((1, 1, nstate), lambda b, t: (b, 0, 0))

    def whole(a):
        return pl.BlockSpec(a.shape, lambda b, t, nd=a.ndim: (0,) * nd)

    d2 = d.reshape(1, SEC)
    gb2 = gate_b.reshape(1, SEC)
    y, hr, hi = pl.pallas_call(
        functools.partial(_s5_body, tl=tl),
        out_shape=(jax.ShapeDtypeStruct((bsz * length, SEC), F32),
                   jax.ShapeDtypeStruct((bsz, 1, nstate), F32),
                   jax.ShapeDtypeStruct((bsz, 1, nstate), F32)),
        grid=(bsz, nt),
        in_specs=[pl.BlockSpec((tl, SEC), lambda b, t: (b * nt + t, UD)), state, state,
                  whole(abar_re), whole(abar_im), whole(b_re), whole(b_im), whole(c_re), whole(c_im),
                  whole(d2), whole(gate_w), whole(gb2)],
        out_specs=(pl.BlockSpec((tl, SEC), lambda b, t: (b * nt + t, 0)), state, state),
        scratch_shapes=[pltpu.VMEM((tl, nstate), F32), pltpu.VMEM((tl, nstate), F32)],
        compiler_params=_params(("parallel", "arbitrary"), 48),
        name="s5",
    )(proj, h0_re.reshape(bsz, 1, nstate), h0_im.reshape(bsz, 1, nstate),
      abar_re, abar_im, b_re, b_im, c_re, c_im, d2, gate_w, gb2)
    return y, hr, hi


def _cross_body(q_ref, k_ref, v_ref, o_ref, *, dh):
    scale = dh ** -0.5
    for h in range(H_X):
        sl = slice(h * dh, (h + 1) * dh)
        s = _dot_nt(q_ref[:, sl].astype(BF16), k_ref[:, sl].astype(BF16)) * scale
        e = jnp.exp(s - jnp.max(s, axis=-1, keepdims=True))
        p = e / jnp.sum(e, axis=-1, keepdims=True)
        o_ref[:, sl] = _dot(p.astype(BF16), v_ref[:, sl].astype(BF16)).astype(o_ref.dtype)


def cross_attend(q, mem_k, mem_v, k_col, v_col, bsz, length, n_mem):
    wx = q.shape[1]
    tq = _row_tile(length, 512, SUBLANES)
    nq = length // tq
    return pl.pallas_call(
        functools.partial(_cross_body, dh=wx // H_X),
        out_shape=jax.ShapeDtypeStruct((bsz * length, wx), F32),
        grid=(bsz, nq),
        in_specs=[pl.BlockSpec((tq, wx), lambda b, i: (b * nq + i, 0)),
                  pl.BlockSpec((n_mem, wx), lambda b, i: (b, k_col)),
                  pl.BlockSpec((n_mem, wx), lambda b, i: (b, v_col))],
        out_specs=pl.BlockSpec((tq, wx), lambda b, i: (b * nq + i, 0)),
        compiler_params=_params(("parallel", "arbitrary"), 32),
        name="cross_attend",
    )(q, mem_k, mem_v)


def _decoder_layer(x, bsz, length, mem_k, mem_v, k_col, v_col, n_mem, sb_fn, mlstm_state, conv_buf,
                   s5_state, p):
    h, gates = rmsnorm_gates(x, p['norm_mix_pre'], p['w_gates'])
    proj = matmul(h, p['w_in'], F32)
    y_a, c_new, n_new, m_new = mlstm(proj, gates, p['gate_bias'], *mlstm_state, bsz, length)
    y_b = sb_fn(proj)
    y_c, conv_new = conformer_conv(proj, conv_buf, p['conv_w'], p['conv_b'], p['conv_ln_g'],
                                   p['conv_ln_b'], bsz, length)
    y_d, s5_re, s5_im = s5(proj, s5_state[0], s5_state[1], p['s5_disc'], p['s5_d'], p['s5_gate_w'],
                           p['s5_gate_b'], bsz, length)
    mixed = group_norm_concat(y_a, y_b, y_c, y_d, p['g_group'])
    x = matmul_norm_res(mixed, p['w_out'], p['norm_mix_post'], x)
    q = matmul(rmsnorm_bf16(x, p['norm_x_pre']), p['w_xq'], F32)
    o = cross_attend(q, mem_k, mem_v, k_col, v_col, bsz, length, n_mem)
    x = matmul_norm_res(o, p['w_xo'], p['norm_x_post'], x)
    x = ffn(rmsnorm_bf16(x, p['norm_ffn_pre']), p['w_up'], p['w_down'], p['norm_ffn_post'], x)
    dh_b = SEC // H_B
    k_new = head_split(proj, KB, H_B).reshape(bsz, length, H_B, dh_b)
    v_new = head_split(proj, VB, H_B).reshape(bsz, length, H_B, dh_b)
    return x, k_new, v_new, (c_new, n_new, m_new), conv_new, (s5_re, s5_im)


def kernel(x_prompt, x_sample, cache_sb_k, cache_sb_v, page_table, state_mlstm_c, state_mlstm_n, state_mlstm_m, state_conv, state_s5_re, state_s5_im, cache_mem_k, cache_mem_v, mem_prompt, norm_mix_pre, w_in, b_mlstm_gates, sb_bias, conv_w, conv_b, conv_ln_g, conv_ln_b, s5_a_re, s5_a_im, s5_log_dt, s5_b_re, s5_b_im, s5_c_re, s5_c_im, s5_d, s5_gate_w, s5_gate_b, g_group, w_out, norm_mix_post, norm_mem, norm_x_pre, w_xq, w_xk, w_xv, w_xo, norm_x_post, norm_ffn_pre, w_up, w_down, norm_ffn_post):
    n_b, seq, d_model = x_prompt.shape
    n_db, dec_seq, _ = x_sample.shape
    depth = w_in.shape[0]
    n_mem = mem_prompt.shape[1]
    w_x = w_xq.shape[2]
    dh_a = SEC // H_A
    ngrp = s5_a_re.shape[1]
    n_gate = 2 * H_A
    gate_lo = 4 * SEC

    xp = x_prompt.reshape(n_b * seq, d_model)
    xs = x_sample.reshape(n_db * dec_seq, d_model)
    mem2d = mem_prompt.reshape(n_b * n_mem, d_model)
    outs = [[] for _ in range(18)]
    for l in range(depth):
        wl = w_in[l]
        p = {
            'norm_mix_pre': norm_mix_pre[l],
            'w_in': jnp.concatenate([wl[:, :gate_lo], wl[:, gate_lo + n_gate:]], axis=1).astype(BF16),
            'w_gates': jnp.pad(wl[:, gate_lo:gate_lo + n_gate], ((0, 0), (0, LANES - n_gate))).astype(BF16),
            'gate_bias': jnp.pad(b_mlstm_gates[l].astype(F32), (0, LANES - n_gate)).reshape(1, LANES),
            'conv_w': conv_w[l], 'conv_b': conv_b[l], 'conv_ln_g': conv_ln_g[l], 'conv_ln_b': conv_ln_b[l],
            's5_disc': s5_discretize(s5_a_re[l], s5_a_im[l], s5_log_dt[l], s5_b_re[l], s5_b_im[l],
                                     s5_c_re[l], s5_c_im[l]),
            's5_d': s5_d[l], 's5_gate_w': s5_gate_w[l].astype(BF16), 's5_gate_b': s5_gate_b[l],
            'g_group': g_group[l], 'w_out': cast_bf16(w_out, l), 'norm_mix_post': norm_mix_post[l],
            'norm_x_pre': norm_x_pre[l], 'w_xq': w_xq[l].astype(BF16), 'w_xo': w_xo[l].astype(BF16),
            'norm_x_post': norm_x_post[l], 'norm_ffn_pre': norm_ffn_pre[l],
            'w_up': cast_bf16(w_up, l), 'w_down': cast_bf16(w_down, l),
            'norm_ffn_post': norm_ffn_post[l],
        }
        w_kv = jnp.concatenate([w_xk[l], w_xv[l]], axis=1).astype(BF16)
        mem_kv = matmul(rmsnorm_bf16(mem2d, norm_mem[l]), w_kv, F32, tm_cap=512)
        zero_mlstm = (jnp.zeros((n_b, H_A, dh_a, dh_a), F32), jnp.zeros((n_b, H_A, dh_a), F32),
                      jnp.zeros((n_b, H_A), F32))
        zero_s5 = (jnp.zeros((n_b, ngrp, S5_N), F32), jnp.zeros((n_b, ngrp, S5_N), F32))
        sb_p = functools.partial(sb_prompt, sb_bias=sb_bias[l], bsz=n_b, length=seq)
        xp, kp, vp, mst_p, cnv_p, s5_p = _decoder_layer(
            xp, n_b, seq, mem_kv, mem_kv, 0, 1, n_mem, sb_p, zero_mlstm,
            jnp.zeros((n_b, CONV_W - 1, SEC), F32), zero_s5, p)
        sb_s = functools.partial(sb_sample, sb_bias=sb_bias[l], cache_k=cache_sb_k, cache_v=cache_sb_v,
                                 page_table=page_table, layer=l, bsz=n_db, length=dec_seq)
        xs, ks, vs, mst_s, cnv_s, s5_s = _decoder_layer(
            xs, n_db, dec_seq, cache_mem_k[l].reshape(n_db * n_mem, w_x),
            cache_mem_v[l].reshape(n_db * n_mem, w_x), 0, 0, n_mem, sb_s,
            (state_mlstm_c[l], state_mlstm_n[l], state_mlstm_m[l]), state_conv[l],
            (state_s5_re[l], state_s5_im[l]), p)
        mk = mem_kv[:, :w_x].reshape(n_b, n_mem, H_X, w_x // H_X)
        mv = mem_kv[:, w_x:].reshape(n_b, n_mem, H_X, w_x // H_X)
        s5_shape = lambda a: a.reshape(a.shape[0], ngrp, S5_N)
        layer_out = (kp, vp, ks, vs, mst_p[0], mst_p[1], mst_p[2], mst_s[0], mst_s[1], mst_s[2],
                     cnv_p, cnv_s, s5_shape(s5_p[0]), s5_shape(s5_p[1]), s5_shape(s5_s[0]),
                     s5_shape(s5_s[1]), mk, mv)
        for acc, val in zip(outs, layer_out):
            acc.append(val)
    stacked = tuple(jnp.stack(vals, axis=0) for vals in outs)
    return (xp.reshape(n_b, seq, d_model), xs.reshape(n_db, dec_seq, d_model)) + stacked
```

```python
import functools

import jax
import jax.numpy as jnp
from jax import lax
from jax.experimental import pallas as pl
from jax.experimental.pallas import tpu as pltpu

F32 = jnp.float32
BF16 = jnp.bfloat16
EPS = 1e-6

H_A = 4
H_B = 8
CONV_W = 31
S5_CH = 16
S5_N = 64
H_X = 4
MLSTM_CHUNK_MAX = 256
SB_TILE = 256
SB_HEADS_PER_STEP = 4
SB_PAGES_PER_STEP = 8
CONV_TILE = 64
S5_TILE = 256
FFN_TILE = 1024
FFN_OUT_CHUNK = 1024
FFN_NORM_ROWS = 32
S5_GROUP_BLOCK = 16

LANES = 128
SUBLANES = 8
MIB = 1024 * 1024

SEC = 1024
QA, KA, VA, OA, QB, KB, VB, CA, CG, UD = range(10)


def _params(semantics, vmem_mib):
    return pltpu.CompilerParams(dimension_semantics=semantics, vmem_limit_bytes=vmem_mib * MIB)


def _row_tile(m, cap, mult=16):
    t = min(m, cap)
    while m % t or t % mult:
        t -= 1
    return t


def _log_sigmoid_pair(z):
    sp = jnp.log1p(jnp.exp(-jnp.abs(z)))
    return jnp.minimum(z, 0.0) - sp, -(jnp.maximum(z, 0.0) + sp)


def _split_bf16(x, terms):
    out = []
    r = x
    for _ in range(terms):
        p = r.astype(BF16)
        out.append(p)
        r = r - p.astype(F32)
    return out


def _dot(a, b):
    return jnp.dot(a, b, preferred_element_type=F32)


def _dot_nt(a, b):
    return lax.dot_general(a, b, (((1,), (1,)), ((), ())), preferred_element_type=F32)


def _dot_tn(a, b):
    return lax.dot_general(a, b, (((0,), (0,)), ((), ())), preferred_element_type=F32)


def _iota2(shape, dim):
    return lax.broadcasted_iota(jnp.int32, shape, dim)


def _rmsnorm_body(x_ref, g_ref, o_ref):
    x = x_ref[...]
    r = lax.rsqrt(jnp.mean(x * x, axis=-1, keepdims=True) + EPS)
    o_ref[...] = ((x * r) * g_ref[...]).astype(o_ref.dtype)


def rmsnorm_bf16(x, g):
    m, d = x.shape
    tr = _row_tile(m, 256)
    return pl.pallas_call(
        _rmsnorm_body,
        out_shape=jax.ShapeDtypeStruct((m, d), BF16),
        grid=(m // tr,),
        in_specs=[pl.BlockSpec((tr, d), lambda i: (i, 0)), pl.BlockSpec((1, d), lambda i: (0, 0))],
        out_specs=pl.BlockSpec((tr, d), lambda i: (i, 0)),
        compiler_params=_params(("parallel",), 32),
        name="rmsnorm",
    )(x, g.reshape(1, d))


def _rmsnorm_gates_body(x_ref, g_ref, wg_ref, o_ref, gates_ref):
    x = x_ref[...]
    r = lax.rsqrt(jnp.mean(x * x, axis=-1, keepdims=True) + EPS)
    h = ((x * r) * g_ref[...]).astype(BF16)
    o_ref[...] = h
    gates_ref[...] = _dot(h, wg_ref[...])


def rmsnorm_gates(x, g, wg):
    m, d = x.shape
    tr = _row_tile(m, 256)
    return pl.pallas_call(
        _rmsnorm_gates_body,
        out_shape=(jax.ShapeDtypeStruct((m, d), BF16), jax.ShapeDtypeStruct((m, LANES), F32)),
        grid=(m // tr,),
        in_specs=[pl.BlockSpec((tr, d), lambda i: (i, 0)), pl.BlockSpec((1, d), lambda i: (0, 0)),
                  pl.BlockSpec((d, LANES), lambda i: (0, 0))],
        out_specs=(pl.BlockSpec((tr, d), lambda i: (i, 0)), pl.BlockSpec((tr, LANES), lambda i: (i, 0))),
        compiler_params=_params(("parallel",), 32),
        name="rmsnorm_gates",
    )(x, g.reshape(1, d), wg)


def _matmul_body(a_ref, w_ref, o_ref):
    o_ref[...] = _dot(a_ref[...], w_ref[...]).astype(o_ref.dtype)


def matmul(a, w, out_dtype, tm_cap=1024, tn_cap=1024):
    m, k = a.shape
    n = w.shape[1]
    tm = _row_tile(m, tm_cap)
    tn = _row_tile(n, tn_cap, LANES)
    return pl.pallas_call(
        _matmul_body,
        out_shape=jax.ShapeDtypeStruct((m, n), out_dtype),
        grid=(m // tm, n // tn),
        in_specs=[pl.BlockSpec((tm, k), lambda i, j: (i, 0)), pl.BlockSpec((k, tn), lambda i, j: (0, j))],
        out_specs=pl.BlockSpec((tm, tn), lambda i, j: (i, j)),
        compiler_params=_params(("parallel", "arbitrary"), 56),
        name="matmul",
    )(a, w)


def _norm_residual(y, g_ref, res_ref, o_ref):
    r = lax.rsqrt(jnp.mean(y * y, axis=-1, keepdims=True) + EPS)
    o_ref[...] = res_ref[...] + (y * r) * g_ref[...]


def _matmul_norm_res_body(a_ref, w_ref, g_ref, res_ref, o_ref):
    _norm_residual(_dot(a_ref[...].astype(BF16), w_ref[...]), g_ref, res_ref, o_ref)


def matmul_norm_res(a, w, g, res):
    m, kdim = a.shape
    n = w.shape[1]
    w_bytes = kdim * n * w.dtype.itemsize
    tm = _row_tile(m, 256 if w_bytes > 16 * MIB else 512)
    return pl.pallas_call(
        _matmul_norm_res_body,
        out_shape=jax.ShapeDtypeStruct((m, n), F32),
        grid=(m // tm,),
        in_specs=[pl.BlockSpec((tm, kdim), lambda i: (i, 0)),
                  pl.BlockSpec((kdim, n), lambda i: (0, 0), pipeline_mode=pl.Buffered(1)),
                  pl.BlockSpec((1, n), lambda i: (0, 0)),
                  pl.BlockSpec((tm, n), lambda i: (i, 0))],
        out_specs=pl.BlockSpec((tm, n), lambda i: (i, 0)),
        compiler_params=_params(("parallel",), 60),
        name="matmul_norm_res",
    )(a, w, g.reshape(1, n), res)


def _ffn_body(h_ref, wu_ref, wd_ref, g_ref, res_ref, o_ref, *, nf):
    f = pl.program_id(1)

    @pl.when(f == 0)
    def _():
        o_ref[...] = jnp.zeros_like(o_ref)

    u = _dot(h_ref[...], wu_ref[...])
    u = jnp.square(jnp.maximum(u, 0.0)).astype(BF16)
    d = o_ref.shape[1]
    for c in range(0, d, FFN_OUT_CHUNK):
        cols = slice(c, c + FFN_OUT_CHUNK)
        o_ref[:, cols] += _dot(u, wd_ref[:, cols])

    @pl.when(f == nf - 1)
    def _():
        def rows_step(r, carry):
            rows = pl.ds(pl.multiple_of(r * FFN_NORM_ROWS, FFN_NORM_ROWS), FFN_NORM_ROWS)
            _norm_residual(o_ref[rows, :], g_ref, res_ref.at[rows, :], o_ref.at[rows, :])
            return carry

        lax.fori_loop(0, o_ref.shape[0] // FFN_NORM_ROWS, rows_step, 0)


def ffn(h, w_up_tiles, w_down, g, res, tm_cap=512):
    m, d = h.shape
    nf, _, tf = w_up_tiles.shape
    tm = _row_tile(m, tm_cap)
    once = pl.Buffered(1)
    return pl.pallas_call(
        functools.partial(_ffn_body, nf=nf),
        out_shape=jax.ShapeDtypeStruct((m, d), F32),
        grid=(m // tm, nf),
        in_specs=[pl.BlockSpec((tm, d), lambda i, f: (i, 0), pipeline_mode=once),
                  pl.BlockSpec((None, d, tf), lambda i, f: (f, 0, 0)),
                  pl.BlockSpec((tf, d), lambda i, f: (f, 0)),
                  pl.BlockSpec((1, d), lambda i, f: (0, 0)),
                  pl.BlockSpec((tm, d), lambda i, f: (i, 0), pipeline_mode=once)],
        out_specs=pl.BlockSpec((tm, d), lambda i, f: (i, 0), pipeline_mode=once),
        compiler_params=_params(("parallel", "arbitrary"), 60),
        name="ffn",
    )(h, w_up_tiles, w_down, g.reshape(1, d), res)


def _cast_body(x_ref, o_ref):
    o_ref[...] = x_ref[...].astype(o_ref.dtype)


def cast_bf16_col_tiles(x, layer, tc):
    _, r, c = x.shape
    tr = _row_tile(r, 1024)
    return pl.pallas_call(
        _cast_body,
        out_shape=jax.ShapeDtypeStruct((c // tc, r, tc), BF16),
        grid=(r // tr, c // tc),
        in_specs=[pl.BlockSpec((None, tr, tc), lambda i, j: (layer, i, j))],
        out_specs=pl.BlockSpec((None, tr, tc), lambda i, j: (j, i, 0)),
        compiler_params=_params(("parallel", "parallel"), 40),
        name="cast_bf16_col_tiles",
    )(x)


def cast_bf16(x, layer):
    _, r, c = x.shape
    tr = _row_tile(r, 512)
    tc = _row_tile(c, 4096, LANES)
    return pl.pallas_call(
        _cast_body,
        out_shape=jax.ShapeDtypeStruct((r, c), BF16),
        grid=(r // tr, c // tc),
        in_specs=[pl.BlockSpec((None, tr, tc), lambda i, j: (layer, i, j))],
        out_specs=pl.BlockSpec((tr, tc), lambda i, j: (i, j)),
        compiler_params=_params(("parallel", "parallel"), 40),
        name="cast_bf16",
    )(x)


def _cast_w_in_body(x_ref, nxt_ref, o_ref, *, aligned_tiles, skip):
    j = pl.program_id(1)

    @pl.when(j < aligned_tiles)
    def _():
        o_ref[...] = x_ref[...].astype(o_ref.dtype)

    @pl.when(j >= aligned_tiles)
    def _():
        wide = jnp.concatenate([x_ref[...], nxt_ref[...]], axis=1)
        o_ref[...] = wide[:, skip:skip + x_ref.shape[1]].astype(o_ref.dtype)


def cast_w_in(w_in, layer, gate_lo, n_gate):
    _, r, c = w_in.shape
    tc = SEC
    tr = _row_tile(r, 512)
    n_out = c - n_gate
    assert gate_lo % tc == 0 and n_out % tc == 0 and n_gate < LANES
    return pl.pallas_call(
        functools.partial(_cast_w_in_body, aligned_tiles=gate_lo // tc, skip=n_gate),
        out_shape=jax.ShapeDtypeStruct((r, n_out), BF16),
        grid=(r // tr, n_out // tc),
        in_specs=[pl.BlockSpec((None, tr, tc), lambda i, j: (layer, i, j)),
                  pl.BlockSpec((None, tr, LANES), lambda i, j: (layer, i, (j + 1) * (tc // LANES)))],
        out_specs=pl.BlockSpec((tr, tc), lambda i, j: (i, j)),
        compiler_params=_params(("parallel", "parallel"), 32),
        name="cast_w_in",
    )(w_in, w_in)


def _head_split_body(x_ref, *refs, nh, dh):
    o_ref = refs[-1]
    rows = x_ref.shape[0]
    for h in range(nh):
        o_ref[pl.ds(h, rows, stride=nh), :] = x_ref[:, h * dh:(h + 1) * dh]


def head_split(proj, section, nh, stacked, layer, depth):
    rows = proj.shape[0]
    dh = SEC // nh
    tr = _row_tile(rows, 512, SUBLANES)
    nt = rows // tr
    args = (proj,) if stacked is None else (proj, stacked)
    in_specs = [pl.BlockSpec((tr, SEC), lambda i: (i, section))]
    if stacked is not None:
        in_specs.append(pl.BlockSpec(memory_space=pl.ANY))
    return pl.pallas_call(
        functools.partial(_head_split_body, nh=nh, dh=dh),
        out_shape=jax.ShapeDtypeStruct((depth * rows * nh, dh), F32),
        grid=(nt,),
        in_specs=in_specs,
        out_specs=pl.BlockSpec((tr * nh, dh), lambda i: (layer * nt + i, 0)),
        input_output_aliases={} if stacked is None else {1: 0},
        compiler_params=_params(("parallel",), 32),
        name="head_split",
    )(*args)


def _group_norm_body(a_ref, b_ref, c_ref, d_ref, g_ref, o_ref):
    w = a_ref.shape[1]
    for i, ref in enumerate((a_ref, b_ref, c_ref, d_ref)):
        y = ref[...]
        r = lax.rsqrt(jnp.mean(y * y, axis=-1, keepdims=True) + EPS)
        o_ref[:, i * w:(i + 1) * w] = ((y * r) * g_ref[:, i * w:(i + 1) * w]).astype(o_ref.dtype)


def group_norm_concat(ya, yb, yc, yd, g):
    m, w = ya.shape
    tr = _row_tile(m, 256)
    spec = pl.BlockSpec((tr, w), lambda i: (i, 0))
    return pl.pallas_call(
        _group_norm_body,
        out_shape=jax.ShapeDtypeStruct((m, 4 * w), BF16),
        grid=(m // tr,),
        in_specs=[spec, spec, spec, spec, pl.BlockSpec((1, 4 * w), lambda i: (0, 0))],
        out_specs=pl.BlockSpec((tr, 4 * w), lambda i: (i, 0)),
        compiler_params=_params(("parallel",), 32),
        name="group_norm_concat",
    )(ya, yb, yc, yd, g.reshape(1, 4 * w))


def _mlstm_body(q_ref, k_ref, v_ref, o_ref, gates_ref, bias_ref, c0_ref, n0_ref, m0_ref,
                y_ref, c_ref, n_ref, m_ref, *, cs, dh):
    step = pl.program_id(1)

    @pl.when(step == 0)
    def _():
        c_ref[...] = c0_ref[...]
        n_ref[...] = n0_ref[...]
        m_ref[...] = m0_ref[...]

    g = gates_ref[...] + bias_ref[...]
    col = _iota2(g.shape, 1)
    x = jnp.where(col >= H_A, _log_sigmoid_pair(g)[0], g)
    eye = jnp.where(_iota2((LANES, LANES), 0) == _iota2((LANES, LANES), 1), 1.0, 0.0).astype(BF16)
    rr = _iota2((cs, cs), 0)
    cc = _iota2((cs, cs), 1)
    lower = jnp.where(rr >= cc, 1.0, 0.0).astype(BF16)
    upper = jnp.where(rr <= cc, 1.0, 0.0).astype(BF16)
    xs = _split_bf16(x, 3)
    x_rows = sum(_dot_nt(eye, p) for p in xs)
    fc_cols = sum(_dot(lower, p) for p in xs)
    fc_rows = sum(_dot(p, upper) for p in _split_bf16(x_rows, 3))
    causal = rr >= cc
    scale = dh ** -0.5

    for h in range(H_A):
        sl = slice(h * dh, (h + 1) * dh)
        q = q_ref[:, sl]
        k = k_ref[:, sl] * scale
        v = v_ref[:, sl]
        qb = q.astype(BF16)
        kb = k.astype(BF16)
        i_col = x[:, h:h + 1]
        i_row = x_rows[h:h + 1, :]
        f_col = fc_cols[:, H_A + h:H_A + h + 1]
        f_row = fc_rows[H_A + h:H_A + h + 1, :]
        m_prev = m_ref[0, :, h:h + 1]
        c_prev = c_ref[0, h]
        n_prev = n_ref[0, h:h + 1, :]

        dmat = jnp.where(causal, f_col - f_row + i_row, -jnp.inf)
        carry_log = f_col + m_prev
        m_t = jnp.maximum(jnp.max(dmat, axis=-1, keepdims=True), carry_log)
        s = _dot_nt(qb, kb) * jnp.exp(dmat - m_t)
        w_prev = jnp.exp(carry_log - m_t)
        num = _dot(s.astype(BF16), v.astype(BF16)) + w_prev * _dot_nt(qb, c_prev.astype(BF16))
        den = jnp.sum(s, axis=-1, keepdims=True) + w_prev * jnp.sum(q * n_prev, axis=-1, keepdims=True)
        hid = num / jnp.maximum(jnp.abs(den), jnp.exp(-m_t))
        y_ref[:, sl] = jax.nn.sigmoid(o_ref[:, sl]) * hid

        m_new = m_t[cs - 1:cs, :]
        f_last = f_col[cs - 1:cs, :]
        decay = jnp.exp(f_last + m_prev - m_new)
        w_end = jnp.exp(f_last - f_col + i_col - m_new)
        c_ref[0, h] = decay * c_prev + _dot_tn((w_end * v).astype(BF16), kb)
        n_ref[0, h:h + 1, :] = decay * n_prev + jnp.sum(w_end * k, axis=0, keepdims=True)
        m_ref[0, :, h:h + 1] = m_new


def mlstm(proj, gates, bias, c0, n0, m0, bsz, length):
    dh = SEC // H_A
    cs = _row_tile(length, MLSTM_CHUNK_MAX, SUBLANES)
    nc = length // cs
    rows = bsz * length

    def sec(s):
        return pl.BlockSpec((cs, SEC), lambda b, c, s=s: (b * nc + c, s))

    state_c = pl.BlockSpec((1, H_A, dh, dh), lambda b, c: (b, 0, 0, 0))
    state_n = pl.BlockSpec((1, H_A, dh), lambda b, c: (b, 0, 0))
    state_m = pl.BlockSpec((1, 1, H_A), lambda b, c: (b, 0, 0))
    y, c, n, m = pl.pallas_call(
        functools.partial(_mlstm_body, cs=cs, dh=dh),
        out_shape=(jax.ShapeDtypeStruct((rows, SEC), F32),
                   jax.ShapeDtypeStruct((bsz, H_A, dh, dh), F32),
                   jax.ShapeDtypeStruct((bsz, H_A, dh), F32),
                   jax.ShapeDtypeStruct((bsz, 1, H_A), F32)),
        grid=(bsz, nc),
        in_specs=[sec(QA), sec(KA), sec(VA), sec(OA),
                  pl.BlockSpec((cs, LANES), lambda b, c: (b * nc + c, 0)),
                  pl.BlockSpec((1, LANES), lambda b, c: (0, 0)),
                  state_c, state_n, state_m],
        out_specs=(pl.BlockSpec((cs, SEC), lambda b, c: (b * nc + c, 0)), state_c, state_n, state_m),
        compiler_params=_params(("parallel", "arbitrary"), 48),
        name="mlstm",
    )(proj, proj, proj, proj, gates, bias, c0, n0, m0.reshape(bsz, 1, H_A))
    return y, c, n, m.reshape(bsz, H_A)


def _softplus(z):
    return jnp.maximum(z, 0.0) + jnp.log(1.0 + jnp.exp(-jnp.abs(z)))


def _suffix_sum(sp, strict_upper2):
    hi, lo = _split_bf16(sp, 2)
    return _dot(jnp.concatenate([hi, lo], axis=1), strict_upper2)


def _strict_upper2(n):
    j = _iota2((2 * n, n), 0)
    j = jnp.where(j >= n, j - n, j)
    return jnp.where(j > _iota2((2 * n, n), 1), 1.0, 0.0).astype(BF16)


def _sb_prompt_body(bias_ref, q_ref, k_ref, v_ref, y_ref, *, t, dh, hpb):
    g = pl.program_id(1)
    i = pl.program_id(2)
    su2 = _strict_upper2(t)
    diag_mask = _iota2((t, t), 1) < _iota2((t, t), 0)
    heads = [slice(hh * dh, (hh + 1) * dh) for hh in range(hpb)]
    bias = [bias_ref[g * hpb + hh] for hh in range(hpb)]
    q = [(q_ref[:, sl] * (dh ** -0.5)).astype(BF16) for sl in heads]

    def tile(j, state, mask):
        rows = pl.ds(pl.multiple_of(j * t, t), t)
        z = [_dot_nt(q[hh], k_ref[rows, sl].astype(BF16)) + bias[hh] for hh, sl in enumerate(heads)]
        sp = [_softplus(x) if mask is None else jnp.where(mask, _softplus(x), 0.0) for x in z]
        later = [_suffix_sum(x, su2) + state[hh][0] for hh, x in enumerate(sp)]
        a = [jnp.exp((x - p) - w) for x, p, w in zip(z, sp, later)]
        if mask is not None:
            a = [jnp.where(mask, x, 0.0) for x in a]
        acc = [state[hh][1] + _dot(a[hh].astype(BF16), v_ref[rows, sl].astype(BF16))
               for hh, sl in enumerate(heads)]
        carry = [state[hh][0] + jnp.sum(sp[hh], axis=-1, keepdims=True) for hh in range(hpb)]
        return tuple(zip(carry, acc))

    zero = tuple((jnp.zeros((t, 1), F32), jnp.zeros((t, dh), F32)) for _ in heads)
    state = tile(i, zero, diag_mask)
    state = lax.fori_loop(0, i, lambda n, st: tile(i - 1 - n, st, None), state)
    for hh, sl in enumerate(heads):
        y_ref[:, sl] = state[hh][1]


def sb_prompt(proj, sb_bias, bsz, length):
    dh = SEC // H_B
    hpb = SB_HEADS_PER_STEP
    wide = hpb * dh
    t = _row_tile(length, SB_TILE, SUBLANES)
    nq = length // t
    cpb = SEC // wide
    return pl.pallas_call(
        functools.partial(_sb_prompt_body, t=t, dh=dh, hpb=hpb),
        out_shape=jax.ShapeDtypeStruct((bsz * length, SEC), F32),
        grid=(bsz, H_B // hpb, nq),
        in_specs=[pl.BlockSpec(memory_space=pltpu.SMEM),
                  pl.BlockSpec((t, wide), lambda b, g, i: (b * nq + i, QB * cpb + g)),
                  pl.BlockSpec((length, wide), lambda b, g, i: (b, KB * cpb + g)),
                  pl.BlockSpec((length, wide), lambda b, g, i: (b, VB * cpb + g))],
        out_specs=pl.BlockSpec((t, wide), lambda b, g, i: (b * nq + i, g)),
        compiler_params=_params(("parallel", "parallel", "arbitrary"), 40),
        name="sb_prompt",
    )(sb_bias, proj, proj, proj)


def _sb_sample_body(pt_ref, q_ref, kn_ref, vn_ref, bias_ref, *refs, lq, dh, page, ppb):
    del pt_ref
    page_refs = refs[:2 * ppb]
    y_ref, q2_ref, acc_ref, carry_ref = refs[2 * ppb:]
    j = pl.program_id(1)
    rows = H_B * lq
    bias = bias_ref[...]

    @pl.when(j == 0)
    def _():
        q = q_ref[...] * (dh ** -0.5)
        q2 = jnp.concatenate([q[:, h * dh:(h + 1) * dh] for h in range(H_B)], axis=0).astype(BF16)
        q2_ref[...] = q2
        q_wide = jnp.concatenate([q2] * H_B, axis=1)
        same_head = (_iota2((rows, SEC), 0) // lq) == (_iota2((rows, SEC), 1) // dh)
        q_bd = jnp.where(same_head, q_wide, jnp.zeros_like(q_wide))
        fill = jnp.zeros((LANES - lq, SEC), F32)
        kn = jnp.concatenate([kn_ref[...], fill], axis=0).astype(BF16)
        vn = jnp.concatenate([vn_ref[...], fill], axis=0).astype(BF16)
        z = _dot_nt(q_bd, kn) + bias
        mask = _iota2((rows, LANES), 1) < (_iota2((rows, LANES), 0) % lq)
        sp = jnp.where(mask, _softplus(z), 0.0)
        a = jnp.where(mask, jnp.exp((z - sp) - _suffix_sum(sp, _strict_upper2(LANES))), 0.0)
        full = _dot(a.astype(BF16), vn)
        acc_ref[...] = jnp.concatenate(
            [full[h * lq:(h + 1) * lq, h * dh:(h + 1) * dh] for h in range(H_B)], axis=0)
        carry_ref[...] = jnp.sum(sp, axis=-1, keepdims=True)

    q2 = q2_ref[...]
    su2 = _strict_upper2(page)
    def head_rows(ref, h):
        return ref[pl.ds(h, page, stride=H_B), :].astype(BF16)

    def own_rows(per_head):
        return jnp.concatenate([x[h * lq:(h + 1) * lq, :] for h, x in enumerate(per_head)], axis=0)

    z = [own_rows([_dot_nt(q2, head_rows(kp, h)) for h in range(H_B)]) + bias
         for kp in page_refs[:ppb]]
    sp = [_softplus(x) for x in z]
    later = [_suffix_sum(x, su2) for x in sp]
    carry = carry_ref[...]
    acc = acc_ref[...]
    for x, p, w, vp in zip(z, sp, later, page_refs[ppb:]):
        a = jnp.exp((x - p) - (w + carry)).astype(BF16)
        acc = acc + own_rows([_dot(a, head_rows(vp, h)) for h in range(H_B)])
        carry = carry + jnp.sum(p, axis=-1, keepdims=True)
    acc_ref[...] = acc
    carry_ref[...] = carry

    @pl.when(j == pl.num_programs(1) - 1)
    def _():
        acc = acc_ref[...]
        y_ref[...] = jnp.concatenate([acc[h * lq:(h + 1) * lq, :] for h in range(H_B)], axis=1)


def sb_sample(proj, sb_bias, cache_k, cache_v, page_table, layer, bsz, length):
    dh = SEC // H_B
    n_pages = page_table.shape[1]
    page = cache_k.shape[2]
    rows = H_B * length
    bias_rows = jnp.repeat(sb_bias.astype(F32), length).reshape(rows, 1)

    def new_sec(s):
        return pl.BlockSpec((length, SEC), lambda b, j, pt, s=s: (b, s))

    ppb = SB_PAGES_PER_STEP
    while n_pages % ppb:
        ppb -= 1

    def past(r):
        return pl.BlockSpec((None, None, page * H_B, dh),
                            lambda b, j, pt, r=r: (layer, pt[b, n_pages - 1 - (j * ppb + r)], 0, 0))

    page_specs = [past(r) for r in range(ppb)]
    grid_spec = pltpu.PrefetchScalarGridSpec(
        num_scalar_prefetch=1,
        grid=(bsz, n_pages // ppb),
        in_specs=[new_sec(QB), new_sec(KB), new_sec(VB),
                  pl.BlockSpec((rows, 1), lambda b, j, pt: (0, 0))] + page_specs + page_specs,
        out_specs=pl.BlockSpec((length, SEC), lambda b, j, pt: (b, 0)),
        scratch_shapes=[pltpu.VMEM((rows, dh), BF16), pltpu.VMEM((rows, dh), F32),
                        pltpu.VMEM((rows, 1), F32)],
    )
    k2d = cache_k.reshape(cache_k.shape[0], cache_k.shape[1], page * H_B, dh)
    v2d = cache_v.reshape(cache_v.shape[0], cache_v.shape[1], page * H_B, dh)
    return pl.pallas_call(
        functools.partial(_sb_sample_body, lq=length, dh=dh, page=page, ppb=ppb),
        out_shape=jax.ShapeDtypeStruct((bsz * length, SEC), F32),
        grid_spec=grid_spec,
        compiler_params=_params(("parallel", "arbitrary"), 40),
        name="sb_sample",
    )(page_table, proj, proj, proj, bias_rows, *([k2d] * ppb), *([v2d] * ppb))


def _conv_body(a_ref, g_ref, buf_ref, w_ref, b_ref, lg_ref, lb_ref, y_ref, new_ref, ext_ref, *, tl):
    hist = CONV_W - 1
    pad = 32 - hist
    step = pl.program_id(1)

    @pl.when(step == 0)
    def _():
        ext_ref[0:SUBLANES, :] = jnp.zeros((SUBLANES, SEC), F32)
        ext_ref[pad:32, :] = buf_ref[0]

    ext_ref[32:32 + tl, :] = a_ref[...] * jax.nn.sigmoid(g_ref[...])
    cols = []
    for c in range(SEC // LANES):
        lanes = slice(c * LANES, (c + 1) * LANES)
        slab = ext_ref[:, lanes]
        nrow = slab.shape[0]
        acc = jnp.zeros((tl, LANES), F32)
        for phase in range(SUBLANES):
            taps = [j for j in range(CONV_W) if (pad + j) % SUBLANES == phase]
            shifted = slab if phase == 0 else pltpu.roll(slab, nrow - phase, axis=0)
            for j in taps:
                start = pad + j - phase
                acc = acc + w_ref[j:j + 1, lanes] * shifted[start:start + tl, :]
        cols.append(acc)
    y = jnp.concatenate(cols, axis=1) + b_ref[...]
    yc = y - jnp.mean(y, axis=-1, keepdims=True)
    yn = yc * lax.rsqrt(jnp.mean(yc * yc, axis=-1, keepdims=True) + EPS)
    yn = yn * lg_ref[...] + lb_ref[...]
    y_ref[...] = yn * jax.nn.sigmoid(yn)
    new_ref[0] = ext_ref[tl + pad:tl + 32, :]
    if tl >= 32:
        ext_ref[0:32, :] = ext_ref[tl:tl + 32, :]


def conformer_conv(proj, buf, w, b, ln_g, ln_b, bsz, length):
    tl = _row_tile(length, CONV_TILE, SUBLANES)
    nt = length // tl
    assert nt == 1 or tl >= 32
    hist = CONV_W - 1
    vec = pl.BlockSpec((1, SEC), lambda bb, t: (0, 0))
    return pl.pallas_call(
        functools.partial(_conv_body, tl=tl),
        out_shape=(jax.ShapeDtypeStruct((bsz * length, SEC), F32),
                   jax.ShapeDtypeStruct((bsz, hist, SEC), F32)),
        grid=(bsz, nt),
        in_specs=[pl.BlockSpec((tl, SEC), lambda bb, t: (bb * nt + t, CA)),
                  pl.BlockSpec((tl, SEC), lambda bb, t: (bb * nt + t, CG)),
                  pl.BlockSpec((1, hist, SEC), lambda bb, t: (bb, 0, 0)),
                  pl.BlockSpec((CONV_W, SEC), lambda bb, t: (0, 0)),
                  vec, vec, vec],
        out_specs=(pl.BlockSpec((tl, SEC), lambda bb, t: (bb * nt + t, 0)),
                   pl.BlockSpec((1, hist, SEC), lambda bb, t: (bb, 0, 0))),
        scratch_shapes=[pltpu.VMEM((32 + tl, SEC), F32)],
        compiler_params=_params(("parallel", "arbitrary"), 32),
        name="conformer_conv",
    )(proj, proj, buf, w, b.reshape(1, SEC), ln_g.reshape(1, SEC), ln_b.reshape(1, SEC))


def _s5_drive(u_ref, br_ref, bi_ref, sr_ref, si_ref, perm):
    nblk, cin, nst = br_ref.shape
    ub = u_ref[...].astype(BF16)
    if perm is not None:
        ub = _dot(perm, ub).astype(BF16)
    for blk in range(nblk):
        ublk = ub[:, blk * cin:(blk + 1) * cin]
        sr_ref[:, blk * nst:(blk + 1) * nst] = _dot(ublk, br_ref[blk])
        si_ref[:, blk * nst:(blk + 1) * nst] = _dot(ublk, bi_ref[blk])


def _s5_readout(sr_ref, si_ref, cr_ref, ci_ref):
    nblk, nst, _ = cr_ref.shape
    ys = []
    for blk in range(nblk):
        lanes = slice(blk * nst, (blk + 1) * nst)
        ys.append(_dot(sr_ref[:, lanes].astype(BF16), cr_ref[blk])
                  - _dot(si_ref[:, lanes].astype(BF16), ci_ref[blk]))
    return jnp.concatenate(ys, axis=1)


def _s5_gate(y, gw_ref, gb_ref, y_ref):
    g = jax.nn.gelu(y)
    gate = _dot(g.astype(BF16), gw_ref[...]) + gb_ref[...]
    y_ref[...] = g * jax.nn.sigmoid(gate)


def _s5_segmented_body(u_ref, h0r_ref, h0i_ref, ar_ref, ai_ref, br_ref, bi_ref, cr_ref, ci_ref, d_ref,
                       gw_ref, gb_ref, y_ref, hr_ref, hi_ref, sr_ref, si_ref, pr_ref, pi_ref,
                       yp_ref, yt_ref, *, tl):
    step = pl.program_id(1)
    nseg = SUBLANES
    slen = tl // nseg
    nblk = br_ref.shape[0]
    nst = br_ref.shape[2]

    @pl.when(step == 0)
    def _():
        hr_ref[...] = h0r_ref[...]
        hi_ref[...] = h0i_ref[...]
        for blk in range(nblk):
            lanes = slice(blk * nst, (blk + 1) * nst)
            ar = ar_ref[:, lanes]
            ai = ai_ref[:, lanes]
            qr, qi = ar, ai
            for t in range(slen):
                pr_ref[t:t + 1, lanes] = qr
                pi_ref[t:t + 1, lanes] = qi
                qr, qi = qr * ar - qi * ai, qr * ai + qi * ar

    dst = _iota2((tl, tl), 0)
    src = (dst % nseg) * slen + dst // nseg
    perm = jnp.where(_iota2((tl, tl), 1) == src, 1.0, 0.0).astype(BF16)
    _s5_drive(u_ref, br_ref, bi_ref, sr_ref, si_ref, perm)

    first = _iota2((nseg, nst), 0) == 0
    for blk in range(nblk):
        lanes = slice(blk * nst, (blk + 1) * nst)
        ar = ar_ref[:, lanes]
        ai = ai_ref[:, lanes]

        def local_step(t, state, lanes=lanes, ar=ar, ai=ai):
            hr, hi = state
            rows = pl.ds(pl.multiple_of(t * nseg, nseg), nseg)
            nr = ar * hr - ai * hi + sr_ref[rows, lanes]
            ni = ar * hi + ai * hr + si_ref[rows, lanes]
            sr_ref[rows, lanes] = nr
            si_ref[rows, lanes] = ni
            return nr, ni

        start = (jnp.where(first, hr_ref[0, :, lanes], 0.0), jnp.where(first, hi_ref[0, :, lanes], 0.0))
        er, ei = lax.fori_loop(0, slen, local_step, start, unroll=4)

        wr = pr_ref[slen - 1:slen, lanes]
        wi = pi_ref[slen - 1:slen, lanes]
        tr, ti = er[0:1, :], ei[0:1, :]
        starts_r = [jnp.zeros_like(tr)]
        starts_i = [jnp.zeros_like(ti)]
        for s in range(1, nseg):
            starts_r.append(tr)
            starts_i.append(ti)
            tr, ti = er[s:s + 1, :] + wr * tr - wi * ti, ei[s:s + 1, :] + wr * ti + wi * tr
        hr_ref[0, :, lanes] = tr
        hi_ref[0, :, lanes] = ti
        gr = jnp.concatenate(starts_r, axis=0)
        gi = jnp.concatenate(starts_i, axis=0)

        def fix_step(t, carry, lanes=lanes, gr=gr, gi=gi):
            rows = pl.ds(pl.multiple_of(t * nseg, nseg), nseg)
            qr = pr_ref[pl.ds(t, 1), lanes]
            qi = pi_ref[pl.ds(t, 1), lanes]
            sr_ref[rows, lanes] += qr * gr - qi * gi
            si_ref[rows, lanes] += qr * gi + qi * gr
            return carry

        lax.fori_loop(0, slen, fix_step, 0, unroll=4)

    yp = _s5_readout(sr_ref, si_ref, cr_ref, ci_ref)
    for c in range(SEC // LANES):
        yp_ref[c] = yp[:, c * LANES:(c + 1) * LANES]
    for c in range(SEC // LANES):
        for s in range(nseg):
            yt_ref[s * slen:(s + 1) * slen, c * LANES:(c + 1) * LANES] = (
                yp_ref[c, pl.ds(s, slen, stride=nseg), :])
    _s5_gate(yt_ref[...] + d_ref[...] * u_ref[...], gw_ref, gb_ref, y_ref)


def _s5_body(u_ref, h0r_ref, h0i_ref, ar_ref, ai_ref, br_ref, bi_ref, cr_ref, ci_ref, d_ref,
             gw_ref, gb_ref, y_ref, hr_ref, hi_ref, sr_ref, si_ref, *, tl):
    step = pl.program_id(1)
    nblk = br_ref.shape[0]
    nst = br_ref.shape[2]

    @pl.when(step == 0)
    def _():
        hr_ref[...] = h0r_ref[...]
        hi_ref[...] = h0i_ref[...]

    _s5_drive(u_ref, br_ref, bi_ref, sr_ref, si_ref, None)

    for blk in range(nblk):
        lanes = slice(blk * nst, (blk + 1) * nst)
        ar = ar_ref[:, lanes]
        ai = ai_ref[:, lanes]

        def scan_step(t, state, lanes=lanes, ar=ar, ai=ai):
            hr, hi = state
            row = pl.ds(t, 1)
            nr = ar * hr - ai * hi + sr_ref[row, lanes]
            ni = ar * hi + ai * hr + si_ref[row, lanes]
            sr_ref[row, lanes] = nr
            si_ref[row, lanes] = ni
            return nr, ni

        hr, hi = lax.fori_loop(0, tl, scan_step, (hr_ref[0, :, lanes], hi_ref[0, :, lanes]),
                               unroll=SUBLANES)
        hr_ref[0, :, lanes] = hr
        hi_ref[0, :, lanes] = hi

    y = _s5_readout(sr_ref, si_ref, cr_ref, ci_ref) + d_ref[...] * u_ref[...]
    _s5_gate(y, gw_ref, gb_ref, y_ref)


def s5_discretize(a_re, a_im, log_dt, b_re, b_im, c_re, c_im):
    ngrp, nst = a_re.shape
    dt = jnp.exp(log_dt.astype(F32))[:, None]
    ar = a_re.astype(F32)
    ai = a_im.astype(F32)
    mag = jnp.exp(ar * dt)
    abar_re = mag * jnp.cos(ai * dt)
    abar_im = mag * jnp.sin(ai * dt)
    zr = abar_re - 1.0
    zi = abar_im
    den = ar * ar + ai * ai
    coef_re = (zr * ar + zi * ai) / den
    coef_im = (zi * ar - zr * ai) / den
    br = b_re.astype(F32)
    bi = b_im.astype(F32)
    bbar_re = coef_re[..., None] * br - coef_im[..., None] * bi
    bbar_im = coef_re[..., None] * bi + coef_im[..., None] * br
    gb = S5_GROUP_BLOCK
    nblk = ngrp // gb
    eye = jnp.eye(gb, dtype=F32)

    def in_blocks(bbar):
        t = bbar.reshape(nblk, gb, nst, S5_CH)
        return jnp.einsum('bgnc,gh->bgchn', t, eye).reshape(nblk, gb * S5_CH, gb * nst).astype(BF16)

    def out_blocks(c):
        t = c.astype(F32).reshape(nblk, gb, S5_CH, nst)
        return jnp.einsum('bgcn,gh->bgnhc', t, eye).reshape(nblk, gb * nst, gb * S5_CH).astype(BF16)

    return (abar_re.reshape(1, ngrp * nst), abar_im.reshape(1, ngrp * nst),
            in_blocks(bbar_re), in_blocks(bbar_im), out_blocks(c_re), out_blocks(c_im))


def s5(proj, h0_re, h0_im, disc, d, gate_w, gate_b, bsz, length):
    abar_re, abar_im, b_re, b_im, c_re, c_im = disc
    nstate = abar_re.shape[1]
    tl = _row_tile(length, S5_TILE, SUBLANES)
    nt = length // tl
    state = pl.BlockSpec((1, 1, nstate), lambda b, t: (b, 0, 0))

    def whole(a):
        return pl.BlockSpec(a.shape, lambda b, t, nd=a.ndim: (0,) * nd)

    d2 = d.reshape(1, SEC)
    gb2 = gate_b.reshape(1, SEC)
    scratch = [pltpu.VMEM((tl, nstate), F32), pltpu.VMEM((tl, nstate), F32)]
    body = _s5_body
    if tl % (SUBLANES * SUBLANES) == 0:
        body = _s5_segmented_body
        slen = tl // SUBLANES
        scratch += [pltpu.VMEM((slen, nstate), F32), pltpu.VMEM((slen, nstate), F32),
                    pltpu.VMEM((SEC // LANES, tl, LANES), F32), pltpu.VMEM((tl, SEC), F32)]
    y, hr, hi = pl.pallas_call(
        functools.partial(body, tl=tl),
        out_shape=(jax.ShapeDtypeStruct((bsz * length, SEC), F32),
                   jax.ShapeDtypeStruct((bsz, 1, nstate), F32),
                   jax.ShapeDtypeStruct((bsz, 1, nstate), F32)),
        grid=(bsz, nt),
        in_specs=[pl.BlockSpec((tl, SEC), lambda b, t: (b * nt + t, UD)), state, state,
                  whole(abar_re), whole(abar_im), whole(b_re), whole(b_im), whole(c_re), whole(c_im),
                  whole(d2), whole(gate_w), whole(gb2)],
        out_specs=(pl.BlockSpec((tl, SEC), lambda b, t: (b * nt + t, 0)), state, state),
        scratch_shapes=scratch,
        compiler_params=_params(("parallel", "arbitrary"), 48),
        name="s5",
    )(proj, h0_re.reshape(bsz, 1, nstate), h0_im.reshape(bsz, 1, nstate),
      abar_re, abar_im, b_re, b_im, c_re, c_im, d2, gate_w, gb2)
    return y, hr, hi


def _cross_body(q_ref, k_ref, v_ref, o_ref, *, dh):
    scale = dh ** -0.5
    for h in range(H_X):
        sl = slice(h * dh, (h + 1) * dh)
        s = _dot_nt(q_ref[:, sl].astype(BF16), k_ref[:, sl].astype(BF16)) * scale
        e = jnp.exp(s - jnp.max(s, axis=-1, keepdims=True))
        p = e / jnp.sum(e, axis=-1, keepdims=True)
        o_ref[:, sl] = _dot(p.astype(BF16), v_ref[:, sl].astype(BF16)).astype(o_ref.dtype)


def cross_attend(q, mem_k, mem_v, k_col, v_col, bsz, length, n_mem):
    wx = q.shape[1]
    tq = _row_tile(length, 512, SUBLANES)
    nq = length // tq
    return pl.pallas_call(
        functools.partial(_cross_body, dh=wx // H_X),
        out_shape=jax.ShapeDtypeStruct((bsz * length, wx), F32),
        grid=(bsz, nq),
        in_specs=[pl.BlockSpec((tq, wx), lambda b, i: (b * nq + i, 0)),
                  pl.BlockSpec((n_mem, wx), lambda b, i: (b, k_col)),
                  pl.BlockSpec((n_mem, wx), lambda b, i: (b, v_col))],
        out_specs=pl.BlockSpec((tq, wx), lambda b, i: (b * nq + i, 0)),
        compiler_params=_params(("parallel", "arbitrary"), 32),
        name="cross_attend",
    )(q, mem_k, mem_v)


def _decoder_layer(x, bsz, length, mem_k, mem_v, k_col, v_col, n_mem, sb_fn, mlstm_state, conv_buf,
                   s5_state, p, kv_stacked, layer, depth):
    h, gates = rmsnorm_gates(x, p['norm_mix_pre'], p['w_gates'])
    proj = matmul(h, p['w_in'], F32)
    y_a, c_new, n_new, m_new = mlstm(proj, gates, p['gate_bias'], *mlstm_state, bsz, length)
    y_b = sb_fn(proj)
    y_c, conv_new = conformer_conv(proj, conv_buf, p['conv_w'], p['conv_b'], p['conv_ln_g'],
                                   p['conv_ln_b'], bsz, length)
    y_d, s5_re, s5_im = s5(proj, s5_state[0], s5_state[1], p['s5_disc'], p['s5_d'], p['s5_gate_w'],
                           p['s5_gate_b'], bsz, length)
    mixed = group_norm_concat(y_a, y_b, y_c, y_d, p['g_group'])
    x = matmul_norm_res(mixed, p['w_out'], p['norm_mix_post'], x)
    q = matmul(rmsnorm_bf16(x, p['norm_x_pre']), p['w_xq'], F32)
    o = cross_attend(q, mem_k, mem_v, k_col, v_col, bsz, length, n_mem)
    x = matmul_norm_res(o, p['w_xo'], p['norm_x_post'], x)
    x = ffn(rmsnorm_bf16(x, p['norm_ffn_pre']), p['w_up'], p['w_down'], p['norm_ffn_post'], x)
    kv_stacked = (head_split(proj, KB, H_B, kv_stacked[0], layer, depth),
                  head_split(proj, VB, H_B, kv_stacked[1], layer, depth))
    return x, kv_stacked, (c_new, n_new, m_new), conv_new, (s5_re, s5_im)


def kernel(x_prompt, x_sample, cache_sb_k, cache_sb_v, page_table, state_mlstm_c, state_mlstm_n, state_mlstm_m, state_conv, state_s5_re, state_s5_im, cache_mem_k, cache_mem_v, mem_prompt, norm_mix_pre, w_in, b_mlstm_gates, sb_bias, conv_w, conv_b, conv_ln_g, conv_ln_b, s5_a_re, s5_a_im, s5_log_dt, s5_b_re, s5_b_im, s5_c_re, s5_c_im, s5_d, s5_gate_w, s5_gate_b, g_group, w_out, norm_mix_post, norm_mem, norm_x_pre, w_xq, w_xk, w_xv, w_xo, norm_x_post, norm_ffn_pre, w_up, w_down, norm_ffn_post):
    n_b, seq, d_model = x_prompt.shape
    n_db, dec_seq, _ = x_sample.shape
    depth = w_in.shape[0]
    n_mem = mem_prompt.shape[1]
    w_x = w_xq.shape[2]
    dh_a = SEC // H_A
    ngrp = s5_a_re.shape[1]
    n_gate = 2 * H_A
    gate_lo = 4 * SEC

    xp = x_prompt.reshape(n_b * seq, d_model)
    xs = x_sample.reshape(n_db * dec_seq, d_model)
    mem2d = mem_prompt.reshape(n_b * n_mem, d_model)
    outs = [[] for _ in range(14)]
    kv_p = (None, None)
    kv_s = (None, None)
    for l in range(depth):
        p = {
            'norm_mix_pre': norm_mix_pre[l],
            'w_in': cast_w_in(w_in, l, gate_lo, n_gate),
            'w_gates': jnp.pad(w_in[l, :, gate_lo:gate_lo + n_gate],
                               ((0, 0), (0, LANES - n_gate))).astype(BF16),
            'gate_bias': jnp.pad(b_mlstm_gates[l].astype(F32), (0, LANES - n_gate)).reshape(1, LANES),
            'conv_w': conv_w[l], 'conv_b': conv_b[l], 'conv_ln_g': conv_ln_g[l], 'conv_ln_b': conv_ln_b[l],
            's5_disc': s5_discretize(s5_a_re[l], s5_a_im[l], s5_log_dt[l], s5_b_re[l], s5_b_im[l],
                                     s5_c_re[l], s5_c_im[l]),
            's5_d': s5_d[l], 's5_gate_w': s5_gate_w[l].astype(BF16), 's5_gate_b': s5_gate_b[l],
            'g_group': g_group[l], 'w_out': cast_bf16(w_out, l), 'norm_mix_post': norm_mix_post[l],
            'norm_x_pre': norm_x_pre[l], 'w_xq': w_xq[l].astype(BF16), 'w_xo': w_xo[l].astype(BF16),
            'norm_x_post': norm_x_post[l], 'norm_ffn_pre': norm_ffn_pre[l],
            'w_up': cast_bf16_col_tiles(w_up, l, FFN_TILE), 'w_down': cast_bf16(w_down, l),
            'norm_ffn_post': norm_ffn_post[l],
        }
        w_kv = jnp.concatenate([w_xk[l], w_xv[l]], axis=1).astype(BF16)
        mem_kv = matmul(rmsnorm_bf16(mem2d, norm_mem[l]), w_kv, F32, tm_cap=512)
        zero_mlstm = (jnp.zeros((n_b, H_A, dh_a, dh_a), F32), jnp.zeros((n_b, H_A, dh_a), F32),
                      jnp.zeros((n_b, H_A), F32))
        zero_s5 = (jnp.zeros((n_b, ngrp, S5_N), F32), jnp.zeros((n_b, ngrp, S5_N), F32))
        sb_p = functools.partial(sb_prompt, sb_bias=sb_bias[l], bsz=n_b, length=seq)
        xp, kv_p, mst_p, cnv_p, s5_p = _decoder_layer(
            xp, n_b, seq, mem_kv, mem_kv, 0, 1, n_mem, sb_p, zero_mlstm,
            jnp.zeros((n_b, CONV_W - 1, SEC), F32), zero_s5, p, kv_p, l, depth)
        sb_s = functools.partial(sb_sample, sb_bias=sb_bias[l], cache_k=cache_sb_k, cache_v=cache_sb_v,
                                 page_table=page_table, layer=l, bsz=n_db, length=dec_seq)
        xs, kv_s, mst_s, cnv_s, s5_s = _decoder_layer(
            xs, n_db, dec_seq, cache_mem_k[l].reshape(n_db * n_mem, w_x),
            cache_mem_v[l].reshape(n_db * n_mem, w_x), 0, 0, n_mem, sb_s,
            (state_mlstm_c[l], state_mlstm_n[l], state_mlstm_m[l]), state_conv[l],
            (state_s5_re[l], state_s5_im[l]), p, kv_s, l, depth)
        mk = mem_kv[:, :w_x].reshape(n_b, n_mem, H_X, w_x // H_X)
        mv = mem_kv[:, w_x:].reshape(n_b, n_mem, H_X, w_x // H_X)
        s5_shape = lambda a: a.reshape(a.shape[0], ngrp, S5_N)
        layer_out = (mst_p[0], mst_p[1], mst_p[2], mst_s[0], mst_s[1], mst_s[2],
                     cnv_p, cnv_s, s5_shape(s5_p[0]), s5_shape(s5_p[1]), s5_shape(s5_s[0]),
                     s5_shape(s5_s[1]), mk, mv)
        for acc, val in zip(outs, layer_out):
            acc.append(val)
    stacked = tuple(jnp.stack(vals, axis=0) for vals in outs)
    dh_b = SEC // H_B
    kv = tuple(a.reshape(depth, n, t, H_B, dh_b)
               for a, n, t in ((kv_p[0], n_b, seq), (kv_p[1], n_b, seq),
                               (kv_s[0], n_db, dec_seq), (kv_s[1], n_db, dec_seq)))
    return (xp.reshape(n_b, seq, d_model), xs.reshape(n_db, dec_seq, d_model)) + kv + stacked
```

```python
import functools

import jax
import jax.numpy as jnp
from jax import lax
from jax.experimental import pallas as pl
from jax.experimental.pallas import tpu as pltpu

F32 = jnp.float32
BF16 = jnp.bfloat16
EPS = 1e-6

H_A = 4
H_B = 8
CONV_W = 31
S5_CH = 16
S5_N = 64
H_X = 4
MLSTM_CHUNK_MAX = 256
SB_TILE = 256
SB_HEADS_PER_STEP = 4
SB_PAGES_PER_STEP = 8
CONV_TILE = 64
S5_TILE = 256
FFN_TILE = 1024
FFN_OUT_CHUNK = 1024
FFN_NORM_ROWS = 32
S5_GROUP_BLOCK = 16

LANES = 128
SUBLANES = 8
MIB = 1024 * 1024

SEC = 1024
QA, KA, VA, OA, QB, KB, VB, CA, CG, UD = range(10)


def _params(semantics, vmem_mib):
    return pltpu.CompilerParams(dimension_semantics=semantics, vmem_limit_bytes=vmem_mib * MIB)


def _row_tile(m, cap, mult=16):
    t = min(m, cap)
    while m % t or t % mult:
        t -= 1
    return t


def _log_sigmoid_pair(z):
    sp = jnp.log1p(jnp.exp(-jnp.abs(z)))
    return jnp.minimum(z, 0.0) - sp, -(jnp.maximum(z, 0.0) + sp)


def _split_bf16(x, terms):
    out = []
    r = x
    for _ in range(terms):
        p = r.astype(BF16)
        out.append(p)
        r = r - p.astype(F32)
    return out


def _dot(a, b):
    return jnp.dot(a, b, preferred_element_type=F32)


def _dot_nt(a, b):
    return lax.dot_general(a, b, (((1,), (1,)), ((), ())), preferred_element_type=F32)


def _dot_tn(a, b):
    return lax.dot_general(a, b, (((0,), (0,)), ((), ())), preferred_element_type=F32)


def _iota2(shape, dim):
    return lax.broadcasted_iota(jnp.int32, shape, dim)


def _rmsnorm_body(x_ref, g_ref, o_ref):
    x = x_ref[...]
    r = lax.rsqrt(jnp.mean(x * x, axis=-1, keepdims=True) + EPS)
    o_ref[...] = ((x * r) * g_ref[...]).astype(o_ref.dtype)


def rmsnorm_bf16(x, g):
    m, d = x.shape
    tr = _row_tile(m, 256)
    return pl.pallas_call(
        _rmsnorm_body,
        out_shape=jax.ShapeDtypeStruct((m, d), BF16),
        grid=(m // tr,),
        in_specs=[pl.BlockSpec((tr, d), lambda i: (i, 0)), pl.BlockSpec((1, d), lambda i: (0, 0))],
        out_specs=pl.BlockSpec((tr, d), lambda i: (i, 0)),
        compiler_params=_params(("parallel",), 32),
        name="rmsnorm",
    )(x, g.reshape(1, d))


def _rmsnorm_gates_body(x_ref, g_ref, wg_ref, o_ref, gates_ref):
    x = x_ref[...]
    r = lax.rsqrt(jnp.mean(x * x, axis=-1, keepdims=True) + EPS)
    h = ((x * r) * g_ref[...]).astype(BF16)
    o_ref[...] = h
    gates_ref[...] = _dot(h, wg_ref[...])


def rmsnorm_gates(x, g, wg):
    m, d = x.shape
    tr = _row_tile(m, 256)
    return pl.pallas_call(
        _rmsnorm_gates_body,
        out_shape=(jax.ShapeDtypeStruct((m, d), BF16), jax.ShapeDtypeStruct((m, LANES), F32)),
        grid=(m // tr,),
        in_specs=[pl.BlockSpec((tr, d), lambda i: (i, 0)), pl.BlockSpec((1, d), lambda i: (0, 0)),
                  pl.BlockSpec((d, LANES), lambda i: (0, 0))],
        out_specs=(pl.BlockSpec((tr, d), lambda i: (i, 0)), pl.BlockSpec((tr, LANES), lambda i: (i, 0))),
        compiler_params=_params(("parallel",), 32),
        name="rmsnorm_gates",
    )(x, g.reshape(1, d), wg)


def _matmul_body(a_ref, w_ref, o_ref):
    o_ref[...] = _dot(a_ref[...], w_ref[...]).astype(o_ref.dtype)


def matmul(a, w, out_dtype, tm_cap=1024, tn_cap=1024):
    m, k = a.shape
    n = w.shape[1]
    tm = _row_tile(m, tm_cap)
    tn = _row_tile(n, tn_cap, LANES)
    return pl.pallas_call(
        _matmul_body,
        out_shape=jax.ShapeDtypeStruct((m, n), out_dtype),
        grid=(m // tm, n // tn),
        in_specs=[pl.BlockSpec((tm, k), lambda i, j: (i, 0)), pl.BlockSpec((k, tn), lambda i, j: (0, j))],
        out_specs=pl.BlockSpec((tm, tn), lambda i, j: (i, j)),
        compiler_params=_params(("parallel", "arbitrary"), 56),
        name="matmul",
    )(a, w)


def _norm_residual(y, g_ref, res_ref, o_ref):
    r = lax.rsqrt(jnp.mean(y * y, axis=-1, keepdims=True) + EPS)
    o_ref[...] = res_ref[...] + (y * r) * g_ref[...]


def _matmul_norm_res_body(a_ref, w_ref, g_ref, res_ref, g_next_ref, o_ref, h_ref):
    _norm_residual(_dot(a_ref[...].astype(BF16), w_ref[...]), g_ref, res_ref, o_ref)
    x = o_ref[...]
    r = lax.rsqrt(jnp.mean(x * x, axis=-1, keepdims=True) + EPS)
    h_ref[...] = ((x * r) * g_next_ref[...]).astype(h_ref.dtype)


def matmul_norm_res(a, w, g, res, g_next):
    m, kdim = a.shape
    n = w.shape[1]
    w_bytes = kdim * n * w.dtype.itemsize
    tm = _row_tile(m, 128 if w_bytes > 16 * MIB else 256)
    rows = pl.BlockSpec((tm, n), lambda i: (i, 0))
    vec = pl.BlockSpec((1, n), lambda i: (0, 0))
    return pl.pallas_call(
        _matmul_norm_res_body,
        out_shape=(jax.ShapeDtypeStruct((m, n), F32), jax.ShapeDtypeStruct((m, n), BF16)),
        grid=(m // tm,),
        in_specs=[pl.BlockSpec((tm, kdim), lambda i: (i, 0)),
                  pl.BlockSpec((kdim, n), lambda i: (0, 0), pipeline_mode=pl.Buffered(1)),
                  vec, rows, vec],
        out_specs=(rows, rows),
        compiler_params=_params(("parallel",), 60 if w_bytes > 16 * MIB else 40),
        name="matmul_norm_res",
    )(a, w, g.reshape(1, n), res, g_next.reshape(1, n))


def _ffn_body(h_ref, wu_ref, wd_ref, g_ref, res_ref, o_ref, *, nf):
    f = pl.program_id(1)

    @pl.when(f == 0)
    def _():
        o_ref[...] = jnp.zeros_like(o_ref)

    u = _dot(h_ref[...], wu_ref[...])
    u = jnp.square(jnp.maximum(u, 0.0)).astype(BF16)
    d = o_ref.shape[1]
    for c in range(0, d, FFN_OUT_CHUNK):
        cols = slice(c, c + FFN_OUT_CHUNK)
        o_ref[:, cols] += _dot(u, wd_ref[:, cols])

    @pl.when(f == nf - 1)
    def _():
        def rows_step(r, carry):
            rows = pl.ds(pl.multiple_of(r * FFN_NORM_ROWS, FFN_NORM_ROWS), FFN_NORM_ROWS)
            _norm_residual(o_ref[rows, :], g_ref, res_ref.at[rows, :], o_ref.at[rows, :])
            return carry

        lax.fori_loop(0, o_ref.shape[0] // FFN_NORM_ROWS, rows_step, 0)


def ffn(h, w_up_tiles, w_down, g, res, tm_cap=512):
    m, d = h.shape
    nf, _, tf = w_up_tiles.shape
    tm = _row_tile(m, tm_cap)
    once = pl.Buffered(1)
    return pl.pallas_call(
        functools.partial(_ffn_body, nf=nf),
        out_shape=jax.ShapeDtypeStruct((m, d), F32),
        grid=(m // tm, nf),
        in_specs=[pl.BlockSpec((tm, d), lambda i, f: (i, 0), pipeline_mode=once),
                  pl.BlockSpec((None, d, tf), lambda i, f: (f, 0, 0)),
                  pl.BlockSpec((tf, d), lambda i, f: (f, 0)),
                  pl.BlockSpec((1, d), lambda i, f: (0, 0)),
                  pl.BlockSpec((tm, d), lambda i, f: (i, 0), pipeline_mode=once)],
        out_specs=pl.BlockSpec((tm, d), lambda i, f: (i, 0), pipeline_mode=once),
        compiler_params=_params(("parallel", "arbitrary"), 60),
        name="ffn",
    )(h, w_up_tiles, w_down, g.reshape(1, d), res)


def _cast_body(x_ref, o_ref):
    o_ref[...] = x_ref[...].astype(o_ref.dtype)


def cast_bf16_col_tiles(x, layer, tc):
    _, r, c = x.shape
    tr = _row_tile(r, 1024)
    return pl.pallas_call(
        _cast_body,
        out_shape=jax.ShapeDtypeStruct((c // tc, r, tc), BF16),
        grid=(r // tr, c // tc),
        in_specs=[pl.BlockSpec((None, tr, tc), lambda i, j: (layer, i, j))],
        out_specs=pl.BlockSpec((None, tr, tc), lambda i, j: (j, i, 0)),
        compiler_params=_params(("parallel", "parallel"), 40),
        name="cast_bf16_col_tiles",
    )(x)


def cast_bf16(x, layer):
    _, r, c = x.shape
    tr = _row_tile(r, 512)
    tc = _row_tile(c, 4096, LANES)
    return pl.pallas_call(
        _cast_body,
        out_shape=jax.ShapeDtypeStruct((r, c), BF16),
        grid=(r // tr, c // tc),
        in_specs=[pl.BlockSpec((None, tr, tc), lambda i, j: (layer, i, j))],
        out_specs=pl.BlockSpec((tr, tc), lambda i, j: (i, j)),
        compiler_params=_params(("parallel", "parallel"), 40),
        name="cast_bf16",
    )(x)


def _cast_w_in_body(x_ref, nxt_ref, o_ref, gates_ref, *, aligned_tiles, skip):
    j = pl.program_id(1)

    @pl.when(j < aligned_tiles)
    def _():
        o_ref[...] = x_ref[...].astype(o_ref.dtype)

    @pl.when(j >= aligned_tiles)
    def _():
        wide = jnp.concatenate([x_ref[...], nxt_ref[...]], axis=1)
        o_ref[...] = wide[:, skip:skip + x_ref.shape[1]].astype(o_ref.dtype)

    @pl.when(j == aligned_tiles)
    def _():
        head = x_ref[:, 0:LANES]
        keep = _iota2(head.shape, 1) < skip
        gates_ref[...] = jnp.where(keep, head, 0.0).astype(gates_ref.dtype)


def cast_w_in(w_in, layer, gate_lo, n_gate):
    _, r, c = w_in.shape
    tc = SEC
    tr = _row_tile(r, 512)
    n_out = c - n_gate
    assert gate_lo % tc == 0 and n_out % tc == 0 and n_gate < LANES
    return pl.pallas_call(
        functools.partial(_cast_w_in_body, aligned_tiles=gate_lo // tc, skip=n_gate),
        out_shape=(jax.ShapeDtypeStruct((r, n_out), BF16), jax.ShapeDtypeStruct((r, LANES), BF16)),
        grid=(r // tr, n_out // tc),
        in_specs=[pl.BlockSpec((None, tr, tc), lambda i, j: (layer, i, j)),
                  pl.BlockSpec((None, tr, LANES), lambda i, j: (layer, i, (j + 1) * (tc // LANES)))],
        out_specs=(pl.BlockSpec((tr, tc), lambda i, j: (i, j)),
                   pl.BlockSpec((tr, LANES), lambda i, j: (i, 0))),
        compiler_params=_params(("parallel", "arbitrary"), 32),
        name="cast_w_in",
    )(w_in, w_in)


def _head_split_body(x_ref, *refs, nh, dh):
    o_ref = refs[-1]
    rows = x_ref.shape[0]
    for h in range(nh):
        o_ref[pl.ds(h, rows, stride=nh), :] = x_ref[:, h * dh:(h + 1) * dh]


def head_split(proj, section, nh, stacked, layer, depth):
    rows = proj.shape[0]
    dh = SEC // nh
    tr = _row_tile(rows, 512, SUBLANES)
    nt = rows // tr
    args = (proj,) if stacked is None else (proj, stacked)
    in_specs = [pl.BlockSpec((tr, SEC), lambda i: (i, section))]
    if stacked is not None:
        in_specs.append(pl.BlockSpec(memory_space=pl.ANY))
    return pl.pallas_call(
        functools.partial(_head_split_body, nh=nh, dh=dh),
        out_shape=jax.ShapeDtypeStruct((depth * rows * nh, dh), F32),
        grid=(nt,),
        in_specs=in_specs,
        out_specs=pl.BlockSpec((tr * nh, dh), lambda i: (layer * nt + i, 0)),
        input_output_aliases={} if stacked is None else {1: 0},
        compiler_params=_params(("parallel",), 32),
        name="head_split",
    )(*args)


def _group_norm_body(a_ref, b_ref, c_ref, d_ref, g_ref, o_ref):
    w = a_ref.shape[1]
    for i, ref in enumerate((a_ref, b_ref, c_ref, d_ref)):
        y = ref[...]
        r = lax.rsqrt(jnp.mean(y * y, axis=-1, keepdims=True) + EPS)
        o_ref[:, i * w:(i + 1) * w] = ((y * r) * g_ref[:, i * w:(i + 1) * w]).astype(o_ref.dtype)


def group_norm_concat(ya, yb, yc, yd, g):
    m, w = ya.shape
    tr = _row_tile(m, 256)
    spec = pl.BlockSpec((tr, w), lambda i: (i, 0))
    return pl.pallas_call(
        _group_norm_body,
        out_shape=jax.ShapeDtypeStruct((m, 4 * w), BF16),
        grid=(m // tr,),
        in_specs=[spec, spec, spec, spec, pl.BlockSpec((1, 4 * w), lambda i: (0, 0))],
        out_specs=pl.BlockSpec((tr, 4 * w), lambda i: (i, 0)),
        compiler_params=_params(("parallel",), 32),
        name="group_norm_concat",
    )(ya, yb, yc, yd, g.reshape(1, 4 * w))


def _mlstm_body(q_ref, k_ref, v_ref, o_ref, gates_ref, bias_ref, c0_ref, n0_ref, m0_ref,
                y_ref, c_ref, n_ref, m_ref, *, cs, dh):
    step = pl.program_id(1)

    @pl.when(step == 0)
    def _():
        c_ref[...] = c0_ref[...]
        n_ref[...] = n0_ref[...]
        m_ref[...] = m0_ref[...]

    g = gates_ref[...] + bias_ref[...]
    col = _iota2(g.shape, 1)
    x = jnp.where(col >= H_A, _log_sigmoid_pair(g)[0], g)
    eye = jnp.where(_iota2((LANES, LANES), 0) == _iota2((LANES, LANES), 1), 1.0, 0.0).astype(BF16)
    rr = _iota2((cs, cs), 0)
    cc = _iota2((cs, cs), 1)
    lower = jnp.where(rr >= cc, 1.0, 0.0).astype(BF16)
    upper = jnp.where(rr <= cc, 1.0, 0.0).astype(BF16)
    xs = _split_bf16(x, 3)
    x_rows = sum(_dot_nt(eye, p) for p in xs)
    fc_cols = sum(_dot(lower, p) for p in xs)
    fc_rows = sum(_dot(p, upper) for p in _split_bf16(x_rows, 3))
    causal = rr >= cc
    scale = dh ** -0.5

    for h in range(H_A):
        sl = slice(h * dh, (h + 1) * dh)
        q = q_ref[:, sl]
        k = k_ref[:, sl] * scale
        v = v_ref[:, sl]
        qb = q.astype(BF16)
        kb = k.astype(BF16)
        i_col = x[:, h:h + 1]
        i_row = x_rows[h:h + 1, :]
        f_col = fc_cols[:, H_A + h:H_A + h + 1]
        f_row = fc_rows[H_A + h:H_A + h + 1, :]
        m_prev = m_ref[0, :, h:h + 1]
        c_prev = c_ref[0, h]
        n_prev = n_ref[0, h:h + 1, :]

        dmat = jnp.where(causal, f_col - f_row + i_row, -jnp.inf)
        carry_log = f_col + m_prev
        m_t = jnp.maximum(jnp.max(dmat, axis=-1, keepdims=True), carry_log)
        s = _dot_nt(qb, kb) * jnp.exp(dmat - m_t)
        w_prev = jnp.exp(carry_log - m_t)
        num = _dot(s.astype(BF16), v.astype(BF16)) + w_prev * _dot_nt(qb, c_prev.astype(BF16))
        den = jnp.sum(s, axis=-1, keepdims=True) + w_prev * jnp.sum(q * n_prev, axis=-1, keepdims=True)
        hid = num / jnp.maximum(jnp.abs(den), jnp.exp(-m_t))
        y_ref[:, sl] = jax.nn.sigmoid(o_ref[:, sl]) * hid

        m_new = m_t[cs - 1:cs, :]
        f_last = f_col[cs - 1:cs, :]
        decay = jnp.exp(f_last + m_prev - m_new)
        w_end = jnp.exp(f_last - f_col + i_col - m_new)
        c_ref[0, h] = decay * c_prev + _dot_tn((w_end * v).astype(BF16), kb)
        n_ref[0, h:h + 1, :] = decay * n_prev + jnp.sum(w_end * k, axis=0, keepdims=True)
        m_ref[0, :, h:h + 1] = m_new


def mlstm(proj, gates, bias, c0, n0, m0, bsz, length):
    dh = SEC // H_A
    cs = _row_tile(length, MLSTM_CHUNK_MAX, SUBLANES)
    nc = length // cs
    rows = bsz * length

    def sec(s):
        return pl.BlockSpec((cs, SEC), lambda b, c, s=s: (b * nc + c, s))

    state_c = pl.BlockSpec((1, H_A, dh, dh), lambda b, c: (b, 0, 0, 0))
    state_n = pl.BlockSpec((1, H_A, dh), lambda b, c: (b, 0, 0))
    state_m = pl.BlockSpec((1, 1, H_A), lambda b, c: (b, 0, 0))
    y, c, n, m = pl.pallas_call(
        functools.partial(_mlstm_body, cs=cs, dh=dh),
        out_shape=(jax.ShapeDtypeStruct((rows, SEC), F32),
                   jax.ShapeDtypeStruct((bsz, H_A, dh, dh), F32),
                   jax.ShapeDtypeStruct((bsz, H_A, dh), F32),
                   jax.ShapeDtypeStruct((bsz, 1, H_A), F32)),
        grid=(bsz, nc),
        in_specs=[sec(QA), sec(KA), sec(VA), sec(OA),
                  pl.BlockSpec((cs, LANES), lambda b, c: (b * nc + c, 0)),
                  pl.BlockSpec((1, LANES), lambda b, c: (0, 0)),
                  state_c, state_n, state_m],
        out_specs=(pl.BlockSpec((cs, SEC), lambda b, c: (b * nc + c, 0)), state_c, state_n, state_m),
        compiler_params=_params(("parallel", "arbitrary"), 48),
        name="mlstm",
    )(proj, proj, proj, proj, gates, bias, c0, n0, m0.reshape(bsz, 1, H_A))
    return y, c, n, m.reshape(bsz, H_A)


def _softplus(z):
    return jnp.maximum(z, 0.0) + jnp.log(1.0 + jnp.exp(-jnp.abs(z)))


def _suffix_sum(sp, strict_upper2):
    hi, lo = _split_bf16(sp, 2)
    return _dot(jnp.concatenate([hi, lo], axis=1), strict_upper2)


def _strict_upper2(n):
    j = _iota2((2 * n, n), 0)
    j = jnp.where(j >= n, j - n, j)
    return jnp.where(j > _iota2((2 * n, n), 1), 1.0, 0.0).astype(BF16)


def _sb_prompt_body(bias_ref, q_ref, k_ref, v_ref, y_ref, *, t, dh, hpb):
    g = pl.program_id(1)
    i = pl.program_id(2)
    su2 = _strict_upper2(t)
    diag_mask = _iota2((t, t), 1) < _iota2((t, t), 0)
    heads = [slice(hh * dh, (hh + 1) * dh) for hh in range(hpb)]
    bias = [bias_ref[g * hpb + hh] for hh in range(hpb)]
    q = [(q_ref[:, sl] * (dh ** -0.5)).astype(BF16) for sl in heads]

    def tile(j, state, mask):
        rows = pl.ds(pl.multiple_of(j * t, t), t)
        z = [_dot_nt(q[hh], k_ref[rows, sl].astype(BF16)) + bias[hh] for hh, sl in enumerate(heads)]
        sp = [_softplus(x) if mask is None else jnp.where(mask, _softplus(x), 0.0) for x in z]
        later = [_suffix_sum(x, su2) + state[hh][0] for hh, x in enumerate(sp)]
        a = [jnp.exp((x - p) - w) for x, p, w in zip(z, sp, later)]
        if mask is not None:
            a = [jnp.where(mask, x, 0.0) for x in a]
        acc = [state[hh][1] + _dot(a[hh].astype(BF16), v_ref[rows, sl].astype(BF16))
               for hh, sl in enumerate(heads)]
        carry = [state[hh][0] + jnp.sum(sp[hh], axis=-1, keepdims=True) for hh in range(hpb)]
        return tuple(zip(carry, acc))

    zero = tuple((jnp.zeros((t, 1), F32), jnp.zeros((t, dh), F32)) for _ in heads)
    state = tile(i, zero, diag_mask)
    state = lax.fori_loop(0, i, lambda n, st: tile(i - 1 - n, st, None), state)
    for hh, sl in enumerate(heads):
        y_ref[:, sl] = state[hh][1]


def sb_prompt(proj, sb_bias, bsz, length):
    dh = SEC // H_B
    hpb = SB_HEADS_PER_STEP
    wide = hpb * dh
    t = _row_tile(length, SB_TILE, SUBLANES)
    nq = length // t
    cpb = SEC // wide
    return pl.pallas_call(
        functools.partial(_sb_prompt_body, t=t, dh=dh, hpb=hpb),
        out_shape=jax.ShapeDtypeStruct((bsz * length, SEC), F32),
        grid=(bsz, H_B // hpb, nq),
        in_specs=[pl.BlockSpec(memory_space=pltpu.SMEM),
                  pl.BlockSpec((t, wide), lambda b, g, i: (b * nq + i, QB * cpb + g)),
                  pl.BlockSpec((length, wide), lambda b, g, i: (b, KB * cpb + g)),
                  pl.BlockSpec((length, wide), lambda b, g, i: (b, VB * cpb + g))],
        out_specs=pl.BlockSpec((t, wide), lambda b, g, i: (b * nq + i, g)),
        compiler_params=_params(("parallel", "parallel", "arbitrary"), 40),
        name="sb_prompt",
    )(sb_bias, proj, proj, proj)


def _sb_sample_body(pt_ref, q_ref, kn_ref, vn_ref, bias_ref, *refs, lq, dh, page, ppb):
    del pt_ref
    page_refs = refs[:2 * ppb]
    y_ref, q2_ref, acc_ref, carry_ref = refs[2 * ppb:]
    j = pl.program_id(1)
    rows = H_B * lq
    bias = bias_ref[...]

    @pl.when(j == 0)
    def _():
        q = q_ref[...] * (dh ** -0.5)
        q2 = jnp.concatenate([q[:, h * dh:(h + 1) * dh] for h in range(H_B)], axis=0).astype(BF16)
        q2_ref[...] = q2
        q_wide = jnp.concatenate([q2] * H_B, axis=1)
        same_head = (_iota2((rows, SEC), 0) // lq) == (_iota2((rows, SEC), 1) // dh)
        q_bd = jnp.where(same_head, q_wide, jnp.zeros_like(q_wide))
        fill = jnp.zeros((LANES - lq, SEC), F32)
        kn = jnp.concatenate([kn_ref[...], fill], axis=0).astype(BF16)
        vn = jnp.concatenate([vn_ref[...], fill], axis=0).astype(BF16)
        z = _dot_nt(q_bd, kn) + bias
        mask = _iota2((rows, LANES), 1) < (_iota2((rows, LANES), 0) % lq)
        sp = jnp.where(mask, _softplus(z), 0.0)
        a = jnp.where(mask, jnp.exp((z - sp) - _suffix_sum(sp, _strict_upper2(LANES))), 0.0)
        full = _dot(a.astype(BF16), vn)
        acc_ref[...] = jnp.concatenate(
            [full[h * lq:(h + 1) * lq, h * dh:(h + 1) * dh] for h in range(H_B)], axis=0)
        carry_ref[...] = jnp.sum(sp, axis=-1, keepdims=True)

    q2 = q2_ref[...]
    su2 = _strict_upper2(page)
    def head_rows(ref, h):
        return ref[pl.ds(h, page, stride=H_B), :].astype(BF16)

    def own_rows(per_head):
        return jnp.concatenate([x[h * lq:(h + 1) * lq, :] for h, x in enumerate(per_head)], axis=0)

    z = [own_rows([_dot_nt(q2, head_rows(kp, h)) for h in range(H_B)]) + bias
         for kp in page_refs[:ppb]]
    sp = [_softplus(x) for x in z]
    later = [_suffix_sum(x, su2) for x in sp]
    carry = carry_ref[...]
    acc = acc_ref[...]
    for x, p, w, vp in zip(z, sp, later, page_refs[ppb:]):
        a = jnp.exp((x - p) - (w + carry)).astype(BF16)
        acc = acc + own_rows([_dot(a, head_rows(vp, h)) for h in range(H_B)])
        carry = carry + jnp.sum(p, axis=-1, keepdims=True)
    acc_ref[...] = acc
    carry_ref[...] = carry

    @pl.when(j == pl.num_programs(1) - 1)
    def _():
        acc = acc_ref[...]
        y_ref[...] = jnp.concatenate([acc[h * lq:(h + 1) * lq, :] for h in range(H_B)], axis=1)


def sb_sample(proj, sb_bias, cache_k, cache_v, page_table, layer, bsz, length):
    dh = SEC // H_B
    n_pages = page_table.shape[1]
    page = cache_k.shape[2]
    rows = H_B * length
    bias_rows = jnp.repeat(sb_bias.astype(F32), length).reshape(rows, 1)

    def new_sec(s):
        return pl.BlockSpec((length, SEC), lambda b, j, pt, s=s: (b, s))

    ppb = SB_PAGES_PER_STEP
    while n_pages % ppb:
        ppb -= 1

    def past(r):
        return pl.BlockSpec((None, None, page * H_B, dh),
                            lambda b, j, pt, r=r: (layer, pt[b, n_pages - 1 - (j * ppb + r)], 0, 0))

    page_specs = [past(r) for r in range(ppb)]
    grid_spec = pltpu.PrefetchScalarGridSpec(
        num_scalar_prefetch=1,
        grid=(bsz, n_pages // ppb),
        in_specs=[new_sec(QB), new_sec(KB), new_sec(VB),
                  pl.BlockSpec((rows, 1), lambda b, j, pt: (0, 0))] + page_specs + page_specs,
        out_specs=pl.BlockSpec((length, SEC), lambda b, j, pt: (b, 0)),
        scratch_shapes=[pltpu.VMEM((rows, dh), BF16), pltpu.VMEM((rows, dh), F32),
                        pltpu.VMEM((rows, 1), F32)],
    )
    k2d = cache_k.reshape(cache_k.shape[0], cache_k.shape[1], page * H_B, dh)
    v2d = cache_v.reshape(cache_v.shape[0], cache_v.shape[1], page * H_B, dh)
    return pl.pallas_call(
        functools.partial(_sb_sample_body, lq=length, dh=dh, page=page, ppb=ppb),
        out_shape=jax.ShapeDtypeStruct((bsz * length, SEC), F32),
        grid_spec=grid_spec,
        compiler_params=_params(("parallel", "arbitrary"), 40),
        name="sb_sample",
    )(page_table, proj, proj, proj, bias_rows, *([k2d] * ppb), *([v2d] * ppb))


def _conv_body(a_ref, g_ref, buf_ref, w_ref, b_ref, lg_ref, lb_ref, y_ref, new_ref, ext_ref, *, tl):
    hist = CONV_W - 1
    pad = 32 - hist
    step = pl.program_id(1)

    @pl.when(step == 0)
    def _():
        ext_ref[0:SUBLANES, :] = jnp.zeros((SUBLANES, SEC), F32)
        ext_ref[pad:32, :] = buf_ref[0]

    ext_ref[32:32 + tl, :] = a_ref[...] * jax.nn.sigmoid(g_ref[...])
    cols = []
    for c in range(SEC // LANES):
        lanes = slice(c * LANES, (c + 1) * LANES)
        slab = ext_ref[:, lanes]
        nrow = slab.shape[0]
        acc = jnp.zeros((tl, LANES), F32)
        for phase in range(SUBLANES):
            taps = [j for j in range(CONV_W) if (pad + j) % SUBLANES == phase]
            shifted = slab if phase == 0 else pltpu.roll(slab, nrow - phase, axis=0)
            for j in taps:
                start = pad + j - phase
                acc = acc + w_ref[j:j + 1, lanes] * shifted[start:start + tl, :]
        cols.append(acc)
    y = jnp.concatenate(cols, axis=1) + b_ref[...]
    yc = y - jnp.mean(y, axis=-1, keepdims=True)
    yn = yc * lax.rsqrt(jnp.mean(yc * yc, axis=-1, keepdims=True) + EPS)
    yn = yn * lg_ref[...] + lb_ref[...]
    y_ref[...] = yn * jax.nn.sigmoid(yn)
    new_ref[0] = ext_ref[tl + pad:tl + 32, :]
    if tl >= 32:
        ext_ref[0:32, :] = ext_ref[tl:tl + 32, :]


def conformer_conv(proj, buf, w, b, ln_g, ln_b, bsz, length):
    tl = _row_tile(length, CONV_TILE, SUBLANES)
    nt = length // tl
    assert nt == 1 or tl >= 32
    hist = CONV_W - 1
    vec = pl.BlockSpec((1, SEC), lambda bb, t: (0, 0))
    return pl.pallas_call(
        functools.partial(_conv_body, tl=tl),
        out_shape=(jax.ShapeDtypeStruct((bsz * length, SEC), F32),
                   jax.ShapeDtypeStruct((bsz, hist, SEC), F32)),
        grid=(bsz, nt),
        in_specs=[pl.BlockSpec((tl, SEC), lambda bb, t: (bb * nt + t, CA)),
                  pl.BlockSpec((tl, SEC), lambda bb, t: (bb * nt + t, CG)),
                  pl.BlockSpec((1, hist, SEC), lambda bb, t: (bb, 0, 0)),
                  pl.BlockSpec((CONV_W, SEC), lambda bb, t: (0, 0)),
                  vec, vec, vec],
        out_specs=(pl.BlockSpec((tl, SEC), lambda bb, t: (bb * nt + t, 0)),
                   pl.BlockSpec((1, hist, SEC), lambda bb, t: (bb, 0, 0))),
        scratch_shapes=[pltpu.VMEM((32 + tl, SEC), F32)],
        compiler_params=_params(("parallel", "arbitrary"), 32),
        name="conformer_conv",
    )(proj, proj, buf, w, b.reshape(1, SEC), ln_g.reshape(1, SEC), ln_b.reshape(1, SEC))


def _s5_drive(u_ref, br_ref, bi_ref, sr_ref, si_ref, perm):
    nblk, cin, nst = br_ref.shape
    ub = u_ref[...].astype(BF16)
    if perm is not None:
        ub = _dot(perm, ub).astype(BF16)
    for blk in range(nblk):
        ublk = ub[:, blk * cin:(blk + 1) * cin]
        sr_ref[:, blk * nst:(blk + 1) * nst] = _dot(ublk, br_ref[blk])
        si_ref[:, blk * nst:(blk + 1) * nst] = _dot(ublk, bi_ref[blk])


def _s5_readout(sr_ref, si_ref, cr_ref, ci_ref):
    nblk, nst, _ = cr_ref.shape
    ys = []
    for blk in range(nblk):
        lanes = slice(blk * nst, (blk + 1) * nst)
        ys.append(_dot(sr_ref[:, lanes].astype(BF16), cr_ref[blk])
                  - _dot(si_ref[:, lanes].astype(BF16), ci_ref[blk]))
    return jnp.concatenate(ys, axis=1)


def _s5_gate(y, gw_ref, gb_ref, y_ref):
    g = jax.nn.gelu(y)
    gate = _dot(g.astype(BF16), gw_ref[...]) + gb_ref[...]
    y_ref[...] = g * jax.nn.sigmoid(gate)


def _s5_segmented_body(u_ref, h0r_ref, h0i_ref, ar_ref, ai_ref, br_ref, bi_ref, cr_ref, ci_ref, d_ref,
                       gw_ref, gb_ref, y_ref, hr_ref, hi_ref, sr_ref, si_ref, pr_ref, pi_ref,
                       yp_ref, yt_ref, *, tl):
    step = pl.program_id(1)
    nseg = SUBLANES
    slen = tl // nseg
    nblk = br_ref.shape[0]
    nst = br_ref.shape[2]

    @pl.when(step == 0)
    def _():
        hr_ref[...] = h0r_ref[...]
        hi_ref[...] = h0i_ref[...]
        for blk in range(nblk):
            lanes = slice(blk * nst, (blk + 1) * nst)
            ar = ar_ref[:, lanes]
            ai = ai_ref[:, lanes]
            qr, qi = ar, ai
            for t in range(slen):
                pr_ref[t:t + 1, lanes] = qr
                pi_ref[t:t + 1, lanes] = qi
                qr, qi = qr * ar - qi * ai, qr * ai + qi * ar

    dst = _iota2((tl, tl), 0)
    src = (dst % nseg) * slen + dst // nseg
    perm = jnp.where(_iota2((tl, tl), 1) == src, 1.0, 0.0).astype(BF16)
    _s5_drive(u_ref, br_ref, bi_ref, sr_ref, si_ref, perm)

    first = _iota2((nseg, nst), 0) == 0
    for blk in range(nblk):
        lanes = slice(blk * nst, (blk + 1) * nst)
        ar = ar_ref[:, lanes]
        ai = ai_ref[:, lanes]

        def local_step(t, state, lanes=lanes, ar=ar, ai=ai):
            hr, hi = state
            rows = pl.ds(pl.multiple_of(t * nseg, nseg), nseg)
            nr = ar * hr - ai * hi + sr_ref[rows, lanes]
            ni = ar * hi + ai * hr + si_ref[rows, lanes]
            sr_ref[rows, lanes] = nr
            si_ref[rows, lanes] = ni
            return nr, ni

        start = (jnp.where(first, hr_ref[0, :, lanes], 0.0), jnp.where(first, hi_ref[0, :, lanes], 0.0))
        er, ei = lax.fori_loop(0, slen, local_step, start, unroll=4)

        wr = pr_ref[slen - 1:slen, lanes]
        wi = pi_ref[slen - 1:slen, lanes]
        tr, ti = er[0:1, :], ei[0:1, :]
        starts_r = [jnp.zeros_like(tr)]
        starts_i = [jnp.zeros_like(ti)]
        for s in range(1, nseg):
            starts_r.append(tr)
            starts_i.append(ti)
            tr, ti = er[s:s + 1, :] + wr * tr - wi * ti, ei[s:s + 1, :] + wr * ti + wi * tr
        hr_ref[0, :, lanes] = tr
        hi_ref[0, :, lanes] = ti
        gr = jnp.concatenate(starts_r, axis=0)
        gi = jnp.concatenate(starts_i, axis=0)

        def fix_step(t, carry, lanes=lanes, gr=gr, gi=gi):
            rows = pl.ds(pl.multiple_of(t * nseg, nseg), nseg)
            qr = pr_ref[pl.ds(t, 1), lanes]
            qi = pi_ref[pl.ds(t, 1), lanes]
            sr_ref[rows, lanes] += qr * gr - qi * gi
            si_ref[rows, lanes] += qr * gi + qi * gr
            return carry

        lax.fori_loop(0, slen, fix_step, 0, unroll=4)

    yp = _s5_readout(sr_ref, si_ref, cr_ref, ci_ref)
    for c in range(SEC // LANES):
        yp_ref[c] = yp[:, c * LANES:(c + 1) * LANES]
    for c in range(SEC // LANES):
        for s in range(nseg):
            yt_ref[s * slen:(s + 1) * slen, c * LANES:(c + 1) * LANES] = (
                yp_ref[c, pl.ds(s, slen, stride=nseg), :])
    _s5_gate(yt_ref[...] + d_ref[...] * u_ref[...], gw_ref, gb_ref, y_ref)


def _s5_body(u_ref, h0r_ref, h0i_ref, ar_ref, ai_ref, br_ref, bi_ref, cr_ref, ci_ref, d_ref,
             gw_ref, gb_ref, y_ref, hr_ref, hi_ref, sr_ref, si_ref, *, tl):
    step = pl.program_id(1)
    nblk = br_ref.shape[0]
    nst = br_ref.shape[2]

    @pl.when(step == 0)
    def _():
        hr_ref[...] = h0r_ref[...]
        hi_ref[...] = h0i_ref[...]

    _s5_drive(u_ref, br_ref, bi_ref, sr_ref, si_ref, None)

    for blk in range(nblk):
        lanes = slice(blk * nst, (blk + 1) * nst)
        ar = ar_ref[:, lanes]
        ai = ai_ref[:, lanes]

        def scan_step(t, state, lanes=lanes, ar=ar, ai=ai):
            hr, hi = state
            row = pl.ds(t, 1)
            nr = ar * hr - ai * hi + sr_ref[row, lanes]
            ni = ar * hi + ai * hr + si_ref[row, lanes]
            sr_ref[row, lanes] = nr
            si_ref[row, lanes] = ni
            return nr, ni

        hr, hi = lax.fori_loop(0, tl, scan_step, (hr_ref[0, :, lanes], hi_ref[0, :, lanes]),
                               unroll=SUBLANES)
        hr_ref[0, :, lanes] = hr
        hi_ref[0, :, lanes] = hi

    y = _s5_readout(sr_ref, si_ref, cr_ref, ci_ref) + d_ref[...] * u_ref[...]
    _s5_gate(y, gw_ref, gb_ref, y_ref)


def s5_discretize(a_re, a_im, log_dt, b_re, b_im, c_re, c_im):
    ngrp, nst = a_re.shape
    dt = jnp.exp(log_dt.astype(F32))[:, None]
    ar = a_re.astype(F32)
    ai = a_im.astype(F32)
    mag = jnp.exp(ar * dt)
    abar_re = mag * jnp.cos(ai * dt)
    abar_im = mag * jnp.sin(ai * dt)
    zr = abar_re - 1.0
    zi = abar_im
    den = ar * ar + ai * ai
    coef_re = (zr * ar + zi * ai) / den
    coef_im = (zi * ar - zr * ai) / den
    br = b_re.astype(F32)
    bi = b_im.astype(F32)
    bbar_re = coef_re[..., None] * br - coef_im[..., None] * bi
    bbar_im = coef_re[..., None] * bi + coef_im[..., None] * br
    gb = S5_GROUP_BLOCK
    nblk = ngrp // gb
    eye = jnp.eye(gb, dtype=F32)

    def in_blocks(bbar):
        t = bbar.reshape(nblk, gb, nst, S5_CH)
        return jnp.einsum('bgnc,gh->bgchn', t, eye).reshape(nblk, gb * S5_CH, gb * nst).astype(BF16)

    def out_blocks(c):
        t = c.astype(F32).reshape(nblk, gb, S5_CH, nst)
        return jnp.einsum('bgcn,gh->bgnhc', t, eye).reshape(nblk, gb * nst, gb * S5_CH).astype(BF16)

    return (abar_re.reshape(1, ngrp * nst), abar_im.reshape(1, ngrp * nst),
            in_blocks(bbar_re), in_blocks(bbar_im), out_blocks(c_re), out_blocks(c_im))


def s5(proj, h0_re, h0_im, disc, d, gate_w, gate_b, bsz, length):
    abar_re, abar_im, b_re, b_im, c_re, c_im = disc
    nstate = abar_re.shape[1]
    tl = _row_tile(length, S5_TILE, SUBLANES)
    nt = length // tl
    state = pl.BlockSpec((1, 1, nstate), lambda b, t: (b, 0, 0))

    def whole(a):
        return pl.BlockSpec(a.shape, lambda b, t, nd=a.ndim: (0,) * nd)

    d2 = d.reshape(1, SEC)
    gb2 = gate_b.reshape(1, SEC)
    scratch = [pltpu.VMEM((tl, nstate), F32), pltpu.VMEM((tl, nstate), F32)]
    body = _s5_body
    if tl % (SUBLANES * SUBLANES) == 0:
        body = _s5_segmented_body
        slen = tl // SUBLANES
        scratch += [pltpu.VMEM((slen, nstate), F32), pltpu.VMEM((slen, nstate), F32),
                    pltpu.VMEM((SEC // LANES, tl, LANES), F32), pltpu.VMEM((tl, SEC), F32)]
    y, hr, hi = pl.pallas_call(
        functools.partial(body, tl=tl),
        out_shape=(jax.ShapeDtypeStruct((bsz * length, SEC), F32),
                   jax.ShapeDtypeStruct((bsz, 1, nstate), F32),
                   jax.ShapeDtypeStruct((bsz, 1, nstate), F32)),
        grid=(bsz, nt),
        in_specs=[pl.BlockSpec((tl, SEC), lambda b, t: (b * nt + t, UD)), state, state,
                  whole(abar_re), whole(abar_im), whole(b_re), whole(b_im), whole(c_re), whole(c_im),
                  whole(d2), whole(gate_w), whole(gb2)],
        out_specs=(pl.BlockSpec((tl, SEC), lambda b, t: (b * nt + t, 0)), state, state),
        scratch_shapes=scratch,
        compiler_params=_params(("parallel", "arbitrary"), 48),
        name="s5",
    )(proj, h0_re.reshape(bsz, 1, nstate), h0_im.reshape(bsz, 1, nstate),
      abar_re, abar_im, b_re, b_im, c_re, c_im, d2, gate_w, gb2)
    return y, hr, hi


def _cross_body(q_ref, k_ref, v_ref, o_ref, *, dh):
    scale = dh ** -0.5
    for h in range(H_X):
        sl = slice(h * dh, (h + 1) * dh)
        s = _dot_nt(q_ref[:, sl].astype(BF16), k_ref[:, sl].astype(BF16)) * scale
        e = jnp.exp(s - jnp.max(s, axis=-1, keepdims=True))
        p = e / jnp.sum(e, axis=-1, keepdims=True)
        o_ref[:, sl] = _dot(p.astype(BF16), v_ref[:, sl].astype(BF16)).astype(o_ref.dtype)


def cross_attend(q, mem_k, mem_v, k_col, v_col, bsz, length, n_mem):
    wx = q.shape[1]
    tq = _row_tile(length, 512, SUBLANES)
    nq = length // tq
    return pl.pallas_call(
        functools.partial(_cross_body, dh=wx // H_X),
        out_shape=jax.ShapeDtypeStruct((bsz * length, wx), F32),
        grid=(bsz, nq),
        in_specs=[pl.BlockSpec((tq, wx), lambda b, i: (b * nq + i, 0)),
                  pl.BlockSpec((n_mem, wx), lambda b, i: (b, k_col)),
                  pl.BlockSpec((n_mem, wx), lambda b, i: (b, v_col))],
        out_specs=pl.BlockSpec((tq, wx), lambda b, i: (b * nq + i, 0)),
        compiler_params=_params(("parallel", "arbitrary"), 32),
        name="cross_attend",
    )(q, mem_k, mem_v)


def _decoder_layer(x, bsz, length, mem_k, mem_v, k_col, v_col, n_mem, sb_fn, mlstm_state, conv_buf,
                   s5_state, p, kv_stacked, layer, depth):
    h, gates = rmsnorm_gates(x, p['norm_mix_pre'], p['w_gates'])
    proj = matmul(h, p['w_in'], F32)
    y_a, c_new, n_new, m_new = mlstm(proj, gates, p['gate_bias'], *mlstm_state, bsz, length)
    y_b = sb_fn(proj)
    y_c, conv_new = conformer_conv(proj, conv_buf, p['conv_w'], p['conv_b'], p['conv_ln_g'],
                                   p['conv_ln_b'], bsz, length)
    y_d, s5_re, s5_im = s5(proj, s5_state[0], s5_state[1], p['s5_disc'], p['s5_d'], p['s5_gate_w'],
                           p['s5_gate_b'], bsz, length)
    mixed = group_norm_concat(y_a, y_b, y_c, y_d, p['g_group'])
    x, hx = matmul_norm_res(mixed, p['w_out'], p['norm_mix_post'], x, p['norm_x_pre'])
    q = matmul(hx, p['w_xq'], F32)
    o = cross_attend(q, mem_k, mem_v, k_col, v_col, bsz, length, n_mem)
    x, hf = matmul_norm_res(o, p['w_xo'], p['norm_x_post'], x, p['norm_ffn_pre'])
    x = ffn(hf, p['w_up'], p['w_down'], p['norm_ffn_post'], x)
    kv_stacked = (head_split(proj, KB, H_B, kv_stacked[0], layer, depth),
                  head_split(proj, VB, H_B, kv_stacked[1], layer, depth))
    return x, kv_stacked, (c_new, n_new, m_new), conv_new, (s5_re, s5_im)


def kernel(x_prompt, x_sample, cache_sb_k, cache_sb_v, page_table, state_mlstm_c, state_mlstm_n, state_mlstm_m, state_conv, state_s5_re, state_s5_im, cache_mem_k, cache_mem_v, mem_prompt, norm_mix_pre, w_in, b_mlstm_gates, sb_bias, conv_w, conv_b, conv_ln_g, conv_ln_b, s5_a_re, s5_a_im, s5_log_dt, s5_b_re, s5_b_im, s5_c_re, s5_c_im, s5_d, s5_gate_w, s5_gate_b, g_group, w_out, norm_mix_post, norm_mem, norm_x_pre, w_xq, w_xk, w_xv, w_xo, norm_x_post, norm_ffn_pre, w_up, w_down, norm_ffn_post):
    n_b, seq, d_model = x_prompt.shape
    n_db, dec_seq, _ = x_sample.shape
    depth = w_in.shape[0]
    n_mem = mem_prompt.shape[1]
    w_x = w_xq.shape[2]
    dh_a = SEC // H_A
    ngrp = s5_a_re.shape[1]
    n_gate = 2 * H_A
    gate_lo = 4 * SEC

    xp = x_prompt.reshape(n_b * seq, d_model)
    xs = x_sample.reshape(n_db * dec_seq, d_model)
    mem2d = mem_prompt.reshape(n_b * n_mem, d_model)
    outs = [[] for _ in range(14)]
    kv_p = (None, None)
    kv_s = (None, None)
    for l in range(depth):
        w_in_main, w_in_gates = cast_w_in(w_in, l, gate_lo, n_gate)
        p = {
            'norm_mix_pre': norm_mix_pre[l],
            'w_in': w_in_main,
            'w_gates': w_in_gates,
            'gate_bias': jnp.pad(b_mlstm_gates[l].astype(F32), (0, LANES - n_gate)).reshape(1, LANES),
            'conv_w': conv_w[l], 'conv_b': conv_b[l], 'conv_ln_g': conv_ln_g[l], 'conv_ln_b': conv_ln_b[l],
            's5_disc': s5_discretize(s5_a_re[l], s5_a_im[l], s5_log_dt[l], s5_b_re[l], s5_b_im[l],
                                     s5_c_re[l], s5_c_im[l]),
            's5_d': s5_d[l], 's5_gate_w': s5_gate_w[l].astype(BF16), 's5_gate_b': s5_gate_b[l],
            'g_group': g_group[l], 'w_out': cast_bf16(w_out, l), 'norm_mix_post': norm_mix_post[l],
            'norm_x_pre': norm_x_pre[l], 'w_xq': w_xq[l].astype(BF16), 'w_xo': w_xo[l].astype(BF16),
            'norm_x_post': norm_x_post[l], 'norm_ffn_pre': norm_ffn_pre[l],
            'w_up': cast_bf16_col_tiles(w_up, l, FFN_TILE), 'w_down': cast_bf16(w_down, l),
            'norm_ffn_post': norm_ffn_post[l],
        }
        w_kv = jnp.concatenate([w_xk[l], w_xv[l]], axis=1).astype(BF16)
        mem_kv = matmul(rmsnorm_bf16(mem2d, norm_mem[l]), w_kv, F32, tm_cap=512)
        zero_mlstm = (jnp.zeros((n_b, H_A, dh_a, dh_a), F32), jnp.zeros((n_b, H_A, dh_a), F32),
                      jnp.zeros((n_b, H_A), F32))
        zero_s5 = (jnp.zeros((n_b, ngrp, S5_N), F32), jnp.zeros((n_b, ngrp, S5_N), F32))
        sb_p = functools.partial(sb_prompt, sb_bias=sb_bias[l], bsz=n_b, length=seq)
        xp, kv_p, mst_p, cnv_p, s5_p = _decoder_layer(
            xp, n_b, seq, mem_kv, mem_kv, 0, 1, n_mem, sb_p, zero_mlstm,
            jnp.zeros((n_b, CONV_W - 1, SEC), F32), zero_s5, p, kv_p, l, depth)
        sb_s = functools.partial(sb_sample, sb_bias=sb_bias[l], cache_k=cache_sb_k, cache_v=cache_sb_v,
                                 page_table=page_table, layer=l, bsz=n_db, length=dec_seq)
        xs, kv_s, mst_s, cnv_s, s5_s = _decoder_layer(
            xs, n_db, dec_seq, cache_mem_k[l].reshape(n_db * n_mem, w_x),
            cache_mem_v[l].reshape(n_db * n_mem, w_x), 0, 0, n_mem, sb_s,
            (state_mlstm_c[l], state_mlstm_n[l], state_mlstm_m[l]), state_conv[l],
            (state_s5_re[l], state_s5_im[l]), p, kv_s, l, depth)
        mk = mem_kv[:, :w_x].reshape(n_b, n_mem, H_X, w_x // H_X)
        mv = mem_kv[:, w_x:].reshape(n_b, n_mem, H_X, w_x // H_X)
        s5_shape = lambda a: a.reshape(a.shape[0], ngrp, S5_N)
        layer_out = (mst_p[0], mst_p[1], mst_p[2], mst_s[0], mst_s[1], mst_s[2],
                     cnv_p, cnv_s, s5_shape(s5_p[0]), s5_shape(s5_p[1]), s5_shape(s5_s[0]),
                     s5_shape(s5_s[1]), mk, mv)
        for acc, val in zip(outs, layer_out):
            acc.append(val)
    stacked = tuple(jnp.stack(vals, axis=0) for vals in outs)
    dh_b = SEC // H_B
    kv = tuple(a.reshape(depth, n, t, H_B, dh_b)
               for a, n, t in ((kv_p[0], n_b, seq), (kv_p[1], n_b, seq),
                               (kv_s[0], n_db, dec_seq), (kv_s[1], n_db, dec_seq)))
    return (xp.reshape(n_b, seq, d_model), xs.reshape(n_db, dec_seq, d_model)) + kv + stacked
```

```python
import functools

import jax
import jax.numpy as jnp
from jax import lax
from jax.experimental import pallas as pl
from jax.experimental.pallas import tpu as pltpu

F32 = jnp.float32
BF16 = jnp.bfloat16
EPS = 1e-6

H_A = 4
H_B = 8
CONV_W = 31
S5_CH = 16
S5_N = 64
H_X = 4
MLSTM_CHUNK_MAX = 256
SB_TILE = 256
SB_HEADS_PER_STEP = 4
SB_PAGES_PER_STEP = 8
CONV_TILE = 64
S5_TILE = 256
FFN_TILE = 1024
FFN_OUT_CHUNK = 1024
FFN_NORM_ROWS = 32
S5_GROUP_BLOCK = 16

LANES = 128
SUBLANES = 8
MIB = 1024 * 1024

SEC = 1024
QA, KA, VA, OA, QB, KB, VB, CA, CG, UD = range(10)


def _params(semantics, vmem_mib):
    return pltpu.CompilerParams(dimension_semantics=semantics, vmem_limit_bytes=vmem_mib * MIB)


def _row_tile(m, cap, mult=16):
    t = min(m, cap)
    while m % t or t % mult:
        t -= 1
    return t


def _log_sigmoid_pair(z):
    sp = jnp.log1p(jnp.exp(-jnp.abs(z)))
    return jnp.minimum(z, 0.0) - sp, -(jnp.maximum(z, 0.0) + sp)


def _split_bf16(x, terms):
    out = []
    r = x
    for _ in range(terms):
        p = r.astype(BF16)
        out.append(p)
        r = r - p.astype(F32)
    return out


def _dot(a, b):
    return jnp.dot(a, b, preferred_element_type=F32)


def _dot_nt(a, b):
    return lax.dot_general(a, b, (((1,), (1,)), ((), ())), preferred_element_type=F32)


def _dot_tn(a, b):
    return lax.dot_general(a, b, (((0,), (0,)), ((), ())), preferred_element_type=F32)


def _iota2(shape, dim):
    return lax.broadcasted_iota(jnp.int32, shape, dim)


def _rmsnorm_body(x_ref, g_ref, o_ref):
    x = x_ref[...]
    r = lax.rsqrt(jnp.mean(x * x, axis=-1, keepdims=True) + EPS)
    o_ref[...] = ((x * r) * g_ref[...]).astype(o_ref.dtype)


def rmsnorm_bf16(x, g):
    m, d = x.shape
    tr = _row_tile(m, 256)
    return pl.pallas_call(
        _rmsnorm_body,
        out_shape=jax.ShapeDtypeStruct((m, d), BF16),
        grid=(m // tr,),
        in_specs=[pl.BlockSpec((tr, d), lambda i: (i, 0)), pl.BlockSpec((1, d), lambda i: (0, 0))],
        out_specs=pl.BlockSpec((tr, d), lambda i: (i, 0)),
        compiler_params=_params(("parallel",), 32),
        name="rmsnorm",
    )(x, g.reshape(1, d))


def _rmsnorm_gates_body(x_ref, g_ref, wg_ref, o_ref, gates_ref):
    x = x_ref[...]
    r = lax.rsqrt(jnp.mean(x * x, axis=-1, keepdims=True) + EPS)
    h = ((x * r) * g_ref[...]).astype(BF16)
    o_ref[...] = h
    gates_ref[...] = _dot(h, wg_ref[...])


def rmsnorm_gates(x, g, wg):
    m, d = x.shape
    tr = _row_tile(m, 256)
    return pl.pallas_call(
        _rmsnorm_gates_body,
        out_shape=(jax.ShapeDtypeStruct((m, d), BF16), jax.ShapeDtypeStruct((m, LANES), F32)),
        grid=(m // tr,),
        in_specs=[pl.BlockSpec((tr, d), lambda i: (i, 0)), pl.BlockSpec((1, d), lambda i: (0, 0)),
                  pl.BlockSpec((d, LANES), lambda i: (0, 0))],
        out_specs=(pl.BlockSpec((tr, d), lambda i: (i, 0)), pl.BlockSpec((tr, LANES), lambda i: (i, 0))),
        compiler_params=_params(("parallel",), 32),
        name="rmsnorm_gates",
    )(x, g.reshape(1, d), wg)


def _matmul_body(a_ref, w_ref, o_ref):
    o_ref[...] = _dot(a_ref[...], w_ref[...]).astype(o_ref.dtype)


def matmul(a, w, out_dtype, tm_cap=1024, tn_cap=1024):
    m, k = a.shape
    n = w.shape[1]
    tm = _row_tile(m, tm_cap)
    tn = _row_tile(n, tn_cap, LANES)
    return pl.pallas_call(
        _matmul_body,
        out_shape=jax.ShapeDtypeStruct((m, n), out_dtype),
        grid=(m // tm, n // tn),
        in_specs=[pl.BlockSpec((tm, k), lambda i, j: (i, 0)), pl.BlockSpec((k, tn), lambda i, j: (0, j))],
        out_specs=pl.BlockSpec((tm, tn), lambda i, j: (i, j)),
        compiler_params=_params(("parallel", "arbitrary"), 56),
        name="matmul",
    )(a, w)


def _norm_residual(y, g_ref, res_ref, o_ref):
    r = lax.rsqrt(jnp.mean(y * y, axis=-1, keepdims=True) + EPS)
    o_ref[...] = res_ref[...] + (y * r) * g_ref[...]


def _norm_residual_next(y, g_ref, res_ref, g_next_ref, o_ref, h_ref):
    _norm_residual(y, g_ref, res_ref, o_ref)
    x = o_ref[...]
    r = lax.rsqrt(jnp.mean(x * x, axis=-1, keepdims=True) + EPS)
    h_ref[...] = ((x * r) * g_next_ref[...]).astype(h_ref.dtype)


def _matmul_norm_res_body(a_ref, w_ref, g_ref, res_ref, g_next_ref, o_ref, h_ref):
    _norm_residual_next(_dot(a_ref[...].astype(BF16), w_ref[...]), g_ref, res_ref, g_next_ref,
                        o_ref, h_ref)


def _mix_out_body(ya_ref, yb_ref, yc_ref, yd_ref, gg_ref, w_ref, g_ref, res_ref, g_next_ref,
                  o_ref, h_ref):
    wg = ya_ref.shape[1]
    parts = []
    for i, ref in enumerate((ya_ref, yb_ref, yc_ref, yd_ref)):
        y = ref[...]
        r = lax.rsqrt(jnp.mean(y * y, axis=-1, keepdims=True) + EPS)
        parts.append(((y * r) * gg_ref[:, i * wg:(i + 1) * wg]).astype(BF16))
    mixed = jnp.concatenate(parts, axis=1)
    _norm_residual_next(_dot(mixed, w_ref[...]), g_ref, res_ref, g_next_ref, o_ref, h_ref)


def mix_out_proj(ya, yb, yc, yd, g_group, w, g, res, g_next):
    m, wg = ya.shape
    n = w.shape[1]
    tm = _row_tile(m, 128)
    rows = pl.BlockSpec((tm, n), lambda i: (i, 0))
    part = pl.BlockSpec((tm, wg), lambda i: (i, 0))
    vec = pl.BlockSpec((1, n), lambda i: (0, 0))
    return pl.pallas_call(
        _mix_out_body,
        out_shape=(jax.ShapeDtypeStruct((m, n), F32), jax.ShapeDtypeStruct((m, n), BF16)),
        grid=(m // tm,),
        in_specs=[part, part, part, part, pl.BlockSpec((1, 4 * wg), lambda i: (0, 0)),
                  pl.BlockSpec(w.shape, lambda i: (0, 0), pipeline_mode=pl.Buffered(1)),
                  vec, rows, vec],
        out_specs=(rows, rows),
        compiler_params=_params(("parallel",), 60),
        name="mix_out_proj",
    )(ya, yb, yc, yd, g_group.reshape(1, 4 * wg), w, g.reshape(1, n), res, g_next.reshape(1, n))


def matmul_norm_res(a, w, g, res, g_next):
    m, kdim = a.shape
    n = w.shape[1]
    w_bytes = kdim * n * w.dtype.itemsize
    tm = _row_tile(m, 128 if w_bytes > 16 * MIB else 256)
    rows = pl.BlockSpec((tm, n), lambda i: (i, 0))
    vec = pl.BlockSpec((1, n), lambda i: (0, 0))
    return pl.pallas_call(
        _matmul_norm_res_body,
        out_shape=(jax.ShapeDtypeStruct((m, n), F32), jax.ShapeDtypeStruct((m, n), BF16)),
        grid=(m // tm,),
        in_specs=[pl.BlockSpec((tm, kdim), lambda i: (i, 0)),
                  pl.BlockSpec((kdim, n), lambda i: (0, 0), pipeline_mode=pl.Buffered(1)),
                  vec, rows, vec],
        out_specs=(rows, rows),
        compiler_params=_params(("parallel",), 60 if w_bytes > 16 * MIB else 40),
        name="matmul_norm_res",
    )(a, w, g.reshape(1, n), res, g_next.reshape(1, n))


def _ffn_body(h_ref, wu_ref, wd_ref, g_ref, res_ref, o_ref, *, nf):
    f = pl.program_id(1)

    @pl.when(f == 0)
    def _():
        o_ref[...] = jnp.zeros_like(o_ref)

    u = _dot(h_ref[...], wu_ref[...])
    u = jnp.square(jnp.maximum(u, 0.0)).astype(BF16)
    d = o_ref.shape[1]
    for c in range(0, d, FFN_OUT_CHUNK):
        cols = slice(c, c + FFN_OUT_CHUNK)
        o_ref[:, cols] += _dot(u, wd_ref[:, cols])

    @pl.when(f == nf - 1)
    def _():
        def rows_step(r, carry):
            rows = pl.ds(pl.multiple_of(r * FFN_NORM_ROWS, FFN_NORM_ROWS), FFN_NORM_ROWS)
            _norm_residual(o_ref[rows, :], g_ref, res_ref.at[rows, :], o_ref.at[rows, :])
            return carry

        lax.fori_loop(0, o_ref.shape[0] // FFN_NORM_ROWS, rows_step, 0)


def ffn(h, w_up_tiles, w_down, g, res, tm_cap=512):
    m, d = h.shape
    nf, _, tf = w_up_tiles.shape
    tm = _row_tile(m, tm_cap)
    once = pl.Buffered(1)
    return pl.pallas_call(
        functools.partial(_ffn_body, nf=nf),
        out_shape=jax.ShapeDtypeStruct((m, d), F32),
        grid=(m // tm, nf),
        in_specs=[pl.BlockSpec((tm, d), lambda i, f: (i, 0), pipeline_mode=once),
                  pl.BlockSpec((None, d, tf), lambda i, f: (f, 0, 0)),
                  pl.BlockSpec((tf, d), lambda i, f: (f, 0)),
                  pl.BlockSpec((1, d), lambda i, f: (0, 0)),
                  pl.BlockSpec((tm, d), lambda i, f: (i, 0), pipeline_mode=once)],
        out_specs=pl.BlockSpec((tm, d), lambda i, f: (i, 0), pipeline_mode=once),
        compiler_params=_params(("parallel", "arbitrary"), 60),
        name="ffn",
    )(h, w_up_tiles, w_down, g.reshape(1, d), res)


def _cast_body(x_ref, o_ref):
    o_ref[...] = x_ref[...].astype(o_ref.dtype)


def cast_bf16_col_tiles(x, layer, tc):
    _, r, c = x.shape
    tr = _row_tile(r, 1024)
    return pl.pallas_call(
        _cast_body,
        out_shape=jax.ShapeDtypeStruct((c // tc, r, tc), BF16),
        grid=(r // tr, c // tc),
        in_specs=[pl.BlockSpec((None, tr, tc), lambda i, j: (layer, i, j))],
        out_specs=pl.BlockSpec((None, tr, tc), lambda i, j: (j, i, 0)),
        compiler_params=_params(("parallel", "parallel"), 40),
        name="cast_bf16_col_tiles",
    )(x)


def cast_bf16(x, layer):
    _, r, c = x.shape
    tr = _row_tile(r, 512)
    tc = _row_tile(c, 4096, LANES)
    return pl.pallas_call(
        _cast_body,
        out_shape=jax.ShapeDtypeStruct((r, c), BF16),
        grid=(r // tr, c // tc),
        in_specs=[pl.BlockSpec((None, tr, tc), lambda i, j: (layer, i, j))],
        out_specs=pl.BlockSpec((tr, tc), lambda i, j: (i, j)),
        compiler_params=_params(("parallel", "parallel"), 40),
        name="cast_bf16",
    )(x)


def _cast_w_in_body(x_ref, nxt_ref, o_ref, gates_ref, *, aligned_tiles, skip):
    j = pl.program_id(1)
    tc = x_ref.shape[0]

    @pl.when(j < aligned_tiles)
    def _():
        o_ref[...] = x_ref[...].T.astype(o_ref.dtype)

    @pl.when(j >= aligned_tiles)
    def _():
        wide = jnp.concatenate([x_ref[...], nxt_ref[...]], axis=0)
        o_ref[...] = wide[skip:skip + tc, :].T.astype(o_ref.dtype)

    @pl.when(j == aligned_tiles)
    def _():
        fill = jnp.zeros((LANES - skip, x_ref.shape[1]), F32)
        gates_ref[...] = jnp.concatenate([x_ref[0:skip, :], fill], axis=0).T.astype(gates_ref.dtype)


def cast_w_in(w_in_t, layer, gate_lo, n_gate):
    _, c, r = w_in_t.shape
    tc = SEC
    tr = _row_tile(r, 512)
    n_out = c - n_gate
    assert gate_lo % tc == 0 and n_out % tc == 0 and n_gate == SUBLANES
    return pl.pallas_call(
        functools.partial(_cast_w_in_body, aligned_tiles=gate_lo // tc, skip=n_gate),
        out_shape=(jax.ShapeDtypeStruct((r, n_out), BF16), jax.ShapeDtypeStruct((r, LANES), BF16)),
        grid=(r // tr, n_out // tc),
        in_specs=[pl.BlockSpec((None, tc, tr), lambda i, j: (layer, j, i)),
                  pl.BlockSpec((None, n_gate, tr), lambda i, j: (layer, (j + 1) * (tc // n_gate), i))],
        out_specs=(pl.BlockSpec((tr, tc), lambda i, j: (i, j)),
                   pl.BlockSpec((tr, LANES), lambda i, j: (i, 0))),
        compiler_params=_params(("parallel", "arbitrary"), 32),
        name="cast_w_in",
    )(w_in_t, w_in_t)


def _head_split_body(x_ref, *refs, nh, dh):
    o_ref = refs[-1]
    rows = x_ref.shape[0]
    for h in range(nh):
        o_ref[pl.ds(h, rows, stride=nh), :] = x_ref[:, h * dh:(h + 1) * dh]


def head_split(proj, section, nh, stacked, layer, depth):
    rows = proj.shape[0]
    dh = SEC // nh
    tr = _row_tile(rows, 512, SUBLANES)
    nt = rows // tr
    args = (proj,) if stacked is None else (proj, stacked)
    in_specs = [pl.BlockSpec((tr, SEC), lambda i: (i, section))]
    if stacked is not None:
        in_specs.append(pl.BlockSpec(memory_space=pl.ANY))
    return pl.pallas_call(
        functools.partial(_head_split_body, nh=nh, dh=dh),
        out_shape=jax.ShapeDtypeStruct((depth * rows * nh, dh), F32),
        grid=(nt,),
        in_specs=in_specs,
        out_specs=pl.BlockSpec((tr * nh, dh), lambda i: (layer * nt + i, 0)),
        input_output_aliases={} if stacked is None else {1: 0},
        compiler_params=_params(("parallel",), 32),
        name="head_split",
    )(*args)


def _mlstm_body(q_ref, k_ref, v_ref, o_ref, gates_ref, bias_ref, c0_ref, n0_ref, m0_ref,
                y_ref, c_ref, n_ref, m_ref, *, cs, dh):
    step = pl.program_id(1)

    @pl.when(step == 0)
    def _():
        c_ref[...] = c0_ref[...]
        n_ref[...] = n0_ref[...]
        m_ref[...] = m0_ref[...]

    g = gates_ref[...] + bias_ref[...]
    col = _iota2(g.shape, 1)
    x = jnp.where(col >= H_A, _log_sigmoid_pair(g)[0], g)
    eye = jnp.where(_iota2((LANES, LANES), 0) == _iota2((LANES, LANES), 1), 1.0, 0.0).astype(BF16)
    rr = _iota2((cs, cs), 0)
    cc = _iota2((cs, cs), 1)
    lower = jnp.where(rr >= cc, 1.0, 0.0).astype(BF16)
    upper = jnp.where(rr <= cc, 1.0, 0.0).astype(BF16)
    xs = _split_bf16(x, 3)
    x_rows = sum(_dot_nt(eye, p) for p in xs)
    fc_cols = sum(_dot(lower, p) for p in xs)
    fc_rows = sum(_dot(p, upper) for p in _split_bf16(x_rows, 3))
    causal = rr >= cc
    scale = dh ** -0.5

    for h in range(H_A):
        sl = slice(h * dh, (h + 1) * dh)
        q = q_ref[:, sl]
        k = k_ref[:, sl] * scale
        v = v_ref[:, sl]
        qb = q.astype(BF16)
        kb = k.astype(BF16)
        i_col = x[:, h:h + 1]
        i_row = x_rows[h:h + 1, :]
        f_col = fc_cols[:, H_A + h:H_A + h + 1]
        f_row = fc_rows[H_A + h:H_A + h + 1, :]
        m_prev = m_ref[0, :, h:h + 1]
        c_prev = c_ref[0, h]
        n_prev = n_ref[0, h:h + 1, :]

        dmat = jnp.where(causal, f_col - f_row + i_row, -jnp.inf)
        carry_log = f_col + m_prev
        m_t = jnp.maximum(jnp.max(dmat, axis=-1, keepdims=True), carry_log)
        s = _dot_nt(qb, kb) * jnp.exp(dmat - m_t)
        w_prev = jnp.exp(carry_log - m_t)
        num = _dot(s.astype(BF16), v.astype(BF16)) + w_prev * _dot_nt(qb, c_prev.astype(BF16))
        den = jnp.sum(s, axis=-1, keepdims=True) + w_prev * jnp.sum(q * n_prev, axis=-1, keepdims=True)
        hid = num / jnp.maximum(jnp.abs(den), jnp.exp(-m_t))
        y_ref[:, sl] = jax.nn.sigmoid(o_ref[:, sl]) * hid

        m_new = m_t[cs - 1:cs, :]
        f_last = f_col[cs - 1:cs, :]
        decay = jnp.exp(f_last + m_prev - m_new)
        w_end = jnp.exp(f_last - f_col + i_col - m_new)
        c_ref[0, h] = decay * c_prev + _dot_tn((w_end * v).astype(BF16), kb)
        n_ref[0, h:h + 1, :] = decay * n_prev + jnp.sum(w_end * k, axis=0, keepdims=True)
        m_ref[0, :, h:h + 1] = m_new


def mlstm(proj, gates, bias, c0, n0, m0, bsz, length):
    dh = SEC // H_A
    cs = _row_tile(length, MLSTM_CHUNK_MAX, SUBLANES)
    nc = length // cs
    rows = bsz * length

    def sec(s):
        return pl.BlockSpec((cs, SEC), lambda b, c, s=s: (b * nc + c, s))

    state_c = pl.BlockSpec((1, H_A, dh, dh), lambda b, c: (b, 0, 0, 0))
    state_n = pl.BlockSpec((1, H_A, dh), lambda b, c: (b, 0, 0))
    state_m = pl.BlockSpec((1, 1, H_A), lambda b, c: (b, 0, 0))
    y, c, n, m = pl.pallas_call(
        functools.partial(_mlstm_body, cs=cs, dh=dh),
        out_shape=(jax.ShapeDtypeStruct((rows, SEC), F32),
                   jax.ShapeDtypeStruct((bsz, H_A, dh, dh), F32),
                   jax.ShapeDtypeStruct((bsz, H_A, dh), F32),
                   jax.ShapeDtypeStruct((bsz, 1, H_A), F32)),
        grid=(bsz, nc),
        in_specs=[sec(QA), sec(KA), sec(VA), sec(OA),
                  pl.BlockSpec((cs, LANES), lambda b, c: (b * nc + c, 0)),
                  pl.BlockSpec((1, LANES), lambda b, c: (0, 0)),
                  state_c, state_n, state_m],
        out_specs=(pl.BlockSpec((cs, SEC), lambda b, c: (b * nc + c, 0)), state_c, state_n, state_m),
        compiler_params=_params(("parallel", "arbitrary"), 48),
        name="mlstm",
    )(proj, proj, proj, proj, gates, bias, c0, n0, m0.reshape(bsz, 1, H_A))
    return y, c, n, m.reshape(bsz, H_A)


def _softplus(z):
    return jnp.maximum(z, 0.0) + jnp.log(1.0 + jnp.exp(-jnp.abs(z)))


def _suffix_sum(sp, strict_upper2):
    hi, lo = _split_bf16(sp, 2)
    return _dot(jnp.concatenate([hi, lo], axis=1), strict_upper2)


def _strict_upper2(n):
    j = _iota2((2 * n, n), 0)
    j = jnp.where(j >= n, j - n, j)
    return jnp.where(j > _iota2((2 * n, n), 1), 1.0, 0.0).astype(BF16)


def _sb_prompt_body(bias_ref, q_ref, k_ref, v_ref, y_ref, *, t, dh, hpb):
    g = pl.program_id(1)
    i = pl.program_id(2)
    su2 = _strict_upper2(t)
    diag_mask = _iota2((t, t), 1) < _iota2((t, t), 0)
    heads = [slice(hh * dh, (hh + 1) * dh) for hh in range(hpb)]
    bias = [bias_ref[g * hpb + hh] for hh in range(hpb)]
    q = [(q_ref[:, sl] * (dh ** -0.5)).astype(BF16) for sl in heads]

    def tile(j, state, mask):
        rows = pl.ds(pl.multiple_of(j * t, t), t)
        z = [_dot_nt(q[hh], k_ref[rows, sl].astype(BF16)) + bias[hh] for hh, sl in enumerate(heads)]
        sp = [_softplus(x) if mask is None else jnp.where(mask, _softplus(x), 0.0) for x in z]
        later = [_suffix_sum(x, su2) + state[hh][0] for hh, x in enumerate(sp)]
        a = [jnp.exp((x - p) - w) for x, p, w in zip(z, sp, later)]
        if mask is not None:
            a = [jnp.where(mask, x, 0.0) for x in a]
        acc = [state[hh][1] + _dot(a[hh].astype(BF16), v_ref[rows, sl].astype(BF16))
               for hh, sl in enumerate(heads)]
        carry = [state[hh][0] + jnp.sum(sp[hh], axis=-1, keepdims=True) for hh in range(hpb)]
        return tuple(zip(carry, acc))

    zero = tuple((jnp.zeros((t, 1), F32), jnp.zeros((t, dh), F32)) for _ in heads)
    state = tile(i, zero, diag_mask)
    state = lax.fori_loop(0, i, lambda n, st: tile(i - 1 - n, st, None), state)
    for hh, sl in enumerate(heads):
        y_ref[:, sl] = state[hh][1]


def sb_prompt(proj, sb_bias, bsz, length):
    dh = SEC // H_B
    hpb = SB_HEADS_PER_STEP
    wide = hpb * dh
    t = _row_tile(length, SB_TILE, SUBLANES)
    nq = length // t
    cpb = SEC // wide
    return pl.pallas_call(
        functools.partial(_sb_prompt_body, t=t, dh=dh, hpb=hpb),
        out_shape=jax.ShapeDtypeStruct((bsz * length, SEC), F32),
        grid=(bsz, H_B // hpb, nq),
        in_specs=[pl.BlockSpec(memory_space=pltpu.SMEM),
                  pl.BlockSpec((t, wide), lambda b, g, i: (b * nq + i, QB * cpb + g)),
                  pl.BlockSpec((length, wide), lambda b, g, i: (b, KB * cpb + g)),
                  pl.BlockSpec((length, wide), lambda b, g, i: (b, VB * cpb + g))],
        out_specs=pl.BlockSpec((t, wide), lambda b, g, i: (b * nq + i, g)),
        compiler_params=_params(("parallel", "parallel", "arbitrary"), 40),
        name="sb_prompt",
    )(sb_bias, proj, proj, proj)


def _sb_sample_body(pt_ref, q_ref, kn_ref, vn_ref, bias_ref, *refs, lq, dh, page, ppb):
    del pt_ref
    page_refs = refs[:2 * ppb]
    y_ref, q2_ref, acc_ref, carry_ref = refs[2 * ppb:]
    j = pl.program_id(1)
    rows = H_B * lq
    bias = bias_ref[...]

    @pl.when(j == 0)
    def _():
        q = q_ref[...] * (dh ** -0.5)
        q2 = jnp.concatenate([q[:, h * dh:(h + 1) * dh] for h in range(H_B)], axis=0).astype(BF16)
        q2_ref[...] = q2
        q_wide = jnp.concatenate([q2] * H_B, axis=1)
        same_head = (_iota2((rows, SEC), 0) // lq) == (_iota2((rows, SEC), 1) // dh)
        q_bd = jnp.where(same_head, q_wide, jnp.zeros_like(q_wide))
        fill = jnp.zeros((LANES - lq, SEC), F32)
        kn = jnp.concatenate([kn_ref[...], fill], axis=0).astype(BF16)
        vn = jnp.concatenate([vn_ref[...], fill], axis=0).astype(BF16)
        z = _dot_nt(q_bd, kn) + bias
        mask = _iota2((rows, LANES), 1) < (_iota2((rows, LANES), 0) % lq)
        sp = jnp.where(mask, _softplus(z), 0.0)
        a = jnp.where(mask, jnp.exp((z - sp) - _suffix_sum(sp, _strict_upper2(LANES))), 0.0)
        full = _dot(a.astype(BF16), vn)
        acc_ref[...] = jnp.concatenate(
            [full[h * lq:(h + 1) * lq, h * dh:(h + 1) * dh] for h in range(H_B)], axis=0)
        carry_ref[...] = jnp.sum(sp, axis=-1, keepdims=True)

    q2 = q2_ref[...]
    su2 = _strict_upper2(page)
    def head_rows(ref, h):
        return ref[pl.ds(h, page, stride=H_B), :].astype(BF16)

    def own_rows(per_head):
        return jnp.concatenate([x[h * lq:(h + 1) * lq, :] for h, x in enumerate(per_head)], axis=0)

    z = [own_rows([_dot_nt(q2, head_rows(kp, h)) for h in range(H_B)]) + bias
         for kp in page_refs[:ppb]]
    sp = [_softplus(x) for x in z]
    later = [_suffix_sum(x, su2) for x in sp]
    carry = carry_ref[...]
    acc = acc_ref[...]
    for x, p, w, vp in zip(z, sp, later, page_refs[ppb:]):
        a = jnp.exp((x - p) - (w + carry)).astype(BF16)
        acc = acc + own_rows([_dot(a, head_rows(vp, h)) for h in range(H_B)])
        carry = carry + jnp.sum(p, axis=-1, keepdims=True)
    acc_ref[...] = acc
    carry_ref[...] = carry

    @pl.when(j == pl.num_programs(1) - 1)
    def _():
        acc = acc_ref[...]
        y_ref[...] = jnp.concatenate([acc[h * lq:(h + 1) * lq, :] for h in range(H_B)], axis=1)


def sb_sample(proj, sb_bias, cache_k, cache_v, page_table, layer, bsz, length):
    dh = SEC // H_B
    n_pages = page_table.shape[1]
    page = cache_k.shape[2]
    rows = H_B * length
    bias_rows = jnp.repeat(sb_bias.astype(F32), length).reshape(rows, 1)

    def new_sec(s):
        return pl.BlockSpec((length, SEC), lambda b, j, pt, s=s: (b, s))

    ppb = SB_PAGES_PER_STEP
    while n_pages % ppb:
        ppb -= 1

    def past(r):
        return pl.BlockSpec((None, None, page * H_B, dh),
                            lambda b, j, pt, r=r: (layer, pt[b, n_pages - 1 - (j * ppb + r)], 0, 0))

    page_specs = [past(r) for r in range(ppb)]
    grid_spec = pltpu.PrefetchScalarGridSpec(
        num_scalar_prefetch=1,
        grid=(bsz, n_pages // ppb),
        in_specs=[new_sec(QB), new_sec(KB), new_sec(VB),
                  pl.BlockSpec((rows, 1), lambda b, j, pt: (0, 0))] + page_specs + page_specs,
        out_specs=pl.BlockSpec((length, SEC), lambda b, j, pt: (b, 0)),
        scratch_shapes=[pltpu.VMEM((rows, dh), BF16), pltpu.VMEM((rows, dh), F32),
                        pltpu.VMEM((rows, 1), F32)],
    )
    k2d = cache_k.reshape(cache_k.shape[0], cache_k.shape[1], page * H_B, dh)
    v2d = cache_v.reshape(cache_v.shape[0], cache_v.shape[1], page * H_B, dh)
    return pl.pallas_call(
        functools.partial(_sb_sample_body, lq=length, dh=dh, page=page, ppb=ppb),
        out_shape=jax.ShapeDtypeStruct((bsz * length, SEC), F32),
        grid_spec=grid_spec,
        compiler_params=_params(("parallel", "arbitrary"), 40),
        name="sb_sample",
    )(page_table, proj, proj, proj, bias_rows, *([k2d] * ppb), *([v2d] * ppb))


def _conv_body(a_ref, g_ref, buf_ref, w_ref, b_ref, lg_ref, lb_ref, y_ref, new_ref, ext_ref, *, tl):
    hist = CONV_W - 1
    pad = 32 - hist
    step = pl.program_id(1)

    @pl.when(step == 0)
    def _():
        ext_ref[0:SUBLANES, :] = jnp.zeros((SUBLANES, SEC), F32)
        ext_ref[pad:32, :] = buf_ref[0]

    ext_ref[32:32 + tl, :] = a_ref[...] * jax.nn.sigmoid(g_ref[...])
    cols = []
    for c in range(SEC // LANES):
        lanes = slice(c * LANES, (c + 1) * LANES)
        slab = ext_ref[:, lanes]
        nrow = slab.shape[0]
        acc = jnp.zeros((tl, LANES), F32)
        for phase in range(SUBLANES):
            taps = [j for j in range(CONV_W) if (pad + j) % SUBLANES == phase]
            shifted = slab if phase == 0 else pltpu.roll(slab, nrow - phase, axis=0)
            for j in taps:
                start = pad + j - phase
                acc = acc + w_ref[j:j + 1, lanes] * shifted[start:start + tl, :]
        cols.append(acc)
    y = jnp.concatenate(cols, axis=1) + b_ref[...]
    yc = y - jnp.mean(y, axis=-1, keepdims=True)
    yn = yc * lax.rsqrt(jnp.mean(yc * yc, axis=-1, keepdims=True) + EPS)
    yn = yn * lg_ref[...] + lb_ref[...]
    y_ref[...] = yn * jax.nn.sigmoid(yn)
    new_ref[0] = ext_ref[tl + pad:tl + 32, :]
    if tl >= 32:
        ext_ref[0:32, :] = ext_ref[tl:tl + 32, :]


def conformer_conv(proj, buf, w, b, ln_g, ln_b, bsz, length):
    tl = _row_tile(length, CONV_TILE, SUBLANES)
    nt = length // tl
    assert nt == 1 or tl >= 32
    hist = CONV_W - 1
    vec = pl.BlockSpec((1, SEC), lambda bb, t: (0, 0))
    return pl.pallas_call(
        functools.partial(_conv_body, tl=tl),
        out_shape=(jax.ShapeDtypeStruct((bsz * length, SEC), F32),
                   jax.ShapeDtypeStruct((bsz, hist, SEC), F32)),
        grid=(bsz, nt),
        in_specs=[pl.BlockSpec((tl, SEC), lambda bb, t: (bb * nt + t, CA)),
                  pl.BlockSpec((tl, SEC), lambda bb, t: (bb * nt + t, CG)),
                  pl.BlockSpec((1, hist, SEC), lambda bb, t: (bb, 0, 0)),
                  pl.BlockSpec((CONV_W, SEC), lambda bb, t: (0, 0)),
                  vec, vec, vec],
        out_specs=(pl.BlockSpec((tl, SEC), lambda bb, t: (bb * nt + t, 0)),
                   pl.BlockSpec((1, hist, SEC), lambda bb, t: (bb, 0, 0))),
        scratch_shapes=[pltpu.VMEM((32 + tl, SEC), F32)],
        compiler_params=_params(("parallel", "arbitrary"), 32),
        name="conformer_conv",
    )(proj, proj, buf, w, b.reshape(1, SEC), ln_g.reshape(1, SEC), ln_b.reshape(1, SEC))


def _s5_drive(u_ref, br_ref, bi_ref, sr_ref, si_ref, perm):
    nblk, cin, nst = br_ref.shape
    ub = u_ref[...].astype(BF16)
    if perm is not None:
        ub = _dot(perm, ub).astype(BF16)
    for blk in range(nblk):
        ublk = ub[:, blk * cin:(blk + 1) * cin]
        sr_ref[:, blk * nst:(blk + 1) * nst] = _dot(ublk, br_ref[blk])
        si_ref[:, blk * nst:(blk + 1) * nst] = _dot(ublk, bi_ref[blk])


def _s5_readout(sr_ref, si_ref, cr_ref, ci_ref):
    nblk, nst, _ = cr_ref.shape
    ys = []
    for blk in range(nblk):
        lanes = slice(blk * nst, (blk + 1) * nst)
        ys.append(_dot(sr_ref[:, lanes].astype(BF16), cr_ref[blk])
                  - _dot(si_ref[:, lanes].astype(BF16), ci_ref[blk]))
    return jnp.concatenate(ys, axis=1)


def _s5_gate(y, gw_ref, gb_ref, y_ref):
    g = jax.nn.gelu(y)
    gate = _dot(g.astype(BF16), gw_ref[...]) + gb_ref[...]
    y_ref[...] = g * jax.nn.sigmoid(gate)


def _s5_segmented_body(u_ref, h0r_ref, h0i_ref, ar_ref, ai_ref, br_ref, bi_ref, cr_ref, ci_ref, d_ref,
                       gw_ref, gb_ref, y_ref, hr_ref, hi_ref, sr_ref, si_ref, pr_ref, pi_ref,
                       yp_ref, yt_ref, *, tl):
    step = pl.program_id(1)
    nseg = SUBLANES
    slen = tl // nseg
    nblk = br_ref.shape[0]
    nst = br_ref.shape[2]

    @pl.when(step == 0)
    def _():
        hr_ref[...] = h0r_ref[...]
        hi_ref[...] = h0i_ref[...]
        for blk in range(nblk):
            lanes = slice(blk * nst, (blk + 1) * nst)
            ar = ar_ref[:, lanes]
            ai = ai_ref[:, lanes]
            qr, qi = ar, ai
            for t in range(slen):
                pr_ref[t:t + 1, lanes] = qr
                pi_ref[t:t + 1, lanes] = qi
                qr, qi = qr * ar - qi * ai, qr * ai + qi * ar

    dst = _iota2((tl, tl), 0)
    src = (dst % nseg) * slen + dst // nseg
    perm = jnp.where(_iota2((tl, tl), 1) == src, 1.0, 0.0).astype(BF16)
    _s5_drive(u_ref, br_ref, bi_ref, sr_ref, si_ref, perm)

    first = _iota2((nseg, nst), 0) == 0
    for blk in range(nblk):
        lanes = slice(blk * nst, (blk + 1) * nst)
        ar = ar_ref[:, lanes]
        ai = ai_ref[:, lanes]

        def local_step(t, state, lanes=lanes, ar=ar, ai=ai):
            hr, hi = state
            rows = pl.ds(pl.multiple_of(t * nseg, nseg), nseg)
            nr = ar * hr - ai * hi + sr_ref[rows, lanes]
            ni = ar * hi + ai * hr + si_ref[rows, lanes]
            sr_ref[rows, lanes] = nr
            si_ref[rows, lanes] = ni
            return nr, ni

        start = (jnp.where(first, hr_ref[0, :, lanes], 0.0), jnp.where(first, hi_ref[0, :, lanes], 0.0))
        er, ei = lax.fori_loop(0, slen, local_step, start, unroll=4)

        wr = pr_ref[slen - 1:slen, lanes]
        wi = pi_ref[slen - 1:slen, lanes]
        tr, ti = er[0:1, :], ei[0:1, :]
        starts_r = [jnp.zeros_like(tr)]
        starts_i = [jnp.zeros_like(ti)]
        for s in range(1, nseg):
            starts_r.append(tr)
            starts_i.append(ti)
            tr, ti = er[s:s + 1, :] + wr * tr - wi * ti, ei[s:s + 1, :] + wr * ti + wi * tr
        hr_ref[0, :, lanes] = tr
        hi_ref[0, :, lanes] = ti
        gr = jnp.concatenate(starts_r, axis=0)
        gi = jnp.concatenate(starts_i, axis=0)

        def fix_step(t, carry, lanes=lanes, gr=gr, gi=gi):
            rows = pl.ds(pl.multiple_of(t * nseg, nseg), nseg)
            qr = pr_ref[pl.ds(t, 1), lanes]
            qi = pi_ref[pl.ds(t, 1), lanes]
            sr_ref[rows, lanes] += qr * gr - qi * gi
            si_ref[rows, lanes] += qr * gi + qi * gr
            return carry

        lax.fori_loop(0, slen, fix_step, 0, unroll=4)

    yp = _s5_readout(sr_ref, si_ref, cr_ref, ci_ref)
    for c in range(SEC // LANES):
        yp_ref[c] = yp[:, c * LANES:(c + 1) * LANES]
    for c in range(SEC // LANES):
        for s in range(nseg):
            yt_ref[s * slen:(s + 1) * slen, c * LANES:(c + 1) * LANES] = (
                yp_ref[c, pl.ds(s, slen, stride=nseg), :])
    _s5_gate(yt_ref[...] + d_ref[...] * u_ref[...], gw_ref, gb_ref, y_ref)


def _s5_body(u_ref, h0r_ref, h0i_ref, ar_ref, ai_ref, br_ref, bi_ref, cr_ref, ci_ref, d_ref,
             gw_ref, gb_ref, y_ref, hr_ref, hi_ref, sr_ref, si_ref, *, tl):
    step = pl.program_id(1)
    nblk = br_ref.shape[0]
    nst = br_ref.shape[2]

    @pl.when(step == 0)
    def _():
        hr_ref[...] = h0r_ref[...]
        hi_ref[...] = h0i_ref[...]

    _s5_drive(u_ref, br_ref, bi_ref, sr_ref, si_ref, None)

    for blk in range(nblk):
        lanes = slice(blk * nst, (blk + 1) * nst)
        ar = ar_ref[:, lanes]
        ai = ai_ref[:, lanes]

        def scan_step(t, state, lanes=lanes, ar=ar, ai=ai):
            hr, hi = state
            row = pl.ds(t, 1)
            nr = ar * hr - ai * hi + sr_ref[row, lanes]
            ni = ar * hi + ai * hr + si_ref[row, lanes]
            sr_ref[row, lanes] = nr
            si_ref[row, lanes] = ni
            return nr, ni

        hr, hi = lax.fori_loop(0, tl, scan_step, (hr_ref[0, :, lanes], hi_ref[0, :, lanes]),
                               unroll=SUBLANES)
        hr_ref[0, :, lanes] = hr
        hi_ref[0, :, lanes] = hi

    y = _s5_readout(sr_ref, si_ref, cr_ref, ci_ref) + d_ref[...] * u_ref[...]
    _s5_gate(y, gw_ref, gb_ref, y_ref)


def s5_discretize(a_re, a_im, log_dt, b_re, b_im, c_re, c_im):
    ngrp, nst = a_re.shape
    dt = jnp.exp(log_dt.astype(F32))[:, None]
    ar = a_re.astype(F32)
    ai = a_im.astype(F32)
    mag = jnp.exp(ar * dt)
    abar_re = mag * jnp.cos(ai * dt)
    abar_im = mag * jnp.sin(ai * dt)
    zr = abar_re - 1.0
    zi = abar_im
    den = ar * ar + ai * ai
    coef_re = (zr * ar + zi * ai) / den
    coef_im = (zi * ar - zr * ai) / den
    br = b_re.astype(F32)
    bi = b_im.astype(F32)
    bbar_re = coef_re[..., None] * br - coef_im[..., None] * bi
    bbar_im = coef_re[..., None] * bi + coef_im[..., None] * br
    gb = S5_GROUP_BLOCK
    nblk = ngrp // gb
    eye = jnp.eye(gb, dtype=F32)

    def in_blocks(bbar):
        t = bbar.reshape(nblk, gb, nst, S5_CH)
        return jnp.einsum('bgnc,gh->bgchn', t, eye).reshape(nblk, gb * S5_CH, gb * nst).astype(BF16)

    def out_blocks(c):
        t = c.astype(F32).reshape(nblk, gb, S5_CH, nst)
        return jnp.einsum('bgcn,gh->bgnhc', t, eye).reshape(nblk, gb * nst, gb * S5_CH).astype(BF16)

    return (abar_re.reshape(1, ngrp * nst), abar_im.reshape(1, ngrp * nst),
            in_blocks(bbar_re), in_blocks(bbar_im), out_blocks(c_re), out_blocks(c_im))


def s5(proj, h0_re, h0_im, disc, d, gate_w, gate_b, bsz, length):
    abar_re, abar_im, b_re, b_im, c_re, c_im = disc
    nstate = abar_re.shape[1]
    tl = _row_tile(length, S5_TILE, SUBLANES)
    nt = length // tl
    state = pl.BlockSpec((1, 1, nstate), lambda b, t: (b, 0, 0))

    def whole(a):
        return pl.BlockSpec(a.shape, lambda b, t, nd=a.ndim: (0,) * nd)

    d2 = d.reshape(1, SEC)
    gb2 = gate_b.reshape(1, SEC)
    scratch = [pltpu.VMEM((tl, nstate), F32), pltpu.VMEM((tl, nstate), F32)]
    body = _s5_body
    if tl % (SUBLANES * SUBLANES) == 0:
        body = _s5_segmented_body
        slen = tl // SUBLANES
        scratch += [pltpu.VMEM((slen, nstate), F32), pltpu.VMEM((slen, nstate), F32),
                    pltpu.VMEM((SEC // LANES, tl, LANES), F32), pltpu.VMEM((tl, SEC), F32)]
    y, hr, hi = pl.pallas_call(
        functools.partial(body, tl=tl),
        out_shape=(jax.ShapeDtypeStruct((bsz * length, SEC), F32),
                   jax.ShapeDtypeStruct((bsz, 1, nstate), F32),
                   jax.ShapeDtypeStruct((bsz, 1, nstate), F32)),
        grid=(bsz, nt),
        in_specs=[pl.BlockSpec((tl, SEC), lambda b, t: (b * nt + t, UD)), state, state,
                  whole(abar_re), whole(abar_im), whole(b_re), whole(b_im), whole(c_re), whole(c_im),
                  whole(d2), whole(gate_w), whole(gb2)],
        out_specs=(pl.BlockSpec((tl, SEC), lambda b, t: (b * nt + t, 0)), state, state),
        scratch_shapes=scratch,
        compiler_params=_params(("parallel", "arbitrary"), 48),
        name="s5",
    )(proj, h0_re.reshape(bsz, 1, nstate), h0_im.reshape(bsz, 1, nstate),
      abar_re, abar_im, b_re, b_im, c_re, c_im, d2, gate_w, gb2)
    return y, hr, hi


def _cross_body(q_ref, k_ref, v_ref, o_ref, *, dh):
    scale = dh ** -0.5
    for h in range(H_X):
        sl = slice(h * dh, (h + 1) * dh)
        s = _dot_nt(q_ref[:, sl].astype(BF16), k_ref[:, sl].astype(BF16)) * scale
        e = jnp.exp(s - jnp.max(s, axis=-1, keepdims=True))
        p = e / jnp.sum(e, axis=-1, keepdims=True)
        o_ref[:, sl] = _dot(p.astype(BF16), v_ref[:, sl].astype(BF16)).astype(o_ref.dtype)


def cross_attend(q, mem_k, mem_v, k_col, v_col, bsz, length, n_mem):
    wx = q.shape[1]
    tq = _row_tile(length, 512, SUBLANES)
    nq = length // tq
    return pl.pallas_call(
        functools.partial(_cross_body, dh=wx // H_X),
        out_shape=jax.ShapeDtypeStruct((bsz * length, wx), F32),
        grid=(bsz, nq),
        in_specs=[pl.BlockSpec((tq, wx), lambda b, i: (b * nq + i, 0)),
                  pl.BlockSpec((n_mem, wx), lambda b, i: (b, k_col)),
                  pl.BlockSpec((n_mem, wx), lambda b, i: (b, v_col))],
        out_specs=pl.BlockSpec((tq, wx), lambda b, i: (b * nq + i, 0)),
        compiler_params=_params(("parallel", "arbitrary"), 32),
        name="cross_attend",
    )(q, mem_k, mem_v)


def _decoder_layer(x, bsz, length, mem_k, mem_v, k_col, v_col, n_mem, sb_fn, mlstm_state, conv_buf,
                   s5_state, p, kv_stacked, layer, depth):
    h, gates = rmsnorm_gates(x, p['norm_mix_pre'], p['w_gates'])
    proj = matmul(h, p['w_in'], F32)
    y_a, c_new, n_new, m_new = mlstm(proj, gates, p['gate_bias'], *mlstm_state, bsz, length)
    y_b = sb_fn(proj)
    y_c, conv_new = conformer_conv(proj, conv_buf, p['conv_w'], p['conv_b'], p['conv_ln_g'],
                                   p['conv_ln_b'], bsz, length)
    y_d, s5_re, s5_im = s5(proj, s5_state[0], s5_state[1], p['s5_disc'], p['s5_d'], p['s5_gate_w'],
                           p['s5_gate_b'], bsz, length)
    x, hx = mix_out_proj(y_a, y_b, y_c, y_d, p['g_group'], p['w_out'], p['norm_mix_post'], x,
                         p['norm_x_pre'])
    q = matmul(hx, p['w_xq'], F32)
    o = cross_attend(q, mem_k, mem_v, k_col, v_col, bsz, length, n_mem)
    x, hf = matmul_norm_res(o, p['w_xo'], p['norm_x_post'], x, p['norm_ffn_pre'])
    x = ffn(hf, p['w_up'], p['w_down'], p['norm_ffn_post'], x)
    kv_stacked = (head_split(proj, KB, H_B, kv_stacked[0], layer, depth),
                  head_split(proj, VB, H_B, kv_stacked[1], layer, depth))
    return x, kv_stacked, (c_new, n_new, m_new), conv_new, (s5_re, s5_im)


def kernel(x_prompt, x_sample, cache_sb_k, cache_sb_v, page_table, state_mlstm_c, state_mlstm_n, state_mlstm_m, state_conv, state_s5_re, state_s5_im, cache_mem_k, cache_mem_v, mem_prompt, norm_mix_pre, w_in, b_mlstm_gates, sb_bias, conv_w, conv_b, conv_ln_g, conv_ln_b, s5_a_re, s5_a_im, s5_log_dt, s5_b_re, s5_b_im, s5_c_re, s5_c_im, s5_d, s5_gate_w, s5_gate_b, g_group, w_out, norm_mix_post, norm_mem, norm_x_pre, w_xq, w_xk, w_xv, w_xo, norm_x_post, norm_ffn_pre, w_up, w_down, norm_ffn_post):
    n_b, seq, d_model = x_prompt.shape
    n_db, dec_seq, _ = x_sample.shape
    depth = w_in.shape[0]
    n_mem = mem_prompt.shape[1]
    w_x = w_xq.shape[2]
    dh_a = SEC // H_A
    ngrp = s5_a_re.shape[1]
    n_gate = 2 * H_A
    gate_lo = 4 * SEC

    xp = x_prompt.reshape(n_b * seq, d_model)
    xs = x_sample.reshape(n_db * dec_seq, d_model)
    mem2d = mem_prompt.reshape(n_b * n_mem, d_model)
    w_in_t = jnp.transpose(w_in, (0, 2, 1))
    outs = [[] for _ in range(14)]
    kv_p = (None, None)
    kv_s = (None, None)
    for l in range(depth):
        w_in_main, w_in_gates = cast_w_in(w_in_t, l, gate_lo, n_gate)
        p = {
            'norm_mix_pre': norm_mix_pre[l],
            'w_in': w_in_main,
            'w_gates': w_in_gates,
            'gate_bias': jnp.pad(b_mlstm_gates[l].astype(F32), (0, LANES - n_gate)).reshape(1, LANES),
            'conv_w': conv_w[l], 'conv_b': conv_b[l], 'conv_ln_g': conv_ln_g[l], 'conv_ln_b': conv_ln_b[l],
            's5_disc': s5_discretize(s5_a_re[l], s5_a_im[l], s5_log_dt[l], s5_b_re[l], s5_b_im[l],
                                     s5_c_re[l], s5_c_im[l]),
            's5_d': s5_d[l], 's5_gate_w': s5_gate_w[l].astype(BF16), 's5_gate_b': s5_gate_b[l],
            'g_group': g_group[l], 'w_out': cast_bf16(w_out, l), 'norm_mix_post': norm_mix_post[l],
            'norm_x_pre': norm_x_pre[l], 'w_xq': w_xq[l].astype(BF16), 'w_xo': w_xo[l].astype(BF16),
            'norm_x_post': norm_x_post[l], 'norm_ffn_pre': norm_ffn_pre[l],
            'w_up': cast_bf16_col_tiles(w_up, l, FFN_TILE), 'w_down': cast_bf16(w_down, l),
            'norm_ffn_post': norm_ffn_post[l],
        }
        w_kv = jnp.concatenate([w_xk[l], w_xv[l]], axis=1).astype(BF16)
        mem_kv = matmul(rmsnorm_bf16(mem2d, norm_mem[l]), w_kv, F32, tm_cap=512)
        zero_mlstm = (jnp.zeros((n_b, H_A, dh_a, dh_a), F32), jnp.zeros((n_b, H_A, dh_a), F32),
                      jnp.zeros((n_b, H_A), F32))
        zero_s5 = (jnp.zeros((n_b, ngrp, S5_N), F32), jnp.zeros((n_b, ngrp, S5_N), F32))
        sb_p = functools.partial(sb_prompt, sb_bias=sb_bias[l], bsz=n_b, length=seq)
        xp, kv_p, mst_p, cnv_p, s5_p = _decoder_layer(
            xp, n_b, seq, mem_kv, mem_kv, 0, 1, n_mem, sb_p, zero_mlstm,
            jnp.zeros((n_b, CONV_W - 1, SEC), F32), zero_s5, p, kv_p, l, depth)
        sb_s = functools.partial(sb_sample, sb_bias=sb_bias[l], cache_k=cache_sb_k, cache_v=cache_sb_v,
                                 page_table=page_table, layer=l, bsz=n_db, length=dec_seq)
        xs, kv_s, mst_s, cnv_s, s5_s = _decoder_layer(
            xs, n_db, dec_seq, cache_mem_k[l].reshape(n_db * n_mem, w_x),
            cache_mem_v[l].reshape(n_db * n_mem, w_x), 0, 0, n_mem, sb_s,
            (state_mlstm_c[l], state_mlstm_n[l], state_mlstm_m[l]), state_conv[l],
            (state_s5_re[l], state_s5_im[l]), p, kv_s, l, depth)
        mk = mem_kv[:, :w_x].reshape(n_b, n_mem, H_X, w_x // H_X)
        mv = mem_kv[:, w_x:].reshape(n_b, n_mem, H_X, w_x // H_X)
        s5_shape = lambda a: a.reshape(a.shape[0], ngrp, S5_N)
        layer_out = (mst_p[0], mst_p[1], mst_p[2], mst_s[0], mst_s[1], mst_s[2],
                     cnv_p, cnv_s, s5_shape(s5_p[0]), s5_shape(s5_p[1]), s5_shape(s5_s[0]),
                     s5_shape(s5_s[1]), mk, mv)
        for acc, val in zip(outs, layer_out):
            acc.append(val)
    stacked = tuple(jnp.stack(vals, axis=0) for vals in outs)
    dh_b = SEC // H_B
    kv = tuple(a.reshape(depth, n, t, H_B, dh_b)
               for a, n, t in ((kv_p[0], n_b, seq), (kv_p[1], n_b, seq),
                               (kv_s[0], n_db, dec_seq), (kv_s[1], n_db, dec_seq)))
    return (xp.reshape(n_b, seq, d_model), xs.reshape(n_db, dec_seq, d_model)) + kv + stacked
```

```python
import functools

import jax
import jax.numpy as jnp
from jax import lax
from jax.experimental import pallas as pl
from jax.experimental.pallas import tpu as pltpu

F32 = jnp.float32
BF16 = jnp.bfloat16
EPS = 1e-6

H_A = 4
H_B = 8
CONV_W = 31
S5_CH = 16
S5_N = 64
H_X = 4
MLSTM_CHUNK_MAX = 256
SB_TILE = 256
SB_HEADS_PER_STEP = 4
SB_PAGES_PER_STEP = 8
CONV_TILE = 64
S5_TILE = 256
FFN_TILE = 1024
FFN_OUT_CHUNK = 1024
FFN_NORM_ROWS = 32
S5_GROUP_BLOCK = 16

LANES = 128
SUBLANES = 8
MIB = 1024 * 1024

SEC = 1024
QA, KA, VA, OA, QB, KB, VB, CA, CG, UD = range(10)


def _params(semantics, vmem_mib):
    return pltpu.CompilerParams(dimension_semantics=semantics, vmem_limit_bytes=vmem_mib * MIB)


def _row_tile(m, cap, mult=16):
    t = min(m, cap)
    while m % t or t % mult:
        t -= 1
    return t


def _log_sigmoid_pair(z):
    sp = jnp.log1p(jnp.exp(-jnp.abs(z)))
    return jnp.minimum(z, 0.0) - sp, -(jnp.maximum(z, 0.0) + sp)


def _split_bf16(x, terms):
    out = []
    r = x
    for _ in range(terms):
        p = r.astype(BF16)
        out.append(p)
        r = r - p.astype(F32)
    return out


def _dot(a, b):
    return jnp.dot(a, b, preferred_element_type=F32)


def _dot_nt(a, b):
    return lax.dot_general(a, b, (((1,), (1,)), ((), ())), preferred_element_type=F32)


def _dot_tn(a, b):
    return lax.dot_general(a, b, (((0,), (0,)), ((), ())), preferred_element_type=F32)


def _iota2(shape, dim):
    return lax.broadcasted_iota(jnp.int32, shape, dim)


def _rmsnorm_body(x_ref, g_ref, o_ref):
    x = x_ref[...]
    r = lax.rsqrt(jnp.mean(x * x, axis=-1, keepdims=True) + EPS)
    o_ref[...] = ((x * r) * g_ref[...]).astype(o_ref.dtype)


def rmsnorm_bf16(x, g):
    m, d = x.shape
    tr = _row_tile(m, 256)
    return pl.pallas_call(
        _rmsnorm_body,
        out_shape=jax.ShapeDtypeStruct((m, d), BF16),
        grid=(m // tr,),
        in_specs=[pl.BlockSpec((tr, d), lambda i: (i, 0)), pl.BlockSpec((1, d), lambda i: (0, 0))],
        out_specs=pl.BlockSpec((tr, d), lambda i: (i, 0)),
        compiler_params=_params(("parallel",), 32),
        name="rmsnorm",
    )(x, g.reshape(1, d))


def _rmsnorm_gates_body(x_ref, g_ref, wg_ref, o_ref, gates_ref):
    x = x_ref[...]
    r = lax.rsqrt(jnp.mean(x * x, axis=-1, keepdims=True) + EPS)
    h = ((x * r) * g_ref[...]).astype(BF16)
    o_ref[...] = h
    gates_ref[...] = _dot(h, wg_ref[...])


def rmsnorm_gates(x, g, wg):
    m, d = x.shape
    tr = _row_tile(m, 256)
    return pl.pallas_call(
        _rmsnorm_gates_body,
        out_shape=(jax.ShapeDtypeStruct((m, d), BF16), jax.ShapeDtypeStruct((m, LANES), F32)),
        grid=(m // tr,),
        in_specs=[pl.BlockSpec((tr, d), lambda i: (i, 0)), pl.BlockSpec((1, d), lambda i: (0, 0)),
                  pl.BlockSpec((d, LANES), lambda i: (0, 0))],
        out_specs=(pl.BlockSpec((tr, d), lambda i: (i, 0)), pl.BlockSpec((tr, LANES), lambda i: (i, 0))),
        compiler_params=_params(("parallel",), 32),
        name="rmsnorm_gates",
    )(x, g.reshape(1, d), wg)


def _matmul_body(a_ref, w_ref, o_ref):
    o_ref[...] = _dot(a_ref[...], w_ref[...]).astype(o_ref.dtype)


def matmul(a, w, out_dtype, tm_cap=1024, tn_cap=1024):
    m, k = a.shape
    n = w.shape[1]
    tm = _row_tile(m, tm_cap)
    tn = _row_tile(n, tn_cap, LANES)
    return pl.pallas_call(
        _matmul_body,
        out_shape=jax.ShapeDtypeStruct((m, n), out_dtype),
        grid=(m // tm, n // tn),
        in_specs=[pl.BlockSpec((tm, k), lambda i, j: (i, 0)), pl.BlockSpec((k, tn), lambda i, j: (0, j))],
        out_specs=pl.BlockSpec((tm, tn), lambda i, j: (i, j)),
        compiler_params=_params(("parallel", "arbitrary"), 56),
        name="matmul",
    )(a, w)


def _norm_residual(y, g_ref, res_ref, o_ref):
    r = lax.rsqrt(jnp.mean(y * y, axis=-1, keepdims=True) + EPS)
    o_ref[...] = res_ref[...] + (y * r) * g_ref[...]


def _norm_residual_next(y, g_ref, res_ref, g_next_ref, o_ref, h_ref):
    _norm_residual(y, g_ref, res_ref, o_ref)
    x = o_ref[...]
    r = lax.rsqrt(jnp.mean(x * x, axis=-1, keepdims=True) + EPS)
    h_ref[...] = ((x * r) * g_next_ref[...]).astype(h_ref.dtype)


def _matmul_norm_res_body(a_ref, w_ref, g_ref, res_ref, g_next_ref, o_ref, h_ref):
    _norm_residual_next(_dot(a_ref[...].astype(BF16), w_ref[...]), g_ref, res_ref, g_next_ref,
                        o_ref, h_ref)


def _mix_out_body(ya_ref, yb_ref, yc_ref, yd_ref, gg_ref, w_ref, g_ref, res_ref, g_next_ref,
                  o_ref, h_ref):
    wg = ya_ref.shape[1]
    parts = []
    for i, ref in enumerate((ya_ref, yb_ref, yc_ref, yd_ref)):
        y = ref[...]
        r = lax.rsqrt(jnp.mean(y * y, axis=-1, keepdims=True) + EPS)
        parts.append(((y * r) * gg_ref[:, i * wg:(i + 1) * wg]).astype(BF16))
    mixed = jnp.concatenate(parts, axis=1)
    _norm_residual_next(_dot(mixed, w_ref[...]), g_ref, res_ref, g_next_ref, o_ref, h_ref)


def mix_out_proj(ya, yb, yc, yd, g_group, w, g, res, g_next):
    m, wg = ya.shape
    n = w.shape[1]
    tm = _row_tile(m, 128)
    rows = pl.BlockSpec((tm, n), lambda i: (i, 0))
    part = pl.BlockSpec((tm, wg), lambda i: (i, 0))
    vec = pl.BlockSpec((1, n), lambda i: (0, 0))
    return pl.pallas_call(
        _mix_out_body,
        out_shape=(jax.ShapeDtypeStruct((m, n), F32), jax.ShapeDtypeStruct((m, n), BF16)),
        grid=(m // tm,),
        in_specs=[part, part, part, part, pl.BlockSpec((1, 4 * wg), lambda i: (0, 0)),
                  pl.BlockSpec(w.shape, lambda i: (0, 0), pipeline_mode=pl.Buffered(1)),
                  vec, rows, vec],
        out_specs=(rows, rows),
        compiler_params=_params(("parallel",), 60),
        name="mix_out_proj",
    )(ya, yb, yc, yd, g_group.reshape(1, 4 * wg), w, g.reshape(1, n), res, g_next.reshape(1, n))


def matmul_norm_res(a, w, g, res, g_next):
    m, kdim = a.shape
    n = w.shape[1]
    w_bytes = kdim * n * w.dtype.itemsize
    tm = _row_tile(m, 128 if w_bytes > 16 * MIB else 256)
    rows = pl.BlockSpec((tm, n), lambda i: (i, 0))
    vec = pl.BlockSpec((1, n), lambda i: (0, 0))
    return pl.pallas_call(
        _matmul_norm_res_body,
        out_shape=(jax.ShapeDtypeStruct((m, n), F32), jax.ShapeDtypeStruct((m, n), BF16)),
        grid=(m // tm,),
        in_specs=[pl.BlockSpec((tm, kdim), lambda i: (i, 0)),
                  pl.BlockSpec((kdim, n), lambda i: (0, 0), pipeline_mode=pl.Buffered(1)),
                  vec, rows, vec],
        out_specs=(rows, rows),
        compiler_params=_params(("parallel",), 60 if w_bytes > 16 * MIB else 40),
        name="matmul_norm_res",
    )(a, w, g.reshape(1, n), res, g_next.reshape(1, n))


def _ffn_body(h_ref, wu_ref, wd_ref, g_ref, res_ref, o_ref, *, nf):
    f = pl.program_id(1)

    @pl.when(f == 0)
    def _():
        o_ref[...] = jnp.zeros_like(o_ref)

    u = _dot(h_ref[...], wu_ref[...])
    u = jnp.square(jnp.maximum(u, 0.0)).astype(BF16)
    d = o_ref.shape[1]
    for c in range(0, d, FFN_OUT_CHUNK):
        cols = slice(c, c + FFN_OUT_CHUNK)
        o_ref[:, cols] += _dot(u, wd_ref[:, cols])

    @pl.when(f == nf - 1)
    def _():
        def rows_step(r, carry):
            rows = pl.ds(pl.multiple_of(r * FFN_NORM_ROWS, FFN_NORM_ROWS), FFN_NORM_ROWS)
            _norm_residual(o_ref[rows, :], g_ref, res_ref.at[rows, :], o_ref.at[rows, :])
            return carry

        lax.fori_loop(0, o_ref.shape[0] // FFN_NORM_ROWS, rows_step, 0)


def ffn(h, w_up_tiles, w_down, g, res, tm_cap=512):
    m, d = h.shape
    nf, _, tf = w_up_tiles.shape
    tm = _row_tile(m, tm_cap)
    once = pl.Buffered(1)
    return pl.pallas_call(
        functools.partial(_ffn_body, nf=nf),
        out_shape=jax.ShapeDtypeStruct((m, d), F32),
        grid=(m // tm, nf),
        in_specs=[pl.BlockSpec((tm, d), lambda i, f: (i, 0), pipeline_mode=once),
                  pl.BlockSpec((None, d, tf), lambda i, f: (f, 0, 0)),
                  pl.BlockSpec((tf, d), lambda i, f: (f, 0)),
                  pl.BlockSpec((1, d), lambda i, f: (0, 0)),
                  pl.BlockSpec((tm, d), lambda i, f: (i, 0), pipeline_mode=once)],
        out_specs=pl.BlockSpec((tm, d), lambda i, f: (i, 0), pipeline_mode=once),
        compiler_params=_params(("parallel", "arbitrary"), 60),
        name="ffn",
    )(h, w_up_tiles, w_down, g.reshape(1, d), res)


def _cast_body(x_ref, o_ref):
    o_ref[...] = x_ref[...].astype(o_ref.dtype)


def _cast_up_body(x_ref, hs_ref, o_ref, u_ref):
    i = pl.program_id(1)
    wb = x_ref[...].astype(BF16)
    o_ref[...] = wb
    part = _dot(hs_ref[...], wb)

    @pl.when(i == 0)
    def _():
        u_ref[...] = part

    @pl.when(i > 0)
    def _():
        u_ref[...] += part


def cast_up_with_rows(w_up, layer, tc, hs):
    _, r, c = w_up.shape
    ms = hs.shape[0]
    tr = _row_tile(r, 1024)
    return pl.pallas_call(
        _cast_up_body,
        out_shape=(jax.ShapeDtypeStruct((c // tc, r, tc), BF16), jax.ShapeDtypeStruct((ms, c), F32)),
        grid=(c // tc, r // tr),
        in_specs=[pl.BlockSpec((None, tr, tc), lambda j, i: (layer, i, j)),
                  pl.BlockSpec((ms, tr), lambda j, i: (0, i))],
        out_specs=(pl.BlockSpec((None, tr, tc), lambda j, i: (j, i, 0)),
                   pl.BlockSpec((ms, tc), lambda j, i: (0, j))),
        compiler_params=_params(("parallel", "arbitrary"), 40),
        name="cast_up_with_rows",
    )(w_up, hs)


def _cast_down_body(x_ref, u_ref, g_ref, res_ref, o_ref, y_ref, *, steps):
    i = pl.program_id(0)
    wb = x_ref[...].astype(BF16)
    o_ref[...] = wb
    u = jnp.square(jnp.maximum(u_ref[...], 0.0)).astype(BF16)
    part = _dot(u, wb)

    @pl.when(i == 0)
    def _():
        y_ref[...] = part

    @pl.when(i > 0)
    def _():
        y_ref[...] += part

    @pl.when(i == steps - 1)
    def _():
        _norm_residual(y_ref[...], g_ref, res_ref, y_ref)


def cast_down_with_rows(w_down, layer, u_pre, g, res):
    _, f, d = w_down.shape
    ms = u_pre.shape[0]
    tr = _row_tile(f, 512)
    steps = f // tr
    return pl.pallas_call(
        functools.partial(_cast_down_body, steps=steps),
        out_shape=(jax.ShapeDtypeStruct((f, d), BF16), jax.ShapeDtypeStruct((ms, d), F32)),
        grid=(steps,),
        in_specs=[pl.BlockSpec((None, tr, d), lambda i: (layer, i, 0)),
                  pl.BlockSpec((ms, tr), lambda i: (0, i)),
                  pl.BlockSpec((1, d), lambda i: (0, 0)),
                  pl.BlockSpec((ms, d), lambda i: (0, 0))],
        out_specs=(pl.BlockSpec((tr, d), lambda i: (i, 0)),
                   pl.BlockSpec((ms, d), lambda i: (0, 0))),
        compiler_params=_params(("arbitrary",), 48),
        name="cast_down_with_rows",
    )(w_down, u_pre, g.reshape(1, d), res)


def cast_bf16(x, layer):
    _, r, c = x.shape
    tr = _row_tile(r, 512)
    tc = _row_tile(c, 4096, LANES)
    return pl.pallas_call(
        _cast_body,
        out_shape=jax.ShapeDtypeStruct((r, c), BF16),
        grid=(r // tr, c // tc),
        in_specs=[pl.BlockSpec((None, tr, tc), lambda i, j: (layer, i, j))],
        out_specs=pl.BlockSpec((tr, tc), lambda i, j: (i, j)),
        compiler_params=_params(("parallel", "parallel"), 40),
        name="cast_bf16",
    )(x)


def _cast_w_in_body(x_ref, nxt_ref, o_ref, gates_ref, *, aligned_tiles, skip):
    j = pl.program_id(1)
    tc = x_ref.shape[0]

    @pl.when(j < aligned_tiles)
    def _():
        o_ref[...] = x_ref[...].T.astype(o_ref.dtype)

    @pl.when(j >= aligned_tiles)
    def _():
        wide = jnp.concatenate([x_ref[...], nxt_ref[...]], axis=0)
        o_ref[...] = wide[skip:skip + tc, :].T.astype(o_ref.dtype)

    @pl.when(j == aligned_tiles)
    def _():
        fill = jnp.zeros((LANES - skip, x_ref.shape[1]), F32)
        gates_ref[...] = jnp.concatenate([x_ref[0:skip, :], fill], axis=0).T.astype(gates_ref.dtype)


def cast_w_in(w_in_t, layer, gate_lo, n_gate):
    _, c, r = w_in_t.shape
    tc = SEC
    tr = _row_tile(r, 512)
    n_out = c - n_gate
    assert gate_lo % tc == 0 and n_out % tc == 0 and n_gate == SUBLANES
    return pl.pallas_call(
        functools.partial(_cast_w_in_body, aligned_tiles=gate_lo // tc, skip=n_gate),
        out_shape=(jax.ShapeDtypeStruct((r, n_out), BF16), jax.ShapeDtypeStruct((r, LANES), BF16)),
        grid=(r // tr, n_out // tc),
        in_specs=[pl.BlockSpec((None, tc, tr), lambda i, j: (layer, j, i)),
                  pl.BlockSpec((None, n_gate, tr), lambda i, j: (layer, (j + 1) * (tc // n_gate), i))],
        out_specs=(pl.BlockSpec((tr, tc), lambda i, j: (i, j)),
                   pl.BlockSpec((tr, LANES), lambda i, j: (i, 0))),
        compiler_params=_params(("parallel", "arbitrary"), 32),
        name="cast_w_in",
    )(w_in_t, w_in_t)


def _head_split_body(x_ref, *refs, nh, dh):
    o_ref = refs[-1]
    rows = x_ref.shape[0]
    for h in range(nh):
        o_ref[pl.ds(h, rows, stride=nh), :] = x_ref[:, h * dh:(h + 1) * dh]


def head_split(proj, section, nh, stacked, layer, depth):
    rows = proj.shape[0]
    dh = SEC // nh
    tr = _row_tile(rows, 512, SUBLANES)
    nt = rows // tr
    args = (proj,) if stacked is None else (proj, stacked)
    in_specs = [pl.BlockSpec((tr, SEC), lambda i: (i, section))]
    if stacked is not None:
        in_specs.append(pl.BlockSpec(memory_space=pl.ANY))
    return pl.pallas_call(
        functools.partial(_head_split_body, nh=nh, dh=dh),
        out_shape=jax.ShapeDtypeStruct((depth * rows * nh, dh), F32),
        grid=(nt,),
        in_specs=in_specs,
        out_specs=pl.BlockSpec((tr * nh, dh), lambda i: (layer * nt + i, 0)),
        input_output_aliases={} if stacked is None else {1: 0},
        compiler_params=_params(("parallel",), 32),
        name="head_split",
    )(*args)


def _mlstm_body(q_ref, k_ref, v_ref, o_ref, gates_ref, bias_ref, c0_ref, n0_ref, m0_ref,
                y_ref, c_ref, n_ref, m_ref, *, cs, dh):
    step = pl.program_id(1)

    @pl.when(step == 0)
    def _():
        c_ref[...] = c0_ref[...]
        n_ref[...] = n0_ref[...]
        m_ref[...] = m0_ref[...]

    g = gates_ref[...] + bias_ref[...]
    col = _iota2(g.shape, 1)
    x = jnp.where(col >= H_A, _log_sigmoid_pair(g)[0], g)
    eye = jnp.where(_iota2((LANES, LANES), 0) == _iota2((LANES, LANES), 1), 1.0, 0.0).astype(BF16)
    rr = _iota2((cs, cs), 0)
    cc = _iota2((cs, cs), 1)
    lower = jnp.where(rr >= cc, 1.0, 0.0).astype(BF16)
    upper = jnp.where(rr <= cc, 1.0, 0.0).astype(BF16)
    xs = _split_bf16(x, 3)
    x_rows = sum(_dot_nt(eye, p) for p in xs)
    fc_cols = sum(_dot(lower, p) for p in xs)
    fc_rows = sum(_dot(p, upper) for p in _split_bf16(x_rows, 3))
    causal = rr >= cc
    scale = dh ** -0.5

    for h in range(H_A):
        sl = slice(h * dh, (h + 1) * dh)
        q = q_ref[:, sl]
        k = k_ref[:, sl] * scale
        v = v_ref[:, sl]
        qb = q.astype(BF16)
        kb = k.astype(BF16)
        i_col = x[:, h:h + 1]
        i_row = x_rows[h:h + 1, :]
        f_col = fc_cols[:, H_A + h:H_A + h + 1]
        f_row = fc_rows[H_A + h:H_A + h + 1, :]
        m_prev = m_ref[0, :, h:h + 1]
        c_prev = c_ref[0, h]
        n_prev = n_ref[0, h:h + 1, :]

        dmat = jnp.where(causal, f_col - f_row + i_row, -jnp.inf)
        carry_log = f_col + m_prev
        m_t = jnp.maximum(jnp.max(dmat, axis=-1, keepdims=True), carry_log)
        s = _dot_nt(qb, kb) * jnp.exp(dmat - m_t)
        w_prev = jnp.exp(carry_log - m_t)
        num = _dot(s.astype(BF16), v.astype(BF16)) + w_prev * _dot_nt(qb, c_prev.astype(BF16))
        den = jnp.sum(s, axis=-1, keepdims=True) + w_prev * jnp.sum(q * n_prev, axis=-1, keepdims=True)
        hid = num / jnp.maximum(jnp.abs(den), jnp.exp(-m_t))
        y_ref[:, sl] = jax.nn.sigmoid(o_ref[:, sl]) * hid

        m_new = m_t[cs - 1:cs, :]
        f_last = f_col[cs - 1:cs, :]
        decay = jnp.exp(f_last + m_prev - m_new)
        w_end = jnp.exp(f_last - f_col + i_col - m_new)
        c_ref[0, h] = decay * c_prev + _dot_tn((w_end * v).astype(BF16), kb)
        n_ref[0, h:h + 1, :] = decay * n_prev + jnp.sum(w_end * k, axis=0, keepdims=True)
        m_ref[0, :, h:h + 1] = m_new


def mlstm(proj, gates, bias, c0, n0, m0, bsz, length):
    dh = SEC // H_A
    cs = _row_tile(length, MLSTM_CHUNK_MAX, SUBLANES)
    nc = length // cs
    rows = bsz * length

    def sec(s):
        return pl.BlockSpec((cs, SEC), lambda b, c, s=s: (b * nc + c, s))

    state_c = pl.BlockSpec((1, H_A, dh, dh), lambda b, c: (b, 0, 0, 0))
    state_n = pl.BlockSpec((1, H_A, dh), lambda b, c: (b, 0, 0))
    state_m = pl.BlockSpec((1, 1, H_A), lambda b, c: (b, 0, 0))
    y, c, n, m = pl.pallas_call(
        functools.partial(_mlstm_body, cs=cs, dh=dh),
        out_shape=(jax.ShapeDtypeStruct((rows, SEC), F32),
                   jax.ShapeDtypeStruct((bsz, H_A, dh, dh), F32),
                   jax.ShapeDtypeStruct((bsz, H_A, dh), F32),
                   jax.ShapeDtypeStruct((bsz, 1, H_A), F32)),
        grid=(bsz, nc),
        in_specs=[sec(QA), sec(KA), sec(VA), sec(OA),
                  pl.BlockSpec((cs, LANES), lambda b, c: (b * nc + c, 0)),
                  pl.BlockSpec((1, LANES), lambda b, c: (0, 0)),
                  state_c, state_n, state_m],
        out_specs=(pl.BlockSpec((cs, SEC), lambda b, c: (b * nc + c, 0)), state_c, state_n, state_m),
        compiler_params=_params(("parallel", "arbitrary"), 48),
        name="mlstm",
    )(proj, proj, proj, proj, gates, bias, c0, n0, m0.reshape(bsz, 1, H_A))
    return y, c, n, m.reshape(bsz, H_A)


def _softplus(z):
    return jnp.maximum(z, 0.0) + jnp.log(1.0 + jnp.exp(-jnp.abs(z)))


def _suffix_sum(sp, strict_upper2):
    hi, lo = _split_bf16(sp, 2)
    return _dot(jnp.concatenate([hi, lo], axis=1), strict_upper2)


def _strict_upper2(n):
    j = _iota2((2 * n, n), 0)
    j = jnp.where(j >= n, j - n, j)
    return jnp.where(j > _iota2((2 * n, n), 1), 1.0, 0.0).astype(BF16)


def _sb_prompt_body(bias_ref, q_ref, k_ref, v_ref, y_ref, *, t, dh, hpb):
    g = pl.program_id(1)
    i = pl.program_id(2)
    su2 = _strict_upper2(t)
    diag_mask = _iota2((t, t), 1) < _iota2((t, t), 0)
    heads = [slice(hh * dh, (hh + 1) * dh) for hh in range(hpb)]
    bias = [bias_ref[g * hpb + hh] for hh in range(hpb)]
    q = [(q_ref[:, sl] * (dh ** -0.5)).astype(BF16) for sl in heads]

    def tile(j, state, mask):
        rows = pl.ds(pl.multiple_of(j * t, t), t)
        z = [_dot_nt(q[hh], k_ref[rows, sl].astype(BF16)) + bias[hh] for hh, sl in enumerate(heads)]
        sp = [_softplus(x) if mask is None else jnp.where(mask, _softplus(x), 0.0) for x in z]
        later = [_suffix_sum(x, su2) + state[hh][0] for hh, x in enumerate(sp)]
        a = [jnp.exp((x - p) - w) for x, p, w in zip(z, sp, later)]
        if mask is not None:
            a = [jnp.where(mask, x, 0.0) for x in a]
        acc = [state[hh][1] + _dot(a[hh].astype(BF16), v_ref[rows, sl].astype(BF16))
               for hh, sl in enumerate(heads)]
        carry = [state[hh][0] + jnp.sum(sp[hh], axis=-1, keepdims=True) for hh in range(hpb)]
        return tuple(zip(carry, acc))

    zero = tuple((jnp.zeros((t, 1), F32), jnp.zeros((t, dh), F32)) for _ in heads)
    state = tile(i, zero, diag_mask)
    state = lax.fori_loop(0, i, lambda n, st: tile(i - 1 - n, st, None), state)
    for hh, sl in enumerate(heads):
        y_ref[:, sl] = state[hh][1]


def sb_prompt(proj, sb_bias, bsz, length):
    dh = SEC // H_B
    hpb = SB_HEADS_PER_STEP
    wide = hpb * dh
    t = _row_tile(length, SB_TILE, SUBLANES)
    nq = length // t
    cpb = SEC // wide
    return pl.pallas_call(
        functools.partial(_sb_prompt_body, t=t, dh=dh, hpb=hpb),
        out_shape=jax.ShapeDtypeStruct((bsz * length, SEC), F32),
        grid=(bsz, H_B // hpb, nq),
        in_specs=[pl.BlockSpec(memory_space=pltpu.SMEM),
                  pl.BlockSpec((t, wide), lambda b, g, i: (b * nq + i, QB * cpb + g)),
                  pl.BlockSpec((length, wide), lambda b, g, i: (b, KB * cpb + g)),
                  pl.BlockSpec((length, wide), lambda b, g, i: (b, VB * cpb + g))],
        out_specs=pl.BlockSpec((t, wide), lambda b, g, i: (b * nq + i, g)),
        compiler_params=_params(("parallel", "parallel", "arbitrary"), 40),
        name="sb_prompt",
    )(sb_bias, proj, proj, proj)


def _sb_sample_body(pt_ref, q_ref, kn_ref, vn_ref, bias_ref, *refs, lq, dh, page, ppb):
    del pt_ref
    page_refs = refs[:2 * ppb]
    y_ref, q2_ref, acc_ref, carry_ref = refs[2 * ppb:]
    j = pl.program_id(1)
    rows = H_B * lq
    bias = bias_ref[...]

    @pl.when(j == 0)
    def _():
        q = q_ref[...] * (dh ** -0.5)
        q2 = jnp.concatenate([q[:, h * dh:(h + 1) * dh] for h in range(H_B)], axis=0).astype(BF16)
        q2_ref[...] = q2
        q_wide = jnp.concatenate([q2] * H_B, axis=1)
        same_head = (_iota2((rows, SEC), 0) // lq) == (_iota2((rows, SEC), 1) // dh)
        q_bd = jnp.where(same_head, q_wide, jnp.zeros_like(q_wide))
        fill = jnp.zeros((LANES - lq, SEC), F32)
        kn = jnp.concatenate([kn_ref[...], fill], axis=0).astype(BF16)
        vn = jnp.concatenate([vn_ref[...], fill], axis=0).astype(BF16)
        z = _dot_nt(q_bd, kn) + bias
        mask = _iota2((rows, LANES), 1) < (_iota2((rows, LANES), 0) % lq)
        sp = jnp.where(mask, _softplus(z), 0.0)
        a = jnp.where(mask, jnp.exp((z - sp) - _suffix_sum(sp, _strict_upper2(LANES))), 0.0)
        full = _dot(a.astype(BF16), vn)
        acc_ref[...] = jnp.concatenate(
            [full[h * lq:(h + 1) * lq, h * dh:(h + 1) * dh] for h in range(H_B)], axis=0)
        carry_ref[...] = jnp.sum(sp, axis=-1, keepdims=True)

    q2 = q2_ref[...]
    su2 = _strict_upper2(page)
    def head_rows(ref, h):
        return ref[pl.ds(h, page, stride=H_B), :].astype(BF16)

    def own_rows(per_head):
        return jnp.concatenate([x[h * lq:(h + 1) * lq, :] for h, x in enumerate(per_head)], axis=0)

    z = [own_rows([_dot_nt(q2, head_rows(kp, h)) for h in range(H_B)]) + bias
         for kp in page_refs[:ppb]]
    sp = [_softplus(x) for x in z]
    later = [_suffix_sum(x, su2) for x in sp]
    carry = carry_ref[...]
    acc = acc_ref[...]
    for x, p, w, vp in zip(z, sp, later, page_refs[ppb:]):
        a = jnp.exp((x - p) - (w + carry)).astype(BF16)
        acc = acc + own_rows([_dot(a, head_rows(vp, h)) for h in range(H_B)])
        carry = carry + jnp.sum(p, axis=-1, keepdims=True)
    acc_ref[...] = acc
    carry_ref[...] = carry

    @pl.when(j == pl.num_programs(1) - 1)
    def _():
        acc = acc_ref[...]
        y_ref[...] = jnp.concatenate([acc[h * lq:(h + 1) * lq, :] for h in range(H_B)], axis=1)


def sb_sample(proj, sb_bias, cache_k, cache_v, page_table, layer, bsz, length):
    dh = SEC // H_B
    n_pages = page_table.shape[1]
    page = cache_k.shape[2]
    rows = H_B * length
    bias_rows = jnp.repeat(sb_bias.astype(F32), length).reshape(rows, 1)

    def new_sec(s):
        return pl.BlockSpec((length, SEC), lambda b, j, pt, s=s: (b, s))

    ppb = SB_PAGES_PER_STEP
    while n_pages % ppb:
        ppb -= 1

    def past(r):
        return pl.BlockSpec((None, None, page * H_B, dh),
                            lambda b, j, pt, r=r: (layer, pt[b, n_pages - 1 - (j * ppb + r)], 0, 0))

    page_specs = [past(r) for r in range(ppb)]
    grid_spec = pltpu.PrefetchScalarGridSpec(
        num_scalar_prefetch=1,
        grid=(bsz, n_pages // ppb),
        in_specs=[new_sec(QB), new_sec(KB), new_sec(VB),
                  pl.BlockSpec((rows, 1), lambda b, j, pt: (0, 0))] + page_specs + page_specs,
        out_specs=pl.BlockSpec((length, SEC), lambda b, j, pt: (b, 0)),
        scratch_shapes=[pltpu.VMEM((rows, dh), BF16), pltpu.VMEM((rows, dh), F32),
                        pltpu.VMEM((rows, 1), F32)],
    )
    k2d = cache_k.reshape(cache_k.shape[0], cache_k.shape[1], page * H_B, dh)
    v2d = cache_v.reshape(cache_v.shape[0], cache_v.shape[1], page * H_B, dh)
    return pl.pallas_call(
        functools.partial(_sb_sample_body, lq=length, dh=dh, page=page, ppb=ppb),
        out_shape=jax.ShapeDtypeStruct((bsz * length, SEC), F32),
        grid_spec=grid_spec,
        compiler_params=_params(("parallel", "arbitrary"), 40),
        name="sb_sample",
    )(page_table, proj, proj, proj, bias_rows, *([k2d] * ppb), *([v2d] * ppb))


def _conv_body(a_ref, g_ref, buf_ref, w_ref, b_ref, lg_ref, lb_ref, y_ref, new_ref, ext_ref, *, tl):
    hist = CONV_W - 1
    pad = 32 - hist
    step = pl.program_id(1)

    @pl.when(step == 0)
    def _():
        ext_ref[0:SUBLANES, :] = jnp.zeros((SUBLANES, SEC), F32)
        ext_ref[pad:32, :] = buf_ref[0]

    ext_ref[32:32 + tl, :] = a_ref[...] * jax.nn.sigmoid(g_ref[...])
    cols = []
    for c in range(SEC // LANES):
        lanes = slice(c * LANES, (c + 1) * LANES)
        slab = ext_ref[:, lanes]
        nrow = slab.shape[0]
        acc = jnp.zeros((tl, LANES), F32)
        for phase in range(SUBLANES):
            taps = [j for j in range(CONV_W) if (pad + j) % SUBLANES == phase]
            shifted = slab if phase == 0 else pltpu.roll(slab, nrow - phase, axis=0)
            for j in taps:
                start = pad + j - phase
                acc = acc + w_ref[j:j + 1, lanes] * shifted[start:start + tl, :]
        cols.append(acc)
    y = jnp.concatenate(cols, axis=1) + b_ref[...]
    yc = y - jnp.mean(y, axis=-1, keepdims=True)
    yn = yc * lax.rsqrt(jnp.mean(yc * yc, axis=-1, keepdims=True) + EPS)
    yn = yn * lg_ref[...] + lb_ref[...]
    y_ref[...] = yn * jax.nn.sigmoid(yn)
    new_ref[0] = ext_ref[tl + pad:tl + 32, :]
    if tl >= 32:
        ext_ref[0:32, :] = ext_ref[tl:tl + 32, :]


def conformer_conv(proj, buf, w, b, ln_g, ln_b, bsz, length):
    tl = _row_tile(length, CONV_TILE, SUBLANES)
    nt = length // tl
    assert nt == 1 or tl >= 32
    hist = CONV_W - 1
    vec = pl.BlockSpec((1, SEC), lambda bb, t: (0, 0))
    return pl.pallas_call(
        functools.partial(_conv_body, tl=tl),
        out_shape=(jax.ShapeDtypeStruct((bsz * length, SEC), F32),
                   jax.ShapeDtypeStruct((bsz, hist, SEC), F32)),
        grid=(bsz, nt),
        in_specs=[pl.BlockSpec((tl, SEC), lambda bb, t: (bb * nt + t, CA)),
                  pl.BlockSpec((tl, SEC), lambda bb, t: (bb * nt + t, CG)),
                  pl.BlockSpec((1, hist, SEC), lambda bb, t: (bb, 0, 0)),
                  pl.BlockSpec((CONV_W, SEC), lambda bb, t: (0, 0)),
                  vec, vec, vec],
        out_specs=(pl.BlockSpec((tl, SEC), lambda bb, t: (bb * nt + t, 0)),
                   pl.BlockSpec((1, hist, SEC), lambda bb, t: (bb, 0, 0))),
        scratch_shapes=[pltpu.VMEM((32 + tl, SEC), F32)],
        compiler_params=_params(("parallel", "arbitrary"), 32),
        name="conformer_conv",
    )(proj, proj, buf, w, b.reshape(1, SEC), ln_g.reshape(1, SEC), ln_b.reshape(1, SEC))


def _s5_drive(u_ref, br_ref, bi_ref, sr_ref, si_ref, perm):
    nblk, cin, nst = br_ref.shape
    ub = u_ref[...].astype(BF16)
    if perm is not None:
        ub = _dot(perm, ub).astype(BF16)
    for blk in range(nblk):
        ublk = ub[:, blk * cin:(blk + 1) * cin]
        sr_ref[:, blk * nst:(blk + 1) * nst] = _dot(ublk, br_ref[blk])
        si_ref[:, blk * nst:(blk + 1) * nst] = _dot(ublk, bi_ref[blk])


def _s5_readout(sr_ref, si_ref, cr_ref, ci_ref):
    nblk, nst, _ = cr_ref.shape
    ys = []
    for blk in range(nblk):
        lanes = slice(blk * nst, (blk + 1) * nst)
        ys.append(_dot(sr_ref[:, lanes].astype(BF16), cr_ref[blk])
                  - _dot(si_ref[:, lanes].astype(BF16), ci_ref[blk]))
    return jnp.concatenate(ys, axis=1)


def _s5_gate(y, gw_ref, gb_ref, y_ref):
    g = jax.nn.gelu(y)
    gate = _dot(g.astype(BF16), gw_ref[...]) + gb_ref[...]
    y_ref[...] = g * jax.nn.sigmoid(gate)


def _s5_segmented_body(u_ref, h0r_ref, h0i_ref, ar_ref, ai_ref, br_ref, bi_ref, cr_ref, ci_ref, d_ref,
                       gw_ref, gb_ref, y_ref, hr_ref, hi_ref, sr_ref, si_ref, pr_ref, pi_ref,
                       yp_ref, yt_ref, *, tl):
    step = pl.program_id(1)
    nseg = SUBLANES
    slen = tl // nseg
    nblk = br_ref.shape[0]
    nst = br_ref.shape[2]

    @pl.when(step == 0)
    def _():
        hr_ref[...] = h0r_ref[...]
        hi_ref[...] = h0i_ref[...]
        for blk in range(nblk):
            lanes = slice(blk * nst, (blk + 1) * nst)
            ar = ar_ref[:, lanes]
            ai = ai_ref[:, lanes]
            qr, qi = ar, ai
            for t in range(slen):
                pr_ref[t:t + 1, lanes] = qr
                pi_ref[t:t + 1, lanes] = qi
                qr, qi = qr * ar - qi * ai, qr * ai + qi * ar

    dst = _iota2((tl, tl), 0)
    src = (dst % nseg) * slen + dst // nseg
    perm = jnp.where(_iota2((tl, tl), 1) == src, 1.0, 0.0).astype(BF16)
    _s5_drive(u_ref, br_ref, bi_ref, sr_ref, si_ref, perm)

    first = _iota2((nseg, nst), 0) == 0
    for blk in range(nblk):
        lanes = slice(blk * nst, (blk + 1) * nst)
        ar = ar_ref[:, lanes]
        ai = ai_ref[:, lanes]

        def local_step(t, state, lanes=lanes, ar=ar, ai=ai):
            hr, hi = state
            rows = pl.ds(pl.multiple_of(t * nseg, nseg), nseg)
            nr = ar * hr - ai * hi + sr_ref[rows, lanes]
            ni = ar * hi + ai * hr + si_ref[rows, lanes]
            sr_ref[rows, lanes] = nr
            si_ref[rows, lanes] = ni
            return nr, ni

        start = (jnp.where(first, hr_ref[0, :, lanes], 0.0), jnp.where(first, hi_ref[0, :, lanes], 0.0))
        er, ei = lax.fori_loop(0, slen, local_step, start, unroll=4)

        wr = pr_ref[slen - 1:slen, lanes]
        wi = pi_ref[slen - 1:slen, lanes]
        tr, ti = er[0:1, :], ei[0:1, :]
        starts_r = [jnp.zeros_like(tr)]
        starts_i = [jnp.zeros_like(ti)]
        for s in range(1, nseg):
            starts_r.append(tr)
            starts_i.append(ti)
            tr, ti = er[s:s + 1, :] + wr * tr - wi * ti, ei[s:s + 1, :] + wr * ti + wi * tr
        hr_ref[0, :, lanes] = tr
        hi_ref[0, :, lanes] = ti
        gr = jnp.concatenate(starts_r, axis=0)
        gi = jnp.concatenate(starts_i, axis=0)

        def fix_step(t, carry, lanes=lanes, gr=gr, gi=gi):
            rows = pl.ds(pl.multiple_of(t * nseg, nseg), nseg)
            qr = pr_ref[pl.ds(t, 1), lanes]
            qi = pi_ref[pl.ds(t, 1), lanes]
            sr_ref[rows, lanes] += qr * gr - qi * gi
            si_ref[rows, lanes] += qr * gi + qi * gr
            return carry

        lax.fori_loop(0, slen, fix_step, 0, unroll=4)

    yp = _s5_readout(sr_ref, si_ref, cr_ref, ci_ref)
    for c in range(SEC // LANES):
        yp_ref[c] = yp[:, c * LANES:(c + 1) * LANES]
    for c in range(SEC // LANES):
        for s in range(nseg):
            yt_ref[s * slen:(s + 1) * slen, c * LANES:(c + 1) * LANES] = (
                yp_ref[c, pl.ds(s, slen, stride=nseg), :])
    _s5_gate(yt_ref[...] + d_ref[...] * u_ref[...], gw_ref, gb_ref, y_ref)


def _s5_body(u_ref, h0r_ref, h0i_ref, ar_ref, ai_ref, br_ref, bi_ref, cr_ref, ci_ref, d_ref,
             gw_ref, gb_ref, y_ref, hr_ref, hi_ref, sr_ref, si_ref, *, tl):
    step = pl.program_id(1)
    nblk = br_ref.shape[0]
    nst = br_ref.shape[2]

    @pl.when(step == 0)
    def _():
        hr_ref[...] = h0r_ref[...]
        hi_ref[...] = h0i_ref[...]

    _s5_drive(u_ref, br_ref, bi_ref, sr_ref, si_ref, None)

    for blk in range(nblk):
        lanes = slice(blk * nst, (blk + 1) * nst)
        ar = ar_ref[:, lanes]
        ai = ai_ref[:, lanes]

        def scan_step(t, state, lanes=lanes, ar=ar, ai=ai):
            hr, hi = state
            row = pl.ds(t, 1)
            nr = ar * hr - ai * hi + sr_ref[row, lanes]
            ni = ar * hi + ai * hr + si_ref[row, lanes]
            sr_ref[row, lanes] = nr
            si_ref[row, lanes] = ni
            return nr, ni

        hr, hi = lax.fori_loop(0, tl, scan_step, (hr_ref[0, :, lanes], hi_ref[0, :, lanes]),
                               unroll=SUBLANES)
        hr_ref[0, :, lanes] = hr
        hi_ref[0, :, lanes] = hi

    y = _s5_readout(sr_ref, si_ref, cr_ref, ci_ref) + d_ref[...] * u_ref[...]
    _s5_gate(y, gw_ref, gb_ref, y_ref)


def s5_discretize(a_re, a_im, log_dt, b_re, b_im, c_re, c_im):
    ngrp, nst = a_re.shape
    dt = jnp.exp(log_dt.astype(F32))[:, None]
    ar = a_re.astype(F32)
    ai = a_im.astype(F32)
    mag = jnp.exp(ar * dt)
    abar_re = mag * jnp.cos(ai * dt)
    abar_im = mag * jnp.sin(ai * dt)
    zr = abar_re - 1.0
    zi = abar_im
    den = ar * ar + ai * ai
    coef_re = (zr * ar + zi * ai) / den
    coef_im = (zi * ar - zr * ai) / den
    br = b_re.astype(F32)
    bi = b_im.astype(F32)
    bbar_re = coef_re[..., None] * br - coef_im[..., None] * bi
    bbar_im = coef_re[..., None] * bi + coef_im[..., None] * br
    gb = S5_GROUP_BLOCK
    nblk = ngrp // gb
    eye = jnp.eye(gb, dtype=F32)

    def in_blocks(bbar):
        t = bbar.reshape(nblk, gb, nst, S5_CH)
        return jnp.einsum('bgnc,gh->bgchn', t, eye).reshape(nblk, gb * S5_CH, gb * nst).astype(BF16)

    def out_blocks(c):
        t = c.astype(F32).reshape(nblk, gb, S5_CH, nst)
        return jnp.einsum('bgcn,gh->bgnhc', t, eye).reshape(nblk, gb * nst, gb * S5_CH).astype(BF16)

    return (abar_re.reshape(1, ngrp * nst), abar_im.reshape(1, ngrp * nst),
            in_blocks(bbar_re), in_blocks(bbar_im), out_blocks(c_re), out_blocks(c_im))


def s5(proj, h0_re, h0_im, disc, d, gate_w, gate_b, bsz, length):
    abar_re, abar_im, b_re, b_im, c_re, c_im = disc
    nstate = abar_re.shape[1]
    tl = _row_tile(length, S5_TILE, SUBLANES)
    nt = length // tl
    state = pl.BlockSpec((1, 1, nstate), lambda b, t: (b, 0, 0))

    def whole(a):
        return pl.BlockSpec(a.shape, lambda b, t, nd=a.ndim: (0,) * nd)

    d2 = d.reshape(1, SEC)
    gb2 = gate_b.reshape(1, SEC)
    scratch = [pltpu.VMEM((tl, nstate), F32), pltpu.VMEM((tl, nstate), F32)]
    body = _s5_body
    if tl % (SUBLANES * SUBLANES) == 0:
        body = _s5_segmented_body
        slen = tl // SUBLANES
        scratch += [pltpu.VMEM((slen, nstate), F32), pltpu.VMEM((slen, nstate), F32),
                    pltpu.VMEM((SEC // LANES, tl, LANES), F32), pltpu.VMEM((tl, SEC), F32)]
    y, hr, hi = pl.pallas_call(
        functools.partial(body, tl=tl),
        out_shape=(jax.ShapeDtypeStruct((bsz * length, SEC), F32),
                   jax.ShapeDtypeStruct((bsz, 1, nstate), F32),
                   jax.ShapeDtypeStruct((bsz, 1, nstate), F32)),
        grid=(bsz, nt),
        in_specs=[pl.BlockSpec((tl, SEC), lambda b, t: (b * nt + t, UD)), state, state,
                  whole(abar_re), whole(abar_im), whole(b_re), whole(b_im), whole(c_re), whole(c_im),
                  whole(d2), whole(gate_w), whole(gb2)],
        out_specs=(pl.BlockSpec((tl, SEC), lambda b, t: (b * nt + t, 0)), state, state),
        scratch_shapes=scratch,
        compiler_params=_params(("parallel", "arbitrary"), 48),
        name="s5",
    )(proj, h0_re.reshape(bsz, 1, nstate), h0_im.reshape(bsz, 1, nstate),
      abar_re, abar_im, b_re, b_im, c_re, c_im, d2, gate_w, gb2)
    return y, hr, hi


def _cross_body(q_ref, k_ref, v_ref, o_ref, *, dh):
    scale = dh ** -0.5
    for h in range(H_X):
        sl = slice(h * dh, (h + 1) * dh)
        s = _dot_nt(q_ref[:, sl].astype(BF16), k_ref[:, sl].astype(BF16)) * scale
        e = jnp.exp(s - jnp.max(s, axis=-1, keepdims=True))
        p = e / jnp.sum(e, axis=-1, keepdims=True)
        o_ref[:, sl] = _dot(p.astype(BF16), v_ref[:, sl].astype(BF16)).astype(o_ref.dtype)


def cross_attend(q, mem_k, mem_v, k_col, v_col, bsz, length, n_mem):
    wx = q.shape[1]
    tq = _row_tile(length, 512, SUBLANES)
    nq = length // tq
    return pl.pallas_call(
        functools.partial(_cross_body, dh=wx // H_X),
        out_shape=jax.ShapeDtypeStruct((bsz * length, wx), F32),
        grid=(bsz, nq),
        in_specs=[pl.BlockSpec((tq, wx), lambda b, i: (b * nq + i, 0)),
                  pl.BlockSpec((n_mem, wx), lambda b, i: (b, k_col)),
                  pl.BlockSpec((n_mem, wx), lambda b, i: (b, v_col))],
        out_specs=pl.BlockSpec((tq, wx), lambda b, i: (b * nq + i, 0)),
        compiler_params=_params(("parallel", "arbitrary"), 32),
        name="cross_attend",
    )(q, mem_k, mem_v)


def _decoder_layer(x, bsz, length, mem_k, mem_v, k_col, v_col, n_mem, sb_fn, mlstm_state, conv_buf,
                   s5_state, p, kv_stacked, layer, depth):
    h, gates = rmsnorm_gates(x, p['norm_mix_pre'], p['w_gates'])
    proj = matmul(h, p['w_in'], F32)
    y_a, c_new, n_new, m_new = mlstm(proj, gates, p['gate_bias'], *mlstm_state, bsz, length)
    y_b = sb_fn(proj)
    y_c, conv_new = conformer_conv(proj, conv_buf, p['conv_w'], p['conv_b'], p['conv_ln_g'],
                                   p['conv_ln_b'], bsz, length)
    y_d, s5_re, s5_im = s5(proj, s5_state[0], s5_state[1], p['s5_disc'], p['s5_d'], p['s5_gate_w'],
                           p['s5_gate_b'], bsz, length)
    x, hx = mix_out_proj(y_a, y_b, y_c, y_d, p['g_group'], p['w_out'], p['norm_mix_post'], x,
                         p['norm_x_pre'])
    q = matmul(hx, p['w_xq'], F32)
    o = cross_attend(q, mem_k, mem_v, k_col, v_col, bsz, length, n_mem)
    x, hf = matmul_norm_res(o, p['w_xo'], p['norm_x_post'], x, p['norm_ffn_pre'])
    kv_stacked = (head_split(proj, KB, H_B, kv_stacked[0], layer, depth),
                  head_split(proj, VB, H_B, kv_stacked[1], layer, depth))
    return x, hf, kv_stacked, (c_new, n_new, m_new), conv_new, (s5_re, s5_im)


def kernel(x_prompt, x_sample, cache_sb_k, cache_sb_v, page_table, state_mlstm_c, state_mlstm_n, state_mlstm_m, state_conv, state_s5_re, state_s5_im, cache_mem_k, cache_mem_v, mem_prompt, norm_mix_pre, w_in, b_mlstm_gates, sb_bias, conv_w, conv_b, conv_ln_g, conv_ln_b, s5_a_re, s5_a_im, s5_log_dt, s5_b_re, s5_b_im, s5_c_re, s5_c_im, s5_d, s5_gate_w, s5_gate_b, g_group, w_out, norm_mix_post, norm_mem, norm_x_pre, w_xq, w_xk, w_xv, w_xo, norm_x_post, norm_ffn_pre, w_up, w_down, norm_ffn_post):
    n_b, seq, d_model = x_prompt.shape
    n_db, dec_seq, _ = x_sample.shape
    depth = w_in.shape[0]
    n_mem = mem_prompt.shape[1]
    w_x = w_xq.shape[2]
    dh_a = SEC // H_A
    ngrp = s5_a_re.shape[1]
    n_gate = 2 * H_A
    gate_lo = 4 * SEC

    xp = x_prompt.reshape(n_b * seq, d_model)
    xs = x_sample.reshape(n_db * dec_seq, d_model)
    mem2d = mem_prompt.reshape(n_b * n_mem, d_model)
    w_in_t = jnp.transpose(w_in, (0, 2, 1))
    outs = [[] for _ in range(14)]
    kv_p = (None, None)
    kv_s = (None, None)
    for l in range(depth):
        w_in_main, w_in_gates = cast_w_in(w_in_t, l, gate_lo, n_gate)
        p = {
            'norm_mix_pre': norm_mix_pre[l],
            'w_in': w_in_main,
            'w_gates': w_in_gates,
            'gate_bias': jnp.pad(b_mlstm_gates[l].astype(F32), (0, LANES - n_gate)).reshape(1, LANES),
            'conv_w': conv_w[l], 'conv_b': conv_b[l], 'conv_ln_g': conv_ln_g[l], 'conv_ln_b': conv_ln_b[l],
            's5_disc': s5_discretize(s5_a_re[l], s5_a_im[l], s5_log_dt[l], s5_b_re[l], s5_b_im[l],
                                     s5_c_re[l], s5_c_im[l]),
            's5_d': s5_d[l], 's5_gate_w': s5_gate_w[l].astype(BF16), 's5_gate_b': s5_gate_b[l],
            'g_group': g_group[l], 'w_out': cast_bf16(w_out, l), 'norm_mix_post': norm_mix_post[l],
            'norm_x_pre': norm_x_pre[l], 'w_xq': w_xq[l].astype(BF16), 'w_xo': w_xo[l].astype(BF16),
            'norm_x_post': norm_x_post[l], 'norm_ffn_pre': norm_ffn_pre[l],
        }
        w_kv = jnp.concatenate([w_xk[l], w_xv[l]], axis=1).astype(BF16)
        mem_kv = matmul(rmsnorm_bf16(mem2d, norm_mem[l]), w_kv, F32, tm_cap=512)
        zero_mlstm = (jnp.zeros((n_b, H_A, dh_a, dh_a), F32), jnp.zeros((n_b, H_A, dh_a), F32),
                      jnp.zeros((n_b, H_A), F32))
        zero_s5 = (jnp.zeros((n_b, ngrp, S5_N), F32), jnp.zeros((n_b, ngrp, S5_N), F32))
        sb_s = functools.partial(sb_sample, sb_bias=sb_bias[l], cache_k=cache_sb_k, cache_v=cache_sb_v,
                                 page_table=page_table, layer=l, bsz=n_db, length=dec_seq)
        xs, hf_s, kv_s, mst_s, cnv_s, s5_s = _decoder_layer(
            xs, n_db, dec_seq, cache_mem_k[l].reshape(n_db * n_mem, w_x),
            cache_mem_v[l].reshape(n_db * n_mem, w_x), 0, 0, n_mem, sb_s,
            (state_mlstm_c[l], state_mlstm_n[l], state_mlstm_m[l]), state_conv[l],
            (state_s5_re[l], state_s5_im[l]), p, kv_s, l, depth)
        w_up_tiles, u_s = cast_up_with_rows(w_up, l, FFN_TILE, hf_s)
        w_down_bf, xs = cast_down_with_rows(w_down, l, u_s, norm_ffn_post[l], xs)
        sb_p = functools.partial(sb_prompt, sb_bias=sb_bias[l], bsz=n_b, length=seq)
        xp, hf_p, kv_p, mst_p, cnv_p, s5_p = _decoder_layer(
            xp, n_b, seq, mem_kv, mem_kv, 0, 1, n_mem, sb_p, zero_mlstm,
            jnp.zeros((n_b, CONV_W - 1, SEC), F32), zero_s5, p, kv_p, l, depth)
        xp = ffn(hf_p, w_up_tiles, w_down_bf, norm_ffn_post[l], xp)
        mk = mem_kv[:, :w_x].reshape(n_b, n_mem, H_X, w_x // H_X)
        mv = mem_kv[:, w_x:].reshape(n_b, n_mem, H_X, w_x // H_X)
        s5_shape = lambda a: a.reshape(a.shape[0], ngrp, S5_N)
        layer_out = (mst_p[0], mst_p[1], mst_p[2], mst_s[0], mst_s[1], mst_s[2],
                     cnv_p, cnv_s, s5_shape(s5_p[0]), s5_shape(s5_p[1]), s5_shape(s5_s[0]),
                     s5_shape(s5_s[1]), mk, mv)
        for acc, val in zip(outs, layer_out):
            acc.append(val)
    stacked = tuple(jnp.stack(vals, axis=0) for vals in outs)
    dh_b = SEC // H_B
    kv = tuple(a.reshape(depth, n, t, H_B, dh_b)
               for a, n, t in ((kv_p[0], n_b, seq), (kv_p[1], n_b, seq),
                               (kv_s[0], n_db, dec_seq), (kv_s[1], n_db, dec_seq)))
    return (xp.reshape(n_b, seq, d_model), xs.reshape(n_db, dec_seq, d_model)) + kv + stacked
```

```python
import functools

import jax
import jax.numpy as jnp
from jax import lax
from jax.experimental import pallas as pl
from jax.experimental.pallas import tpu as pltpu

F32 = jnp.float32
BF16 = jnp.bfloat16
EPS = 1e-6

H_A = 4
H_B = 8
CONV_W = 31
S5_CH = 16
S5_N = 64
H_X = 4
MLSTM_CHUNK_MAX = 256
SB_TILE = 256
SB_HEADS_PER_STEP = 4
SB_PAGES_PER_STEP = 8
CONV_TILE = 64
S5_TILE = 256
FFN_TILE = 1024
FFN_OUT_CHUNK = 1024
FFN_NORM_ROWS = 32
S5_GROUP_BLOCK = 16

LANES = 128
SUBLANES = 8
MIB = 1024 * 1024

SEC = 1024
QA, KA, VA, OA, QB, KB, VB, CA, CG, UD = range(10)


def _params(semantics, vmem_mib):
    return pltpu.CompilerParams(dimension_semantics=semantics, vmem_limit_bytes=vmem_mib * MIB)


def _row_tile(m, cap, mult=16):
    t = min(m, cap)
    while m % t or t % mult:
        t -= 1
    return t


def _log_sigmoid_pair(z):
    sp = jnp.log1p(jnp.exp(-jnp.abs(z)))
    return jnp.minimum(z, 0.0) - sp, -(jnp.maximum(z, 0.0) + sp)


def _split_bf16(x, terms):
    out = []
    r = x
    for _ in range(terms):
        p = r.astype(BF16)
        out.append(p)
        r = r - p.astype(F32)
    return out


def _dot(a, b):
    return jnp.dot(a, b, preferred_element_type=F32)


def _dot_nt(a, b):
    return lax.dot_general(a, b, (((1,), (1,)), ((), ())), preferred_element_type=F32)


def _dot_tn(a, b):
    return lax.dot_general(a, b, (((0,), (0,)), ((), ())), preferred_element_type=F32)


def _iota2(shape, dim):
    return lax.broadcasted_iota(jnp.int32, shape, dim)


def _rmsnorm_body(x_ref, g_ref, o_ref):
    x = x_ref[...]
    r = lax.rsqrt(jnp.mean(x * x, axis=-1, keepdims=True) + EPS)
    o_ref[...] = ((x * r) * g_ref[...]).astype(o_ref.dtype)


def rmsnorm_bf16(x, g):
    m, d = x.shape
    tr = _row_tile(m, 256)
    return pl.pallas_call(
        _rmsnorm_body,
        out_shape=jax.ShapeDtypeStruct((m, d), BF16),
        grid=(m // tr,),
        in_specs=[pl.BlockSpec((tr, d), lambda i: (i, 0)), pl.BlockSpec((1, d), lambda i: (0, 0))],
        out_specs=pl.BlockSpec((tr, d), lambda i: (i, 0)),
        compiler_params=_params(("parallel",), 32),
        name="rmsnorm",
    )(x, g.reshape(1, d))


def _rmsnorm_gates_body(x_ref, g_ref, wg_ref, o_ref, gates_ref):
    x = x_ref[...]
    r = lax.rsqrt(jnp.mean(x * x, axis=-1, keepdims=True) + EPS)
    h = ((x * r) * g_ref[...]).astype(BF16)
    o_ref[...] = h
    gates_ref[...] = _dot(h, wg_ref[...])


def rmsnorm_gates(x, g, wg):
    m, d = x.shape
    tr = _row_tile(m, 256)
    return pl.pallas_call(
        _rmsnorm_gates_body,
        out_shape=(jax.ShapeDtypeStruct((m, d), BF16), jax.ShapeDtypeStruct((m, LANES), F32)),
        grid=(m // tr,),
        in_specs=[pl.BlockSpec((tr, d), lambda i: (i, 0)), pl.BlockSpec((1, d), lambda i: (0, 0)),
                  pl.BlockSpec((d, LANES), lambda i: (0, 0))],
        out_specs=(pl.BlockSpec((tr, d), lambda i: (i, 0)), pl.BlockSpec((tr, LANES), lambda i: (i, 0))),
        compiler_params=_params(("parallel",), 32),
        name="rmsnorm_gates",
    )(x, g.reshape(1, d), wg)


def _matmul_body(a_ref, w_ref, o_ref):
    o_ref[...] = _dot(a_ref[...], w_ref[...]).astype(o_ref.dtype)


def matmul(a, w, out_dtype, tm_cap=1024, tn_cap=1024):
    m, k = a.shape
    n = w.shape[1]
    tm = _row_tile(m, tm_cap)
    tn = _row_tile(n, tn_cap, LANES)
    return pl.pallas_call(
        _matmul_body,
        out_shape=jax.ShapeDtypeStruct((m, n), out_dtype),
        grid=(m // tm, n // tn),
        in_specs=[pl.BlockSpec((tm, k), lambda i, j: (i, 0)), pl.BlockSpec((k, tn), lambda i, j: (0, j))],
        out_specs=pl.BlockSpec((tm, tn), lambda i, j: (i, j)),
        compiler_params=_params(("parallel", "arbitrary"), 56),
        name="matmul",
    )(a, w)


def _norm_residual(y, g_ref, res_ref, o_ref):
    r = lax.rsqrt(jnp.mean(y * y, axis=-1, keepdims=True) + EPS)
    o_ref[...] = res_ref[...] + (y * r) * g_ref[...]


def _norm_residual_next(y, g_ref, res_ref, g_next_ref, o_ref, h_ref):
    _norm_residual(y, g_ref, res_ref, o_ref)
    x = o_ref[...]
    r = lax.rsqrt(jnp.mean(x * x, axis=-1, keepdims=True) + EPS)
    h_ref[...] = ((x * r) * g_next_ref[...]).astype(h_ref.dtype)


def _matmul_norm_res_body(a_ref, w_ref, g_ref, res_ref, g_next_ref, o_ref, h_ref):
    _norm_residual_next(_dot(a_ref[...].astype(BF16), w_ref[...]), g_ref, res_ref, g_next_ref,
                        o_ref, h_ref)


def _mix_out_body(ya_ref, yb_ref, yc_ref, yd_ref, gg_ref, w_ref, g_ref, res_ref, g_next_ref,
                  o_ref, h_ref):
    wg = ya_ref.shape[1]
    parts = []
    for i, ref in enumerate((ya_ref, yb_ref, yc_ref, yd_ref)):
        y = ref[...]
        r = lax.rsqrt(jnp.mean(y * y, axis=-1, keepdims=True) + EPS)
        parts.append(((y * r) * gg_ref[:, i * wg:(i + 1) * wg]).astype(BF16))
    mixed = jnp.concatenate(parts, axis=1)
    _norm_residual_next(_dot(mixed, w_ref[...]), g_ref, res_ref, g_next_ref, o_ref, h_ref)


def mix_out_proj(ya, yb, yc, yd, g_group, w, g, res, g_next):
    m, wg = ya.shape
    n = w.shape[1]
    tm = _row_tile(m, 128)
    rows = pl.BlockSpec((tm, n), lambda i: (i, 0))
    part = pl.BlockSpec((tm, wg), lambda i: (i, 0))
    vec = pl.BlockSpec((1, n), lambda i: (0, 0))
    return pl.pallas_call(
        _mix_out_body,
        out_shape=(jax.ShapeDtypeStruct((m, n), F32), jax.ShapeDtypeStruct((m, n), BF16)),
        grid=(m // tm,),
        in_specs=[part, part, part, part, pl.BlockSpec((1, 4 * wg), lambda i: (0, 0)),
                  pl.BlockSpec(w.shape, lambda i: (0, 0), pipeline_mode=pl.Buffered(1)),
                  vec, rows, vec],
        out_specs=(rows, rows),
        compiler_params=_params(("parallel",), 60),
        name="mix_out_proj",
    )(ya, yb, yc, yd, g_group.reshape(1, 4 * wg), w, g.reshape(1, n), res, g_next.reshape(1, n))


def matmul_norm_res(a, w, g, res, g_next):
    m, kdim = a.shape
    n = w.shape[1]
    w_bytes = kdim * n * w.dtype.itemsize
    tm = _row_tile(m, 128 if w_bytes > 16 * MIB else 256)
    rows = pl.BlockSpec((tm, n), lambda i: (i, 0))
    vec = pl.BlockSpec((1, n), lambda i: (0, 0))
    return pl.pallas_call(
        _matmul_norm_res_body,
        out_shape=(jax.ShapeDtypeStruct((m, n), F32), jax.ShapeDtypeStruct((m, n), BF16)),
        grid=(m // tm,),
        in_specs=[pl.BlockSpec((tm, kdim), lambda i: (i, 0)),
                  pl.BlockSpec((kdim, n), lambda i: (0, 0), pipeline_mode=pl.Buffered(1)),
                  vec, rows, vec],
        out_specs=(rows, rows),
        compiler_params=_params(("parallel",), 60 if w_bytes > 16 * MIB else 40),
        name="matmul_norm_res",
    )(a, w, g.reshape(1, n), res, g_next.reshape(1, n))


def _ffn_body(h_ref, wu_ref, wd_ref, g_ref, res_ref, o_ref, *, nf):
    f = pl.program_id(1)

    @pl.when(f == 0)
    def _():
        o_ref[...] = jnp.zeros_like(o_ref)

    u = _dot(h_ref[...], wu_ref[...])
    u = jnp.square(jnp.maximum(u, 0.0)).astype(BF16)
    d = o_ref.shape[1]
    for c in range(0, d, FFN_OUT_CHUNK):
        cols = slice(c, c + FFN_OUT_CHUNK)
        o_ref[:, cols] += _dot(u, wd_ref[:, cols])

    @pl.when(f == nf - 1)
    def _():
        def rows_step(r, carry):
            rows = pl.ds(pl.multiple_of(r * FFN_NORM_ROWS, FFN_NORM_ROWS), FFN_NORM_ROWS)
            _norm_residual(o_ref[rows, :], g_ref, res_ref.at[rows, :], o_ref.at[rows, :])
            return carry

        lax.fori_loop(0, o_ref.shape[0] // FFN_NORM_ROWS, rows_step, 0)


def ffn(h, w_up_tiles, w_down, g, res, tm_cap=512):
    m, d = h.shape
    nf, _, tf = w_up_tiles.shape
    tm = _row_tile(m, tm_cap)
    once = pl.Buffered(1)
    return pl.pallas_call(
        functools.partial(_ffn_body, nf=nf),
        out_shape=jax.ShapeDtypeStruct((m, d), F32),
        grid=(m // tm, nf),
        in_specs=[pl.BlockSpec((tm, d), lambda i, f: (i, 0), pipeline_mode=once),
                  pl.BlockSpec((None, d, tf), lambda i, f: (f, 0, 0)),
                  pl.BlockSpec((tf, d), lambda i, f: (f, 0)),
                  pl.BlockSpec((1, d), lambda i, f: (0, 0)),
                  pl.BlockSpec((tm, d), lambda i, f: (i, 0), pipeline_mode=once)],
        out_specs=pl.BlockSpec((tm, d), lambda i, f: (i, 0), pipeline_mode=once),
        compiler_params=_params(("parallel", "arbitrary"), 60),
        name="ffn",
    )(h, w_up_tiles, w_down, g.reshape(1, d), res)


def _cast_body(x_ref, o_ref):
    o_ref[...] = x_ref[...].astype(o_ref.dtype)


def _cast_up_body(x_ref, hs_ref, o_ref, u_ref):
    i = pl.program_id(1)
    wb = x_ref[...].astype(BF16)
    o_ref[...] = wb
    part = _dot(hs_ref[...], wb)

    @pl.when(i == 0)
    def _():
        u_ref[...] = part

    @pl.when(i > 0)
    def _():
        u_ref[...] += part


def cast_up_with_rows(w_up, layer, tc, hs):
    _, r, c = w_up.shape
    ms = hs.shape[0]
    tr = _row_tile(r, 1024)
    return pl.pallas_call(
        _cast_up_body,
        out_shape=(jax.ShapeDtypeStruct((c // tc, r, tc), BF16), jax.ShapeDtypeStruct((ms, c), F32)),
        grid=(c // tc, r // tr),
        in_specs=[pl.BlockSpec((None, tr, tc), lambda j, i: (layer, i, j)),
                  pl.BlockSpec((ms, tr), lambda j, i: (0, i))],
        out_specs=(pl.BlockSpec((None, tr, tc), lambda j, i: (j, i, 0)),
                   pl.BlockSpec((ms, tc), lambda j, i: (0, j))),
        compiler_params=_params(("parallel", "arbitrary"), 40),
        name="cast_up_with_rows",
    )(w_up, hs)


def _cast_down_body(x_ref, u_ref, g_ref, res_ref, o_ref, y_ref, *, steps):
    i = pl.program_id(0)
    wb = x_ref[...].astype(BF16)
    o_ref[...] = wb
    u = jnp.square(jnp.maximum(u_ref[...], 0.0)).astype(BF16)
    part = _dot(u, wb)

    @pl.when(i == 0)
    def _():
        y_ref[...] = part

    @pl.when(i > 0)
    def _():
        y_ref[...] += part

    @pl.when(i == steps - 1)
    def _():
        _norm_residual(y_ref[...], g_ref, res_ref, y_ref)


def cast_down_with_rows(w_down, layer, u_pre, g, res):
    _, f, d = w_down.shape
    ms = u_pre.shape[0]
    tr = _row_tile(f, 512)
    steps = f // tr
    return pl.pallas_call(
        functools.partial(_cast_down_body, steps=steps),
        out_shape=(jax.ShapeDtypeStruct((f, d), BF16), jax.ShapeDtypeStruct((ms, d), F32)),
        grid=(steps,),
        in_specs=[pl.BlockSpec((None, tr, d), lambda i: (layer, i, 0)),
                  pl.BlockSpec((ms, tr), lambda i: (0, i)),
                  pl.BlockSpec((1, d), lambda i: (0, 0)),
                  pl.BlockSpec((ms, d), lambda i: (0, 0))],
        out_specs=(pl.BlockSpec((tr, d), lambda i: (i, 0)),
                   pl.BlockSpec((ms, d), lambda i: (0, 0))),
        compiler_params=_params(("arbitrary",), 48),
        name="cast_down_with_rows",
    )(w_down, u_pre, g.reshape(1, d), res)


def cast_bf16(x, layer):
    _, r, c = x.shape
    tr = _row_tile(r, 512)
    tc = _row_tile(c, 4096, LANES)
    return pl.pallas_call(
        _cast_body,
        out_shape=jax.ShapeDtypeStruct((r, c), BF16),
        grid=(r // tr, c // tc),
        in_specs=[pl.BlockSpec((None, tr, tc), lambda i, j: (layer, i, j))],
        out_specs=pl.BlockSpec((tr, tc), lambda i, j: (i, j)),
        compiler_params=_params(("parallel", "parallel"), 40),
        name="cast_bf16",
    )(x)


def _cast_w_in_body(x_ref, nxt_ref, o_ref, gates_ref, *, aligned_tiles, skip):
    j = pl.program_id(1)
    tc = x_ref.shape[0]

    @pl.when(j < aligned_tiles)
    def _():
        o_ref[...] = x_ref[...].T.astype(o_ref.dtype)

    @pl.when(j >= aligned_tiles)
    def _():
        wide = jnp.concatenate([x_ref[...], nxt_ref[...]], axis=0)
        o_ref[...] = wide[skip:skip + tc, :].T.astype(o_ref.dtype)

    @pl.when(j == aligned_tiles)
    def _():
        fill = jnp.zeros((LANES - skip, x_ref.shape[1]), F32)
        gates_ref[...] = jnp.concatenate([x_ref[0:skip, :], fill], axis=0).T.astype(gates_ref.dtype)


def cast_w_in(w_in_t, layer, gate_lo, n_gate):
    _, c, r = w_in_t.shape
    tc = SEC
    tr = _row_tile(r, 512)
    n_out = c - n_gate
    assert gate_lo % tc == 0 and n_out % tc == 0 and n_gate == SUBLANES
    return pl.pallas_call(
        functools.partial(_cast_w_in_body, aligned_tiles=gate_lo // tc, skip=n_gate),
        out_shape=(jax.ShapeDtypeStruct((r, n_out), BF16), jax.ShapeDtypeStruct((r, LANES), BF16)),
        grid=(r // tr, n_out // tc),
        in_specs=[pl.BlockSpec((None, tc, tr), lambda i, j: (layer, j, i)),
                  pl.BlockSpec((None, n_gate, tr), lambda i, j: (layer, (j + 1) * (tc // n_gate), i))],
        out_specs=(pl.BlockSpec((tr, tc), lambda i, j: (i, j)),
                   pl.BlockSpec((tr, LANES), lambda i, j: (i, 0))),
        compiler_params=_params(("parallel", "arbitrary"), 32),
        name="cast_w_in",
    )(w_in_t, w_in_t)


def _head_split_body(x_ref, *refs, nh, dh):
    o_ref = refs[-1]
    rows = x_ref.shape[0]
    for h in range(nh):
        o_ref[pl.ds(h, rows, stride=nh), :] = x_ref[:, h * dh:(h + 1) * dh]


def head_split(proj, section, nh, stacked, layer, depth):
    rows = proj.shape[0]
    dh = SEC // nh
    tr = _row_tile(rows, 512, SUBLANES)
    nt = rows // tr
    args = (proj,) if stacked is None else (proj, stacked)
    in_specs = [pl.BlockSpec((tr, SEC), lambda i: (i, section))]
    if stacked is not None:
        in_specs.append(pl.BlockSpec(memory_space=pl.ANY))
    return pl.pallas_call(
        functools.partial(_head_split_body, nh=nh, dh=dh),
        out_shape=jax.ShapeDtypeStruct((depth * rows * nh, dh), F32),
        grid=(nt,),
        in_specs=in_specs,
        out_specs=pl.BlockSpec((tr * nh, dh), lambda i: (layer * nt + i, 0)),
        input_output_aliases={} if stacked is None else {1: 0},
        compiler_params=_params(("parallel",), 32),
        name="head_split",
    )(*args)


def _mlstm_body(q_ref, k_ref, v_ref, o_ref, gates_ref, bias_ref, c0_ref, n0_ref, m0_ref,
                y_ref, c_ref, n_ref, m_ref, *, cs, dh):
    step = pl.program_id(1)

    @pl.when(step == 0)
    def _():
        c_ref[...] = c0_ref[...]
        n_ref[...] = n0_ref[...]
        m_ref[...] = m0_ref[...]

    g = gates_ref[...] + bias_ref[...]
    col = _iota2(g.shape, 1)
    x = jnp.where(col >= H_A, _log_sigmoid_pair(g)[0], g)
    eye = jnp.where(_iota2((LANES, LANES), 0) == _iota2((LANES, LANES), 1), 1.0, 0.0).astype(BF16)
    rr = _iota2((cs, cs), 0)
    cc = _iota2((cs, cs), 1)
    lower = jnp.where(rr >= cc, 1.0, 0.0).astype(BF16)
    upper = jnp.where(rr <= cc, 1.0, 0.0).astype(BF16)
    xs = _split_bf16(x, 3)
    x_rows = sum(_dot_nt(eye, p) for p in xs)
    fc_cols = sum(_dot(lower, p) for p in xs)
    fc_rows = sum(_dot(p, upper) for p in _split_bf16(x_rows, 3))
    causal = rr >= cc
    scale = dh ** -0.5

    for h in range(H_A):
        sl = slice(h * dh, (h + 1) * dh)
        q = q_ref[:, sl]
        k = k_ref[:, sl] * scale
        v = v_ref[:, sl]
        qb = q.astype(BF16)
        kb = k.astype(BF16)
        i_col = x[:, h:h + 1]
        i_row = x_rows[h:h + 1, :]
        f_col = fc_cols[:, H_A + h:H_A + h + 1]
        f_row = fc_rows[H_A + h:H_A + h + 1, :]
        m_prev = m_ref[0, :, h:h + 1]
        c_prev = c_ref[0, h]
        n_prev = n_ref[0, h:h + 1, :]

        dmat = jnp.where(causal, f_col - f_row + i_row, -jnp.inf)
        carry_log = f_col + m_prev
        m_t = jnp.maximum(jnp.max(dmat, axis=-1, keepdims=True), carry_log)
        s = _dot_nt(qb, kb) * jnp.exp(dmat - m_t)
        w_prev = jnp.exp(carry_log - m_t)
        num = _dot(s.astype(BF16), v.astype(BF16)) + w_prev * _dot_nt(qb, c_prev.astype(BF16))
        den = jnp.sum(s, axis=-1, keepdims=True) + w_prev * jnp.sum(q * n_prev, axis=-1, keepdims=True)
        hid = num / jnp.maximum(jnp.abs(den), jnp.exp(-m_t))
        y_ref[:, sl] = jax.nn.sigmoid(o_ref[:, sl]) * hid

        m_new = m_t[cs - 1:cs, :]
        f_last = f_col[cs - 1:cs, :]
        decay = jnp.exp(f_last + m_prev - m_new)
        w_end = jnp.exp(f_last - f_col + i_col - m_new)
        c_ref[0, h] = decay * c_prev + _dot_tn((w_end * v).astype(BF16), kb)
        n_ref[0, h:h + 1, :] = decay * n_prev + jnp.sum(w_end * k, axis=0, keepdims=True)
        m_ref[0, :, h:h + 1] = m_new


def mlstm(proj, gates, bias, c0, n0, m0, bsz, length):
    dh = SEC // H_A
    cs = _row_tile(length, MLSTM_CHUNK_MAX, SUBLANES)
    nc = length // cs
    rows = bsz * length

    def sec(s):
        return pl.BlockSpec((cs, SEC), lambda b, c, s=s: (b * nc + c, s))

    state_c = pl.BlockSpec((1, H_A, dh, dh), lambda b, c: (b, 0, 0, 0))
    state_n = pl.BlockSpec((1, H_A, dh), lambda b, c: (b, 0, 0))
    state_m = pl.BlockSpec((1, 1, H_A), lambda b, c: (b, 0, 0))
    y, c, n, m = pl.pallas_call(
        functools.partial(_mlstm_body, cs=cs, dh=dh),
        out_shape=(jax.ShapeDtypeStruct((rows, SEC), F32),
                   jax.ShapeDtypeStruct((bsz, H_A, dh, dh), F32),
                   jax.ShapeDtypeStruct((bsz, H_A, dh), F32),
                   jax.ShapeDtypeStruct((bsz, 1, H_A), F32)),
        grid=(bsz, nc),
        in_specs=[sec(QA), sec(KA), sec(VA), sec(OA),
                  pl.BlockSpec((cs, LANES), lambda b, c: (b * nc + c, 0)),
                  pl.BlockSpec((1, LANES), lambda b, c: (0, 0)),
                  state_c, state_n, state_m],
        out_specs=(pl.BlockSpec((cs, SEC), lambda b, c: (b * nc + c, 0)), state_c, state_n, state_m),
        compiler_params=_params(("parallel", "arbitrary"), 48),
        name="mlstm",
    )(proj, proj, proj, proj, gates, bias, c0, n0, m0.reshape(bsz, 1, H_A))
    return y, c, n, m.reshape(bsz, H_A)


def _softplus(z):
    return jnp.maximum(z, 0.0) + jnp.log(1.0 + jnp.exp(-jnp.abs(z)))


def _suffix_sum(sp, strict_upper2):
    hi, lo = _split_bf16(sp, 2)
    return _dot(jnp.concatenate([hi, lo], axis=1), strict_upper2)


def _strict_upper2(n):
    j = _iota2((2 * n, n), 0)
    j = jnp.where(j >= n, j - n, j)
    return jnp.where(j > _iota2((2 * n, n), 1), 1.0, 0.0).astype(BF16)


def _sb_prompt_body(bias_ref, q_ref, k_ref, v_ref, y_ref, *, t, dh, hpb):
    g = pl.program_id(1)
    i = pl.program_id(2)
    su2 = _strict_upper2(t)
    diag_mask = _iota2((t, t), 1) < _iota2((t, t), 0)
    heads = [slice(hh * dh, (hh + 1) * dh) for hh in range(hpb)]
    bias = [bias_ref[g * hpb + hh] for hh in range(hpb)]
    q = [(q_ref[:, sl] * (dh ** -0.5)).astype(BF16) for sl in heads]

    def tile(j, state, mask):
        rows = pl.ds(pl.multiple_of(j * t, t), t)
        z = [_dot_nt(q[hh], k_ref[rows, sl].astype(BF16)) + bias[hh] for hh, sl in enumerate(heads)]
        sp = [_softplus(x) if mask is None else jnp.where(mask, _softplus(x), 0.0) for x in z]
        later = [_suffix_sum(x, su2) + state[hh][0] for hh, x in enumerate(sp)]
        a = [jnp.exp((x - p) - w) for x, p, w in zip(z, sp, later)]
        if mask is not None:
            a = [jnp.where(mask, x, 0.0) for x in a]
        acc = [state[hh][1] + _dot(a[hh].astype(BF16), v_ref[rows, sl].astype(BF16))
               for hh, sl in enumerate(heads)]
        carry = [state[hh][0] + jnp.sum(sp[hh], axis=-1, keepdims=True) for hh in range(hpb)]
        return tuple(zip(carry, acc))

    zero = tuple((jnp.zeros((t, 1), F32), jnp.zeros((t, dh), F32)) for _ in heads)
    state = tile(i, zero, diag_mask)
    state = lax.fori_loop(0, i, lambda n, st: tile(i - 1 - n, st, None), state)
    for hh, sl in enumerate(heads):
        y_ref[:, sl] = state[hh][1]


def sb_prompt(proj, sb_bias, bsz, length):
    dh = SEC // H_B
    hpb = SB_HEADS_PER_STEP
    wide = hpb * dh
    t = _row_tile(length, SB_TILE, SUBLANES)
    nq = length // t
    cpb = SEC // wide
    return pl.pallas_call(
        functools.partial(_sb_prompt_body, t=t, dh=dh, hpb=hpb),
        out_shape=jax.ShapeDtypeStruct((bsz * length, SEC), F32),
        grid=(bsz, H_B // hpb, nq),
        in_specs=[pl.BlockSpec(memory_space=pltpu.SMEM),
                  pl.BlockSpec((t, wide), lambda b, g, i: (b * nq + i, QB * cpb + g)),
                  pl.BlockSpec((length, wide), lambda b, g, i: (b, KB * cpb + g)),
                  pl.BlockSpec((length, wide), lambda b, g, i: (b, VB * cpb + g))],
        out_specs=pl.BlockSpec((t, wide), lambda b, g, i: (b * nq + i, g)),
        compiler_params=_params(("parallel", "parallel", "arbitrary"), 40),
        name="sb_prompt",
    )(sb_bias, proj, proj, proj)


def _sb_sample_body(pt_ref, q_ref, kn_ref, vn_ref, bias_ref, *refs, lq, dh, page, ppb):
    del pt_ref
    page_refs = refs[:2 * ppb]
    y_ref, q2_ref, acc_ref, carry_ref = refs[2 * ppb:]
    j = pl.program_id(1)
    rows = H_B * lq
    bias = bias_ref[...]

    @pl.when(j == 0)
    def _():
        q = q_ref[...] * (dh ** -0.5)
        q2 = jnp.concatenate([q[:, h * dh:(h + 1) * dh] for h in range(H_B)], axis=0).astype(BF16)
        q2_ref[...] = q2
        q_wide = jnp.concatenate([q2] * H_B, axis=1)
        same_head = (_iota2((rows, SEC), 0) // lq) == (_iota2((rows, SEC), 1) // dh)
        q_bd = jnp.where(same_head, q_wide, jnp.zeros_like(q_wide))
        fill = jnp.zeros((LANES - lq, SEC), F32)
        kn = jnp.concatenate([kn_ref[...], fill], axis=0).astype(BF16)
        vn = jnp.concatenate([vn_ref[...], fill], axis=0).astype(BF16)
        z = _dot_nt(q_bd, kn) + bias
        mask = _iota2((rows, LANES), 1) < (_iota2((rows, LANES), 0) % lq)
        sp = jnp.where(mask, _softplus(z), 0.0)
        a = jnp.where(mask, jnp.exp((z - sp) - _suffix_sum(sp, _strict_upper2(LANES))), 0.0)
        full = _dot(a.astype(BF16), vn)
        acc_ref[...] = jnp.concatenate(
            [full[h * lq:(h + 1) * lq, h * dh:(h + 1) * dh] for h in range(H_B)], axis=0)
        carry_ref[...] = jnp.sum(sp, axis=-1, keepdims=True)

    q2 = q2_ref[...]
    su2 = _strict_upper2(page)
    def head_rows(ref, h):
        return ref[pl.ds(h, page, stride=H_B), :].astype(BF16)

    def own_rows(per_pair, n):
        return jnp.concatenate(
            [x[h * lq:(h + 1) * lq, (h % 2) * n:(h % 2 + 1) * n]
             for h, x in ((h, per_pair[h // 2]) for h in range(H_B))], axis=0)

    def scores(kp):
        pairs = [jnp.concatenate([head_rows(kp, h), head_rows(kp, h + 1)], axis=0)
                 for h in range(0, H_B, 2)]
        return own_rows([_dot_nt(q2, kk) for kk in pairs], page)

    def weighted(a, vp):
        pairs = [jnp.concatenate([head_rows(vp, h), head_rows(vp, h + 1)], axis=1)
                 for h in range(0, H_B, 2)]
        return own_rows([_dot(a, vv) for vv in pairs], dh)

    z = [scores(kp) + bias for kp in page_refs[:ppb]]
    sp = [_softplus(x) for x in z]
    later = [_suffix_sum(x, su2) for x in sp]
    carry = carry_ref[...]
    acc = acc_ref[...]
    for x, p, w, vp in zip(z, sp, later, page_refs[ppb:]):
        a = jnp.exp((x - p) - (w + carry)).astype(BF16)
        acc = acc + weighted(a, vp)
        carry = carry + jnp.sum(p, axis=-1, keepdims=True)
    acc_ref[...] = acc
    carry_ref[...] = carry

    @pl.when(j == pl.num_programs(1) - 1)
    def _():
        acc = acc_ref[...]
        y_ref[...] = jnp.concatenate([acc[h * lq:(h + 1) * lq, :] for h in range(H_B)], axis=1)


def sb_sample(proj, sb_bias, cache_k, cache_v, page_table, layer, bsz, length):
    dh = SEC // H_B
    n_pages = page_table.shape[1]
    page = cache_k.shape[2]
    rows = H_B * length
    bias_rows = jnp.repeat(sb_bias.astype(F32), length).reshape(rows, 1)

    def new_sec(s):
        return pl.BlockSpec((length, SEC), lambda b, j, pt, s=s: (b, s))

    ppb = SB_PAGES_PER_STEP
    while n_pages % ppb:
        ppb -= 1

    def past(r):
        return pl.BlockSpec((None, None, page * H_B, dh),
                            lambda b, j, pt, r=r: (layer, pt[b, n_pages - 1 - (j * ppb + r)], 0, 0))

    page_specs = [past(r) for r in range(ppb)]
    grid_spec = pltpu.PrefetchScalarGridSpec(
        num_scalar_prefetch=1,
        grid=(bsz, n_pages // ppb),
        in_specs=[new_sec(QB), new_sec(KB), new_sec(VB),
                  pl.BlockSpec((rows, 1), lambda b, j, pt: (0, 0))] + page_specs + page_specs,
        out_specs=pl.BlockSpec((length, SEC), lambda b, j, pt: (b, 0)),
        scratch_shapes=[pltpu.VMEM((rows, dh), BF16), pltpu.VMEM((rows, dh), F32),
                        pltpu.VMEM((rows, 1), F32)],
    )
    k2d = cache_k.reshape(cache_k.shape[0], cache_k.shape[1], page * H_B, dh)
    v2d = cache_v.reshape(cache_v.shape[0], cache_v.shape[1], page * H_B, dh)
    return pl.pallas_call(
        functools.partial(_sb_sample_body, lq=length, dh=dh, page=page, ppb=ppb),
        out_shape=jax.ShapeDtypeStruct((bsz * length, SEC), F32),
        grid_spec=grid_spec,
        compiler_params=_params(("parallel", "arbitrary"), 40),
        name="sb_sample",
    )(page_table, proj, proj, proj, bias_rows, *([k2d] * ppb), *([v2d] * ppb))


def _conv_body(a_ref, g_ref, buf_ref, w_ref, b_ref, lg_ref, lb_ref, y_ref, new_ref, ext_ref, *, tl):
    hist = CONV_W - 1
    pad = 32 - hist
    step = pl.program_id(1)

    @pl.when(step == 0)
    def _():
        ext_ref[0:SUBLANES, :] = jnp.zeros((SUBLANES, SEC), F32)
        ext_ref[pad:32, :] = buf_ref[0]

    ext_ref[32:32 + tl, :] = a_ref[...] * jax.nn.sigmoid(g_ref[...])
    cols = []
    for c in range(SEC // LANES):
        lanes = slice(c * LANES, (c + 1) * LANES)
        slab = ext_ref[:, lanes]
        nrow = slab.shape[0]
        acc = jnp.zeros((tl, LANES), F32)
        for phase in range(SUBLANES):
            taps = [j for j in range(CONV_W) if (pad + j) % SUBLANES == phase]
            shifted = slab if phase == 0 else pltpu.roll(slab, nrow - phase, axis=0)
            for j in taps:
                start = pad + j - phase
                acc = acc + w_ref[j:j + 1, lanes] * shifted[start:start + tl, :]
        cols.append(acc)
    y = jnp.concatenate(cols, axis=1) + b_ref[...]
    yc = y - jnp.mean(y, axis=-1, keepdims=True)
    yn = yc * lax.rsqrt(jnp.mean(yc * yc, axis=-1, keepdims=True) + EPS)
    yn = yn * lg_ref[...] + lb_ref[...]
    y_ref[...] = yn * jax.nn.sigmoid(yn)
    new_ref[0] = ext_ref[tl + pad:tl + 32, :]
    if tl >= 32:
        ext_ref[0:32, :] = ext_ref[tl:tl + 32, :]


def conformer_conv(proj, buf, w, b, ln_g, ln_b, bsz, length):
    tl = _row_tile(length, CONV_TILE, SUBLANES)
    nt = length // tl
    assert nt == 1 or tl >= 32
    hist = CONV_W - 1
    vec = pl.BlockSpec((1, SEC), lambda bb, t: (0, 0))
    return pl.pallas_call(
        functools.partial(_conv_body, tl=tl),
        out_shape=(jax.ShapeDtypeStruct((bsz * length, SEC), F32),
                   jax.ShapeDtypeStruct((bsz, hist, SEC), F32)),
        grid=(bsz, nt),
        in_specs=[pl.BlockSpec((tl, SEC), lambda bb, t: (bb * nt + t, CA)),
                  pl.BlockSpec((tl, SEC), lambda bb, t: (bb * nt + t, CG)),
                  pl.BlockSpec((1, hist, SEC), lambda bb, t: (bb, 0, 0)),
                  pl.BlockSpec((CONV_W, SEC), lambda bb, t: (0, 0)),
                  vec, vec, vec],
        out_specs=(pl.BlockSpec((tl, SEC), lambda bb, t: (bb * nt + t, 0)),
                   pl.BlockSpec((1, hist, SEC), lambda bb, t: (bb, 0, 0))),
        scratch_shapes=[pltpu.VMEM((32 + tl, SEC), F32)],
        compiler_params=_params(("parallel", "arbitrary"), 32),
        name="conformer_conv",
    )(proj, proj, buf, w, b.reshape(1, SEC), ln_g.reshape(1, SEC), ln_b.reshape(1, SEC))


def _s5_drive(u_ref, br_ref, bi_ref, sr_ref, si_ref, perm):
    nblk, cin, nst = br_ref.shape
    ub = u_ref[...].astype(BF16)
    if perm is not None:
        ub = _dot(perm, ub).astype(BF16)
    for blk in range(nblk):
        ublk = ub[:, blk * cin:(blk + 1) * cin]
        sr_ref[:, blk * nst:(blk + 1) * nst] = _dot(ublk, br_ref[blk])
        si_ref[:, blk * nst:(blk + 1) * nst] = _dot(ublk, bi_ref[blk])


def _s5_readout(sr_ref, si_ref, cr_ref, ci_ref):
    nblk, nst, _ = cr_ref.shape
    ys = []
    for blk in range(nblk):
        lanes = slice(blk * nst, (blk + 1) * nst)
        ys.append(_dot(sr_ref[:, lanes].astype(BF16), cr_ref[blk])
                  - _dot(si_ref[:, lanes].astype(BF16), ci_ref[blk]))
    return jnp.concatenate(ys, axis=1)


def _s5_gate(y, gw_ref, gb_ref, y_ref):
    g = jax.nn.gelu(y)
    gate = _dot(g.astype(BF16), gw_ref[...]) + gb_ref[...]
    y_ref[...] = g * jax.nn.sigmoid(gate)


def _s5_segmented_body(u_ref, h0r_ref, h0i_ref, ar_ref, ai_ref, br_ref, bi_ref, cr_ref, ci_ref, d_ref,
                       gw_ref, gb_ref, y_ref, hr_ref, hi_ref, sr_ref, si_ref, pr_ref, pi_ref,
                       yp_ref, yt_ref, *, tl):
    step = pl.program_id(1)
    nseg = SUBLANES
    slen = tl // nseg
    nblk = br_ref.shape[0]
    nst = br_ref.shape[2]

    @pl.when(step == 0)
    def _():
        hr_ref[...] = h0r_ref[...]
        hi_ref[...] = h0i_ref[...]
        for blk in range(nblk):
            lanes = slice(blk * nst, (blk + 1) * nst)
            ar = ar_ref[:, lanes]
            ai = ai_ref[:, lanes]
            qr, qi = ar, ai
            for t in range(slen):
                pr_ref[t:t + 1, lanes] = qr
                pi_ref[t:t + 1, lanes] = qi
                qr, qi = qr * ar - qi * ai, qr * ai + qi * ar

    dst = _iota2((tl, tl), 0)
    src = (dst % nseg) * slen + dst // nseg
    perm = jnp.where(_iota2((tl, tl), 1) == src, 1.0, 0.0).astype(BF16)
    _s5_drive(u_ref, br_ref, bi_ref, sr_ref, si_ref, perm)

    first = _iota2((nseg, nst), 0) == 0
    for blk in range(nblk):
        lanes = slice(blk * nst, (blk + 1) * nst)
        ar = ar_ref[:, lanes]
        ai = ai_ref[:, lanes]

        def local_step(t, state, lanes=lanes, ar=ar, ai=ai):
            hr, hi = state
            rows = pl.ds(pl.multiple_of(t * nseg, nseg), nseg)
            nr = ar * hr - ai * hi + sr_ref[rows, lanes]
            ni = ar * hi + ai * hr + si_ref[rows, lanes]
            sr_ref[rows, lanes] = nr
            si_ref[rows, lanes] = ni
            return nr, ni

        start = (jnp.where(first, hr_ref[0, :, lanes], 0.0), jnp.where(first, hi_ref[0, :, lanes], 0.0))
        er, ei = lax.fori_loop(0, slen, local_step, start, unroll=4)

        wr = pr_ref[slen - 1:slen, lanes]
        wi = pi_ref[slen - 1:slen, lanes]
        tr, ti = er[0:1, :], ei[0:1, :]
        starts_r = [jnp.zeros_like(tr)]
        starts_i = [jnp.zeros_like(ti)]
        for s in range(1, nseg):
            starts_r.append(tr)
            starts_i.append(ti)
            tr, ti = er[s:s + 1, :] + wr * tr - wi * ti, ei[s:s + 1, :] + wr * ti + wi * tr
        hr_ref[0, :, lanes] = tr
        hi_ref[0, :, lanes] = ti
        gr = jnp.concatenate(starts_r, axis=0)
        gi = jnp.concatenate(starts_i, axis=0)

        def fix_step(t, carry, lanes=lanes, gr=gr, gi=gi):
            rows = pl.ds(pl.multiple_of(t * nseg, nseg), nseg)
            qr = pr_ref[pl.ds(t, 1), lanes]
            qi = pi_ref[pl.ds(t, 1), lanes]
            sr_ref[rows, lanes] += qr * gr - qi * gi
            si_ref[rows, lanes] += qr * gi + qi * gr
            return carry

        lax.fori_loop(0, slen, fix_step, 0, unroll=4)

    yp = _s5_readout(sr_ref, si_ref, cr_ref, ci_ref)
    for c in range(SEC // LANES):
        yp_ref[c] = yp[:, c * LANES:(c + 1) * LANES]
    for c in range(SEC // LANES):
        for s in range(nseg):
            yt_ref[s * slen:(s + 1) * slen, c * LANES:(c + 1) * LANES] = (
                yp_ref[c, pl.ds(s, slen, stride=nseg), :])
    _s5_gate(yt_ref[...] + d_ref[...] * u_ref[...], gw_ref, gb_ref, y_ref)


def _s5_body(u_ref, h0r_ref, h0i_ref, ar_ref, ai_ref, br_ref, bi_ref, cr_ref, ci_ref, d_ref,
             gw_ref, gb_ref, y_ref, hr_ref, hi_ref, sr_ref, si_ref, *, tl):
    step = pl.program_id(1)
    nblk = br_ref.shape[0]
    nst = br_ref.shape[2]

    @pl.when(step == 0)
    def _():
        hr_ref[...] = h0r_ref[...]
        hi_ref[...] = h0i_ref[...]

    _s5_drive(u_ref, br_ref, bi_ref, sr_ref, si_ref, None)

    for blk in range(nblk):
        lanes = slice(blk * nst, (blk + 1) * nst)
        ar = ar_ref[:, lanes]
        ai = ai_ref[:, lanes]

        def scan_step(t, state, lanes=lanes, ar=ar, ai=ai):
            hr, hi = state
            row = pl.ds(t, 1)
            nr = ar * hr - ai * hi + sr_ref[row, lanes]
            ni = ar * hi + ai * hr + si_ref[row, lanes]
            sr_ref[row, lanes] = nr
            si_ref[row, lanes] = ni
            return nr, ni

        hr, hi = lax.fori_loop(0, tl, scan_step, (hr_ref[0, :, lanes], hi_ref[0, :, lanes]),
                               unroll=SUBLANES)
        hr_ref[0, :, lanes] = hr
        hi_ref[0, :, lanes] = hi

    y = _s5_readout(sr_ref, si_ref, cr_ref, ci_ref) + d_ref[...] * u_ref[...]
    _s5_gate(y, gw_ref, gb_ref, y_ref)


def s5_discretize(a_re, a_im, log_dt, b_re, b_im, c_re, c_im):
    ngrp, nst = a_re.shape
    dt = jnp.exp(log_dt.astype(F32))[:, None]
    ar = a_re.astype(F32)
    ai = a_im.astype(F32)
    mag = jnp.exp(ar * dt)
    abar_re = mag * jnp.cos(ai * dt)
    abar_im = mag * jnp.sin(ai * dt)
    zr = abar_re - 1.0
    zi = abar_im
    den = ar * ar + ai * ai
    coef_re = (zr * ar + zi * ai) / den
    coef_im = (zi * ar - zr * ai) / den
    br = b_re.astype(F32)
    bi = b_im.astype(F32)
    bbar_re = coef_re[..., None] * br - coef_im[..., None] * bi
    bbar_im = coef_re[..., None] * bi + coef_im[..., None] * br
    gb = S5_GROUP_BLOCK
    nblk = ngrp // gb
    eye = jnp.eye(gb, dtype=F32)

    def in_blocks(bbar):
        t = bbar.reshape(nblk, gb, nst, S5_CH)
        return jnp.einsum('bgnc,gh->bgchn', t, eye).reshape(nblk, gb * S5_CH, gb * nst).astype(BF16)

    def out_blocks(c):
        t = c.astype(F32).reshape(nblk, gb, S5_CH, nst)
        return jnp.einsum('bgcn,gh->bgnhc', t, eye).reshape(nblk, gb * nst, gb * S5_CH).astype(BF16)

    return (abar_re.reshape(1, ngrp * nst), abar_im.reshape(1, ngrp * nst),
            in_blocks(bbar_re), in_blocks(bbar_im), out_blocks(c_re), out_blocks(c_im))


def s5(proj, h0_re, h0_im, disc, d, gate_w, gate_b, bsz, length):
    abar_re, abar_im, b_re, b_im, c_re, c_im = disc
    nstate = abar_re.shape[1]
    tl = _row_tile(length, S5_TILE, SUBLANES)
    nt = length // tl
    state = pl.BlockSpec((1, 1, nstate), lambda b, t: (b, 0, 0))

    def whole(a):
        return pl.BlockSpec(a.shape, lambda b, t, nd=a.ndim: (0,) * nd)

    d2 = d.reshape(1, SEC)
    gb2 = gate_b.reshape(1, SEC)
    scratch = [pltpu.VMEM((tl, nstate), F32), pltpu.VMEM((tl, nstate), F32)]
    body = _s5_body
    if tl % (SUBLANES * SUBLANES) == 0:
        body = _s5_segmented_body
        slen = tl // SUBLANES
        scratch += [pltpu.VMEM((slen, nstate), F32), pltpu.VMEM((slen, nstate), F32),
                    pltpu.VMEM((SEC // LANES, tl, LANES), F32), pltpu.VMEM((tl, SEC), F32)]
    y, hr, hi = pl.pallas_call(
        functools.partial(body, tl=tl),
        out_shape=(jax.ShapeDtypeStruct((bsz * length, SEC), F32),
                   jax.ShapeDtypeStruct((bsz, 1, nstate), F32),
                   jax.ShapeDtypeStruct((bsz, 1, nstate), F32)),
        grid=(bsz, nt),
        in_specs=[pl.BlockSpec((tl, SEC), lambda b, t: (b * nt + t, UD)), state, state,
                  whole(abar_re), whole(abar_im), whole(b_re), whole(b_im), whole(c_re), whole(c_im),
                  whole(d2), whole(gate_w), whole(gb2)],
        out_specs=(pl.BlockSpec((tl, SEC), lambda b, t: (b * nt + t, 0)), state, state),
        scratch_shapes=scratch,
        compiler_params=_params(("parallel", "arbitrary"), 48),
        name="s5",
    )(proj, h0_re.reshape(bsz, 1, nstate), h0_im.reshape(bsz, 1, nstate),
      abar_re, abar_im, b_re, b_im, c_re, c_im, d2, gate_w, gb2)
    return y, hr, hi


def _cross_body(q_ref, k_ref, v_ref, o_ref, *, dh):
    scale = dh ** -0.5
    for h in range(H_X):
        sl = slice(h * dh, (h + 1) * dh)
        s = _dot_nt(q_ref[:, sl].astype(BF16), k_ref[:, sl].astype(BF16)) * scale
        e = jnp.exp(s - jnp.max(s, axis=-1, keepdims=True))
        p = e / jnp.sum(e, axis=-1, keepdims=True)
        o_ref[:, sl] = _dot(p.astype(BF16), v_ref[:, sl].astype(BF16)).astype(o_ref.dtype)


def cross_attend(q, mem_k, mem_v, k_col, v_col, bsz, length, n_mem):
    wx = q.shape[1]
    tq = _row_tile(length, 512, SUBLANES)
    nq = length // tq
    return pl.pallas_call(
        functools.partial(_cross_body, dh=wx // H_X),
        out_shape=jax.ShapeDtypeStruct((bsz * length, wx), F32),
        grid=(bsz, nq),
        in_specs=[pl.BlockSpec((tq, wx), lambda b, i: (b * nq + i, 0)),
                  pl.BlockSpec((n_mem, wx), lambda b, i: (b, k_col)),
                  pl.BlockSpec((n_mem, wx), lambda b, i: (b, v_col))],
        out_specs=pl.BlockSpec((tq, wx), lambda b, i: (b * nq + i, 0)),
        compiler_params=_params(("parallel", "arbitrary"), 32),
        name="cross_attend",
    )(q, mem_k, mem_v)


def _decoder_layer(x, bsz, length, mem_k, mem_v, k_col, v_col, n_mem, sb_fn, mlstm_state, conv_buf,
                   s5_state, p, kv_stacked, layer, depth):
    h, gates = rmsnorm_gates(x, p['norm_mix_pre'], p['w_gates'])
    proj = matmul(h, p['w_in'], F32)
    y_a, c_new, n_new, m_new = mlstm(proj, gates, p['gate_bias'], *mlstm_state, bsz, length)
    y_b = sb_fn(proj)
    y_c, conv_new = conformer_conv(proj, conv_buf, p['conv_w'], p['conv_b'], p['conv_ln_g'],
                                   p['conv_ln_b'], bsz, length)
    y_d, s5_re, s5_im = s5(proj, s5_state[0], s5_state[1], p['s5_disc'], p['s5_d'], p['s5_gate_w'],
                           p['s5_gate_b'], bsz, length)
    x, hx = mix_out_proj(y_a, y_b, y_c, y_d, p['g_group'], p['w_out'], p['norm_mix_post'], x,
                         p['norm_x_pre'])
    q = matmul(hx, p['w_xq'], F32)
    o = cross_attend(q, mem_k, mem_v, k_col, v_col, bsz, length, n_mem)
    x, hf = matmul_norm_res(o, p['w_xo'], p['norm_x_post'], x, p['norm_ffn_pre'])
    kv_stacked = (head_split(proj, KB, H_B, kv_stacked[0], layer, depth),
                  head_split(proj, VB, H_B, kv_stacked[1], layer, depth))
    return x, hf, kv_stacked, (c_new, n_new, m_new), conv_new, (s5_re, s5_im)


def kernel(x_prompt, x_sample, cache_sb_k, cache_sb_v, page_table, state_mlstm_c, state_mlstm_n, state_mlstm_m, state_conv, state_s5_re, state_s5_im, cache_mem_k, cache_mem_v, mem_prompt, norm_mix_pre, w_in, b_mlstm_gates, sb_bias, conv_w, conv_b, conv_ln_g, conv_ln_b, s5_a_re, s5_a_im, s5_log_dt, s5_b_re, s5_b_im, s5_c_re, s5_c_im, s5_d, s5_gate_w, s5_gate_b, g_group, w_out, norm_mix_post, norm_mem, norm_x_pre, w_xq, w_xk, w_xv, w_xo, norm_x_post, norm_ffn_pre, w_up, w_down, norm_ffn_post):
    n_b, seq, d_model = x_prompt.shape
    n_db, dec_seq, _ = x_sample.shape
    depth = w_in.shape[0]
    n_mem = mem_prompt.shape[1]
    w_x = w_xq.shape[2]
    dh_a = SEC // H_A
    ngrp = s5_a_re.shape[1]
    n_gate = 2 * H_A
    gate_lo = 4 * SEC

    xp = x_prompt.reshape(n_b * seq, d_model)
    xs = x_sample.reshape(n_db * dec_seq, d_model)
    mem2d = mem_prompt.reshape(n_b * n_mem, d_model)
    w_in_t = jnp.transpose(w_in, (0, 2, 1))
    outs = [[] for _ in range(14)]
    kv_p = (None, None)
    kv_s = (None, None)
    for l in range(depth):
        w_in_main, w_in_gates = cast_w_in(w_in_t, l, gate_lo, n_gate)
        p = {
            'norm_mix_pre': norm_mix_pre[l],
            'w_in': w_in_main,
            'w_gates': w_in_gates,
            'gate_bias': jnp.pad(b_mlstm_gates[l].astype(F32), (0, LANES - n_gate)).reshape(1, LANES),
            'conv_w': conv_w[l], 'conv_b': conv_b[l], 'conv_ln_g': conv_ln_g[l], 'conv_ln_b': conv_ln_b[l],
            's5_disc': s5_discretize(s5_a_re[l], s5_a_im[l], s5_log_dt[l], s5_b_re[l], s5_b_im[l],
                                     s5_c_re[l], s5_c_im[l]),
            's5_d': s5_d[l], 's5_gate_w': s5_gate_w[l].astype(BF16), 's5_gate_b': s5_gate_b[l],
            'g_group': g_group[l], 'w_out': cast_bf16(w_out, l), 'norm_mix_post': norm_mix_post[l],
            'norm_x_pre': norm_x_pre[l], 'w_xq': w_xq[l].astype(BF16), 'w_xo': w_xo[l].astype(BF16),
            'norm_x_post': norm_x_post[l], 'norm_ffn_pre': norm_ffn_pre[l],
        }
        w_kv = jnp.concatenate([w_xk[l], w_xv[l]], axis=1).astype(BF16)
        mem_kv = matmul(rmsnorm_bf16(mem2d, norm_mem[l]), w_kv, F32, tm_cap=512)
        zero_mlstm = (jnp.zeros((n_b, H_A, dh_a, dh_a), F32), jnp.zeros((n_b, H_A, dh_a), F32),
                      jnp.zeros((n_b, H_A), F32))
        zero_s5 = (jnp.zeros((n_b, ngrp, S5_N), F32), jnp.zeros((n_b, ngrp, S5_N), F32))
        sb_s = functools.partial(sb_sample, sb_bias=sb_bias[l], cache_k=cache_sb_k, cache_v=cache_sb_v,
                                 page_table=page_table, layer=l, bsz=n_db, length=dec_seq)
        xs, hf_s, kv_s, mst_s, cnv_s, s5_s = _decoder_layer(
            xs, n_db, dec_seq, cache_mem_k[l].reshape(n_db * n_mem, w_x),
            cache_mem_v[l].reshape(n_db * n_mem, w_x), 0, 0, n_mem, sb_s,
            (state_mlstm_c[l], state_mlstm_n[l], state_mlstm_m[l]), state_conv[l],
            (state_s5_re[l], state_s5_im[l]), p, kv_s, l, depth)
        w_up_tiles, u_s = cast_up_with_rows(w_up, l, FFN_TILE, hf_s)
        w_down_bf, xs = cast_down_with_rows(w_down, l, u_s, norm_ffn_post[l], xs)
        sb_p = functools.partial(sb_prompt, sb_bias=sb_bias[l], bsz=n_b, length=seq)
        xp, hf_p, kv_p, mst_p, cnv_p, s5_p = _decoder_layer(
            xp, n_b, seq, mem_kv, mem_kv, 0, 1, n_mem, sb_p, zero_mlstm,
            jnp.zeros((n_b, CONV_W - 1, SEC), F32), zero_s5, p, kv_p, l, depth)
        xp = ffn(hf_p, w_up_tiles, w_down_bf, norm_ffn_post[l], xp)
        mk = mem_kv[:, :w_x].reshape(n_b, n_mem, H_X, w_x // H_X)
        mv = mem_kv[:, w_x:].reshape(n_b, n_mem, H_X, w_x // H_X)
        s5_shape = lambda a: a.reshape(a.shape[0], ngrp, S5_N)
        layer_out = (mst_p[0], mst_p[1], mst_p[2], mst_s[0], mst_s[1], mst_s[2],
                     cnv_p, cnv_s, s5_shape(s5_p[0]), s5_shape(s5_p[1]), s5_shape(s5_s[0]),
                     s5_shape(s5_s[1]), mk, mv)
        for acc, val in zip(outs, layer_out):
            acc.append(val)
    stacked = tuple(jnp.stack(vals, axis=0) for vals in outs)
    dh_b = SEC // H_B
    kv = tuple(a.reshape(depth, n, t, H_B, dh_b)
               for a, n, t in ((kv_p[0], n_b, seq), (kv_p[1], n_b, seq),
                               (kv_s[0], n_db, dec_seq), (kv_s[1], n_db, dec_seq)))
    return (xp.reshape(n_b, seq, d_model), xs.reshape(n_db, dec_seq, d_model)) + kv + stacked
```

```python
import functools

import jax
import jax.numpy as jnp
from jax import lax
from jax.experimental import pallas as pl
from jax.experimental.pallas import tpu as pltpu

F32 = jnp.float32
BF16 = jnp.bfloat16
EPS = 1e-6

H_A = 4
H_B = 8
CONV_W = 31
S5_CH = 16
S5_N = 64
H_X = 4
MLSTM_CHUNK_MAX = 256
SB_TILE = 256
SB_HEADS_PER_STEP = 4
SB_PAGES_PER_STEP = 16
CONV_TILE = 64
S5_TILE = 256
FFN_TILE = 1024
FFN_OUT_CHUNK = 1024
FFN_NORM_ROWS = 32
S5_GROUP_BLOCK = 16

LANES = 128
SUBLANES = 8
MIB = 1024 * 1024

SEC = 1024
QA, KA, VA, OA, QB, KB, VB, CA, CG, UD = range(10)


def _params(semantics, vmem_mib):
    return pltpu.CompilerParams(dimension_semantics=semantics, vmem_limit_bytes=vmem_mib * MIB)


def _row_tile(m, cap, mult=16):
    t = min(m, cap)
    while m % t or t % mult:
        t -= 1
    return t


def _log_sigmoid_pair(z):
    sp = jnp.log1p(jnp.exp(-jnp.abs(z)))
    return jnp.minimum(z, 0.0) - sp, -(jnp.maximum(z, 0.0) + sp)


def _split_bf16(x, terms):
    out = []
    r = x
    for _ in range(terms):
        p = r.astype(BF16)
        out.append(p)
        r = r - p.astype(F32)
    return out


def _dot(a, b):
    return jnp.dot(a, b, preferred_element_type=F32)


def _dot_nt(a, b):
    return lax.dot_general(a, b, (((1,), (1,)), ((), ())), preferred_element_type=F32)


def _dot_tn(a, b):
    return lax.dot_general(a, b, (((0,), (0,)), ((), ())), preferred_element_type=F32)


def _iota2(shape, dim):
    return lax.broadcasted_iota(jnp.int32, shape, dim)


def _rmsnorm_body(x_ref, g_ref, o_ref):
    x = x_ref[...]
    r = lax.rsqrt(jnp.mean(x * x, axis=-1, keepdims=True) + EPS)
    o_ref[...] = ((x * r) * g_ref[...]).astype(o_ref.dtype)


def rmsnorm_bf16(x, g):
    m, d = x.shape
    tr = _row_tile(m, 256)
    return pl.pallas_call(
        _rmsnorm_body,
        out_shape=jax.ShapeDtypeStruct((m, d), BF16),
        grid=(m // tr,),
        in_specs=[pl.BlockSpec((tr, d), lambda i: (i, 0)), pl.BlockSpec((1, d), lambda i: (0, 0))],
        out_specs=pl.BlockSpec((tr, d), lambda i: (i, 0)),
        compiler_params=_params(("parallel",), 32),
        name="rmsnorm",
    )(x, g.reshape(1, d))


def _rmsnorm_gates_body(x_ref, g_ref, wg_ref, o_ref, gates_ref):
    x = x_ref[...]
    r = lax.rsqrt(jnp.mean(x * x, axis=-1, keepdims=True) + EPS)
    h = ((x * r) * g_ref[...]).astype(BF16)
    o_ref[...] = h
    gates_ref[...] = _dot(h, wg_ref[...])


def rmsnorm_gates(x, g, wg):
    m, d = x.shape
    tr = _row_tile(m, 256)
    return pl.pallas_call(
        _rmsnorm_gates_body,
        out_shape=(jax.ShapeDtypeStruct((m, d), BF16), jax.ShapeDtypeStruct((m, LANES), F32)),
        grid=(m // tr,),
        in_specs=[pl.BlockSpec((tr, d), lambda i: (i, 0)), pl.BlockSpec((1, d), lambda i: (0, 0)),
                  pl.BlockSpec((d, LANES), lambda i: (0, 0))],
        out_specs=(pl.BlockSpec((tr, d), lambda i: (i, 0)), pl.BlockSpec((tr, LANES), lambda i: (i, 0))),
        compiler_params=_params(("parallel",), 32),
        name="rmsnorm_gates",
    )(x, g.reshape(1, d), wg)


def _matmul_body(a_ref, w_ref, o_ref):
    o_ref[...] = _dot(a_ref[...], w_ref[...]).astype(o_ref.dtype)


def matmul(a, w, out_dtype, tm_cap=1024, tn_cap=1024):
    m, k = a.shape
    n = w.shape[1]
    tm = _row_tile(m, tm_cap)
    tn = _row_tile(n, tn_cap, LANES)
    return pl.pallas_call(
        _matmul_body,
        out_shape=jax.ShapeDtypeStruct((m, n), out_dtype),
        grid=(m // tm, n // tn),
        in_specs=[pl.BlockSpec((tm, k), lambda i, j: (i, 0)), pl.BlockSpec((k, tn), lambda i, j: (0, j))],
        out_specs=pl.BlockSpec((tm, tn), lambda i, j: (i, j)),
        compiler_params=_params(("parallel", "arbitrary"), 56),
        name="matmul",
    )(a, w)


def _norm_residual(y, g_ref, res_ref, o_ref):
    r = lax.rsqrt(jnp.mean(y * y, axis=-1, keepdims=True) + EPS)
    o_ref[...] = res_ref[...] + (y * r) * g_ref[...]


def _norm_residual_next(y, g_ref, res_ref, g_next_ref, o_ref, h_ref):
    _norm_residual(y, g_ref, res_ref, o_ref)
    x = o_ref[...]
    r = lax.rsqrt(jnp.mean(x * x, axis=-1, keepdims=True) + EPS)
    h_ref[...] = ((x * r) * g_next_ref[...]).astype(h_ref.dtype)


def _matmul_norm_res_body(a_ref, w_ref, g_ref, res_ref, g_next_ref, o_ref, h_ref):
    _norm_residual_next(_dot(a_ref[...].astype(BF16), w_ref[...]), g_ref, res_ref, g_next_ref,
                        o_ref, h_ref)


def _mix_out_body(ya_ref, yb_ref, yc_ref, yd_ref, gg_ref, w_ref, g_ref, res_ref, g_next_ref,
                  o_ref, h_ref):
    wg = ya_ref.shape[1]
    parts = []
    for i, ref in enumerate((ya_ref, yb_ref, yc_ref, yd_ref)):
        y = ref[...]
        r = lax.rsqrt(jnp.mean(y * y, axis=-1, keepdims=True) + EPS)
        parts.append(((y * r) * gg_ref[:, i * wg:(i + 1) * wg]).astype(BF16))
    mixed = jnp.concatenate(parts, axis=1)
    _norm_residual_next(_dot(mixed, w_ref[...]), g_ref, res_ref, g_next_ref, o_ref, h_ref)


def mix_out_proj(ya, yb, yc, yd, g_group, w, g, res, g_next):
    m, wg = ya.shape
    n = w.shape[1]
    tm = _row_tile(m, 128)
    rows = pl.BlockSpec((tm, n), lambda i: (i, 0))
    part = pl.BlockSpec((tm, wg), lambda i: (i, 0))
    vec = pl.BlockSpec((1, n), lambda i: (0, 0))
    return pl.pallas_call(
        _mix_out_body,
        out_shape=(jax.ShapeDtypeStruct((m, n), F32), jax.ShapeDtypeStruct((m, n), BF16)),
        grid=(m // tm,),
        in_specs=[part, part, part, part, pl.BlockSpec((1, 4 * wg), lambda i: (0, 0)),
                  pl.BlockSpec(w.shape, lambda i: (0, 0), pipeline_mode=pl.Buffered(1)),
                  vec, rows, vec],
        out_specs=(rows, rows),
        compiler_params=_params(("parallel",), 60),
        name="mix_out_proj",
    )(ya, yb, yc, yd, g_group.reshape(1, 4 * wg), w, g.reshape(1, n), res, g_next.reshape(1, n))


def matmul_norm_res(a, w, g, res, g_next):
    m, kdim = a.shape
    n = w.shape[1]
    w_bytes = kdim * n * w.dtype.itemsize
    tm = _row_tile(m, 128 if w_bytes > 16 * MIB else 256)
    rows = pl.BlockSpec((tm, n), lambda i: (i, 0))
    vec = pl.BlockSpec((1, n), lambda i: (0, 0))
    return pl.pallas_call(
        _matmul_norm_res_body,
        out_shape=(jax.ShapeDtypeStruct((m, n), F32), jax.ShapeDtypeStruct((m, n), BF16)),
        grid=(m // tm,),
        in_specs=[pl.BlockSpec((tm, kdim), lambda i: (i, 0)),
                  pl.BlockSpec((kdim, n), lambda i: (0, 0), pipeline_mode=pl.Buffered(1)),
                  vec, rows, vec],
        out_specs=(rows, rows),
        compiler_params=_params(("parallel",), 60 if w_bytes > 16 * MIB else 40),
        name="matmul_norm_res",
    )(a, w, g.reshape(1, n), res, g_next.reshape(1, n))


def _ffn_body(h_ref, wu_ref, wd_ref, g_ref, res_ref, o_ref, *, nf):
    f = pl.program_id(1)

    @pl.when(f == 0)
    def _():
        o_ref[...] = jnp.zeros_like(o_ref)

    u = _dot(h_ref[...], wu_ref[...])
    u = jnp.square(jnp.maximum(u, 0.0)).astype(BF16)
    d = o_ref.shape[1]
    for c in range(0, d, FFN_OUT_CHUNK):
        cols = slice(c, c + FFN_OUT_CHUNK)
        o_ref[:, cols] += _dot(u, wd_ref[:, cols])

    @pl.when(f == nf - 1)
    def _():
        def rows_step(r, carry):
            rows = pl.ds(pl.multiple_of(r * FFN_NORM_ROWS, FFN_NORM_ROWS), FFN_NORM_ROWS)
            _norm_residual(o_ref[rows, :], g_ref, res_ref.at[rows, :], o_ref.at[rows, :])
            return carry

        lax.fori_loop(0, o_ref.shape[0] // FFN_NORM_ROWS, rows_step, 0)


def ffn(h, w_up_tiles, w_down, g, res, tm_cap=512):
    m, d = h.shape
    nf, _, tf = w_up_tiles.shape
    tm = _row_tile(m, tm_cap)
    once = pl.Buffered(1)
    return pl.pallas_call(
        functools.partial(_ffn_body, nf=nf),
        out_shape=jax.ShapeDtypeStruct((m, d), F32),
        grid=(m // tm, nf),
        in_specs=[pl.BlockSpec((tm, d), lambda i, f: (i, 0), pipeline_mode=once),
                  pl.BlockSpec((None, d, tf), lambda i, f: (f, 0, 0)),
                  pl.BlockSpec((tf, d), lambda i, f: (f, 0)),
                  pl.BlockSpec((1, d), lambda i, f: (0, 0)),
                  pl.BlockSpec((tm, d), lambda i, f: (i, 0), pipeline_mode=once)],
        out_specs=pl.BlockSpec((tm, d), lambda i, f: (i, 0), pipeline_mode=once),
        compiler_params=_params(("parallel", "arbitrary"), 60),
        name="ffn",
    )(h, w_up_tiles, w_down, g.reshape(1, d), res)


def _cast_body(x_ref, o_ref):
    o_ref[...] = x_ref[...].astype(o_ref.dtype)


def _cast_up_body(x_ref, hs_ref, o_ref, u_ref):
    i = pl.program_id(1)
    wb = x_ref[...].astype(BF16)
    o_ref[...] = wb
    part = _dot(hs_ref[...], wb)

    @pl.when(i == 0)
    def _():
        u_ref[...] = part

    @pl.when(i > 0)
    def _():
        u_ref[...] += part


def cast_up_with_rows(w_up, layer, tc, hs):
    _, r, c = w_up.shape
    ms = hs.shape[0]
    tr = _row_tile(r, 1024)
    return pl.pallas_call(
        _cast_up_body,
        out_shape=(jax.ShapeDtypeStruct((c // tc, r, tc), BF16), jax.ShapeDtypeStruct((ms, c), F32)),
        grid=(c // tc, r // tr),
        in_specs=[pl.BlockSpec((None, tr, tc), lambda j, i: (layer, i, j)),
                  pl.BlockSpec((ms, tr), lambda j, i: (0, i))],
        out_specs=(pl.BlockSpec((None, tr, tc), lambda j, i: (j, i, 0)),
                   pl.BlockSpec((ms, tc), lambda j, i: (0, j))),
        compiler_params=_params(("parallel", "arbitrary"), 40),
        name="cast_up_with_rows",
    )(w_up, hs)


def _cast_down_body(x_ref, u_ref, g_ref, res_ref, o_ref, y_ref, *, steps):
    i = pl.program_id(0)
    wb = x_ref[...].astype(BF16)
    o_ref[...] = wb
    u = jnp.square(jnp.maximum(u_ref[...], 0.0)).astype(BF16)
    part = _dot(u, wb)

    @pl.when(i == 0)
    def _():
        y_ref[...] = part

    @pl.when(i > 0)
    def _():
        y_ref[...] += part

    @pl.when(i == steps - 1)
    def _():
        _norm_residual(y_ref[...], g_ref, res_ref, y_ref)


def cast_down_with_rows(w_down, layer, u_pre, g, res):
    _, f, d = w_down.shape
    ms = u_pre.shape[0]
    tr = _row_tile(f, 512)
    steps = f // tr
    return pl.pallas_call(
        functools.partial(_cast_down_body, steps=steps),
        out_shape=(jax.ShapeDtypeStruct((f, d), BF16), jax.ShapeDtypeStruct((ms, d), F32)),
        grid=(steps,),
        in_specs=[pl.BlockSpec((None, tr, d), lambda i: (layer, i, 0)),
                  pl.BlockSpec((ms, tr), lambda i: (0, i)),
                  pl.BlockSpec((1, d), lambda i: (0, 0)),
                  pl.BlockSpec((ms, d), lambda i: (0, 0))],
        out_specs=(pl.BlockSpec((tr, d), lambda i: (i, 0)),
                   pl.BlockSpec((ms, d), lambda i: (0, 0))),
        compiler_params=_params(("arbitrary",), 48),
        name="cast_down_with_rows",
    )(w_down, u_pre, g.reshape(1, d), res)


def cast_bf16(x, layer):
    _, r, c = x.shape
    tr = _row_tile(r, 512)
    tc = _row_tile(c, 4096, LANES)
    return pl.pallas_call(
        _cast_body,
        out_shape=jax.ShapeDtypeStruct((r, c), BF16),
        grid=(r // tr, c // tc),
        in_specs=[pl.BlockSpec((None, tr, tc), lambda i, j: (layer, i, j))],
        out_specs=pl.BlockSpec((tr, tc), lambda i, j: (i, j)),
        compiler_params=_params(("parallel", "parallel"), 40),
        name="cast_bf16",
    )(x)


def _cast_w_in_body(x_ref, nxt_ref, o_ref, gates_ref, *, aligned_tiles, skip):
    j = pl.program_id(1)
    tc = x_ref.shape[0]

    @pl.when(j < aligned_tiles)
    def _():
        o_ref[...] = x_ref[...].T.astype(o_ref.dtype)

    @pl.when(j >= aligned_tiles)
    def _():
        wide = jnp.concatenate([x_ref[...], nxt_ref[...]], axis=0)
        o_ref[...] = wide[skip:skip + tc, :].T.astype(o_ref.dtype)

    @pl.when(j == aligned_tiles)
    def _():
        fill = jnp.zeros((LANES - skip, x_ref.shape[1]), F32)
        gates_ref[...] = jnp.concatenate([x_ref[0:skip, :], fill], axis=0).T.astype(gates_ref.dtype)


def cast_w_in(w_in_t, layer, gate_lo, n_gate):
    _, c, r = w_in_t.shape
    tc = SEC
    tr = _row_tile(r, 512)
    n_out = c - n_gate
    assert gate_lo % tc == 0 and n_out % tc == 0 and n_gate == SUBLANES
    return pl.pallas_call(
        functools.partial(_cast_w_in_body, aligned_tiles=gate_lo // tc, skip=n_gate),
        out_shape=(jax.ShapeDtypeStruct((r, n_out), BF16), jax.ShapeDtypeStruct((r, LANES), BF16)),
        grid=(r // tr, n_out // tc),
        in_specs=[pl.BlockSpec((None, tc, tr), lambda i, j: (layer, j, i)),
                  pl.BlockSpec((None, n_gate, tr), lambda i, j: (layer, (j + 1) * (tc // n_gate), i))],
        out_specs=(pl.BlockSpec((tr, tc), lambda i, j: (i, j)),
                   pl.BlockSpec((tr, LANES), lambda i, j: (i, 0))),
        compiler_params=_params(("parallel", "arbitrary"), 32),
        name="cast_w_in",
    )(w_in_t, w_in_t)


def _head_split_body(x_ref, *refs, nh, dh):
    o_ref = refs[-1]
    rows = x_ref.shape[0]
    for h in range(nh):
        o_ref[pl.ds(h, rows, stride=nh), :] = x_ref[:, h * dh:(h + 1) * dh]


def head_split(proj, section, nh, stacked, layer, depth):
    rows = proj.shape[0]
    dh = SEC // nh
    tr = _row_tile(rows, 512, SUBLANES)
    nt = rows // tr
    args = (proj,) if stacked is None else (proj, stacked)
    in_specs = [pl.BlockSpec((tr, SEC), lambda i: (i, section))]
    if stacked is not None:
        in_specs.append(pl.BlockSpec(memory_space=pl.ANY))
    return pl.pallas_call(
        functools.partial(_head_split_body, nh=nh, dh=dh),
        out_shape=jax.ShapeDtypeStruct((depth * rows * nh, dh), F32),
        grid=(nt,),
        in_specs=in_specs,
        out_specs=pl.BlockSpec((tr * nh, dh), lambda i: (layer * nt + i, 0)),
        input_output_aliases={} if stacked is None else {1: 0},
        compiler_params=_params(("parallel",), 32),
        name="head_split",
    )(*args)


def _mlstm_body(q_ref, k_ref, v_ref, o_ref, gates_ref, bias_ref, c0_ref, n0_ref, m0_ref,
                y_ref, c_ref, n_ref, m_ref, *, cs, dh):
    step = pl.program_id(1)

    @pl.when(step == 0)
    def _():
        c_ref[...] = c0_ref[...]
        n_ref[...] = n0_ref[...]
        m_ref[...] = m0_ref[...]

    g = gates_ref[...] + bias_ref[...]
    col = _iota2(g.shape, 1)
    x = jnp.where(col >= H_A, _log_sigmoid_pair(g)[0], g)
    eye = jnp.where(_iota2((LANES, LANES), 0) == _iota2((LANES, LANES), 1), 1.0, 0.0).astype(BF16)
    rr = _iota2((cs, cs), 0)
    cc = _iota2((cs, cs), 1)
    lower = jnp.where(rr >= cc, 1.0, 0.0).astype(BF16)
    upper = jnp.where(rr <= cc, 1.0, 0.0).astype(BF16)
    xs = _split_bf16(x, 3)
    x_rows = sum(_dot_nt(eye, p) for p in xs)
    fc_cols = sum(_dot(lower, p) for p in xs)
    fc_rows = sum(_dot(p, upper) for p in _split_bf16(x_rows, 3))
    causal = rr >= cc
    scale = dh ** -0.5

    for h in range(H_A):
        sl = slice(h * dh, (h + 1) * dh)
        q = q_ref[:, sl]
        k = k_ref[:, sl] * scale
        v = v_ref[:, sl]
        qb = q.astype(BF16)
        kb = k.astype(BF16)
        i_col = x[:, h:h + 1]
        i_row = x_rows[h:h + 1, :]
        f_col = fc_cols[:, H_A + h:H_A + h + 1]
        f_row = fc_rows[H_A + h:H_A + h + 1, :]
        m_prev = m_ref[0, :, h:h + 1]
        c_prev = c_ref[0, h]
        n_prev = n_ref[0, h:h + 1, :]

        dmat = jnp.where(causal, f_col - f_row + i_row, -jnp.inf)
        carry_log = f_col + m_prev
        m_t = jnp.maximum(jnp.max(dmat, axis=-1, keepdims=True), carry_log)
        s = _dot_nt(qb, kb) * jnp.exp(dmat - m_t)
        w_prev = jnp.exp(carry_log - m_t)
        num = _dot(s.astype(BF16), v.astype(BF16)) + w_prev * _dot_nt(qb, c_prev.astype(BF16))
        den = jnp.sum(s, axis=-1, keepdims=True) + w_prev * jnp.sum(q * n_prev, axis=-1, keepdims=True)
        hid = num / jnp.maximum(jnp.abs(den), jnp.exp(-m_t))
        y_ref[:, sl] = jax.nn.sigmoid(o_ref[:, sl]) * hid

        m_new = m_t[cs - 1:cs, :]
        f_last = f_col[cs - 1:cs, :]
        decay = jnp.exp(f_last + m_prev - m_new)
        w_end = jnp.exp(f_last - f_col + i_col - m_new)
        c_ref[0, h] = decay * c_prev + _dot_tn((w_end * v).astype(BF16), kb)
        n_ref[0, h:h + 1, :] = decay * n_prev + jnp.sum(w_end * k, axis=0, keepdims=True)
        m_ref[0, :, h:h + 1] = m_new


def mlstm(proj, gates, bias, c0, n0, m0, bsz, length):
    dh = SEC // H_A
    cs = _row_tile(length, MLSTM_CHUNK_MAX, SUBLANES)
    nc = length // cs
    rows = bsz * length

    def sec(s):
        return pl.BlockSpec((cs, SEC), lambda b, c, s=s: (b * nc + c, s))

    state_c = pl.BlockSpec((1, H_A, dh, dh), lambda b, c: (b, 0, 0, 0))
    state_n = pl.BlockSpec((1, H_A, dh), lambda b, c: (b, 0, 0))
    state_m = pl.BlockSpec((1, 1, H_A), lambda b, c: (b, 0, 0))
    y, c, n, m = pl.pallas_call(
        functools.partial(_mlstm_body, cs=cs, dh=dh),
        out_shape=(jax.ShapeDtypeStruct((rows, SEC), F32),
                   jax.ShapeDtypeStruct((bsz, H_A, dh, dh), F32),
                   jax.ShapeDtypeStruct((bsz, H_A, dh), F32),
                   jax.ShapeDtypeStruct((bsz, 1, H_A), F32)),
        grid=(bsz, nc),
        in_specs=[sec(QA), sec(KA), sec(VA), sec(OA),
                  pl.BlockSpec((cs, LANES), lambda b, c: (b * nc + c, 0)),
                  pl.BlockSpec((1, LANES), lambda b, c: (0, 0)),
                  state_c, state_n, state_m],
        out_specs=(pl.BlockSpec((cs, SEC), lambda b, c: (b * nc + c, 0)), state_c, state_n, state_m),
        compiler_params=_params(("parallel", "arbitrary"), 48),
        name="mlstm",
    )(proj, proj, proj, proj, gates, bias, c0, n0, m0.reshape(bsz, 1, H_A))
    return y, c, n, m.reshape(bsz, H_A)


def _softplus(z):
    return jnp.maximum(z, 0.0) + jnp.log(1.0 + jnp.exp(-jnp.abs(z)))


def _suffix_sum(sp, strict_upper2):
    hi, lo = _split_bf16(sp, 2)
    return _dot(jnp.concatenate([hi, lo], axis=1), strict_upper2)


def _strict_upper2(n):
    j = _iota2((2 * n, n), 0)
    j = jnp.where(j >= n, j - n, j)
    return jnp.where(j > _iota2((2 * n, n), 1), 1.0, 0.0).astype(BF16)


def _sb_prompt_body(bias_ref, q_ref, k_ref, v_ref, y_ref, *, t, dh, hpb):
    g = pl.program_id(1)
    i = pl.program_id(2)
    su2 = _strict_upper2(t)
    diag_mask = _iota2((t, t), 1) < _iota2((t, t), 0)
    heads = [slice(hh * dh, (hh + 1) * dh) for hh in range(hpb)]
    bias = [bias_ref[g * hpb + hh] for hh in range(hpb)]
    q = [(q_ref[:, sl] * (dh ** -0.5)).astype(BF16) for sl in heads]

    def tile(j, state, mask):
        rows = pl.ds(pl.multiple_of(j * t, t), t)
        z = [_dot_nt(q[hh], k_ref[rows, sl].astype(BF16)) + bias[hh] for hh, sl in enumerate(heads)]
        sp = [_softplus(x) if mask is None else jnp.where(mask, _softplus(x), 0.0) for x in z]
        later = [_suffix_sum(x, su2) + state[hh][0] for hh, x in enumerate(sp)]
        a = [jnp.exp((x - p) - w) for x, p, w in zip(z, sp, later)]
        if mask is not None:
            a = [jnp.where(mask, x, 0.0) for x in a]
        acc = [state[hh][1] + _dot(a[hh].astype(BF16), v_ref[rows, sl].astype(BF16))
               for hh, sl in enumerate(heads)]
        carry = [state[hh][0] + jnp.sum(sp[hh], axis=-1, keepdims=True) for hh in range(hpb)]
        return tuple(zip(carry, acc))

    zero = tuple((jnp.zeros((t, 1), F32), jnp.zeros((t, dh), F32)) for _ in heads)
    state = tile(i, zero, diag_mask)
    state = lax.fori_loop(0, i, lambda n, st: tile(i - 1 - n, st, None), state)
    for hh, sl in enumerate(heads):
        y_ref[:, sl] = state[hh][1]


def sb_prompt(proj, sb_bias, bsz, length):
    dh = SEC // H_B
    hpb = SB_HEADS_PER_STEP
    wide = hpb * dh
    t = _row_tile(length, SB_TILE, SUBLANES)
    nq = length // t
    cpb = SEC // wide
    return pl.pallas_call(
        functools.partial(_sb_prompt_body, t=t, dh=dh, hpb=hpb),
        out_shape=jax.ShapeDtypeStruct((bsz * length, SEC), F32),
        grid=(bsz, H_B // hpb, nq),
        in_specs=[pl.BlockSpec(memory_space=pltpu.SMEM),
                  pl.BlockSpec((t, wide), lambda b, g, i: (b * nq + i, QB * cpb + g)),
                  pl.BlockSpec((length, wide), lambda b, g, i: (b, KB * cpb + g)),
                  pl.BlockSpec((length, wide), lambda b, g, i: (b, VB * cpb + g))],
        out_specs=pl.BlockSpec((t, wide), lambda b, g, i: (b * nq + i, g)),
        compiler_params=_params(("parallel", "parallel", "arbitrary"), 40),
        name="sb_prompt",
    )(sb_bias, proj, proj, proj)


def _sb_sample_body(pt_ref, q_ref, kn_ref, vn_ref, bias_ref, *refs, lq, dh, page, ppb):
    del pt_ref
    page_refs = refs[:2 * ppb]
    y_ref, q2_ref, acc_ref, carry_ref = refs[2 * ppb:]
    j = pl.program_id(1)
    rows = H_B * lq
    bias = bias_ref[...]

    @pl.when(j == 0)
    def _():
        q = q_ref[...] * (dh ** -0.5)
        q2 = jnp.concatenate([q[:, h * dh:(h + 1) * dh] for h in range(H_B)], axis=0).astype(BF16)
        q2_ref[...] = q2
        q_wide = jnp.concatenate([q2] * H_B, axis=1)
        same_head = (_iota2((rows, SEC), 0) // lq) == (_iota2((rows, SEC), 1) // dh)
        q_bd = jnp.where(same_head, q_wide, jnp.zeros_like(q_wide))
        fill = jnp.zeros((LANES - lq, SEC), F32)
        kn = jnp.concatenate([kn_ref[...], fill], axis=0).astype(BF16)
        vn = jnp.concatenate([vn_ref[...], fill], axis=0).astype(BF16)
        z = _dot_nt(q_bd, kn) + bias
        mask = _iota2((rows, LANES), 1) < (_iota2((rows, LANES), 0) % lq)
        sp = jnp.where(mask, _softplus(z), 0.0)
        a = jnp.where(mask, jnp.exp((z - sp) - _suffix_sum(sp, _strict_upper2(LANES))), 0.0)
        full = _dot(a.astype(BF16), vn)
        acc_ref[...] = jnp.concatenate(
            [full[h * lq:(h + 1) * lq, h * dh:(h + 1) * dh] for h in range(H_B)], axis=0)
        carry_ref[...] = jnp.sum(sp, axis=-1, keepdims=True)

    q2 = q2_ref[...]
    su2 = _strict_upper2(page)
    def head_rows(ref, h):
        return ref[pl.ds(h, page, stride=H_B), :].astype(BF16)

    def own_rows(per_pair, n):
        return jnp.concatenate(
            [x[h * lq:(h + 1) * lq, (h % 2) * n:(h % 2 + 1) * n]
             for h, x in ((h, per_pair[h // 2]) for h in range(H_B))], axis=0)

    def scores(kp):
        pairs = [jnp.concatenate([head_rows(kp, h), head_rows(kp, h + 1)], axis=0)
                 for h in range(0, H_B, 2)]
        return own_rows([_dot_nt(q2, kk) for kk in pairs], page)

    def weighted(a, vp):
        pairs = [jnp.concatenate([head_rows(vp, h), head_rows(vp, h + 1)], axis=1)
                 for h in range(0, H_B, 2)]
        return own_rows([_dot(a, vv) for vv in pairs], dh)

    z = [scores(kp) + bias for kp in page_refs[:ppb]]
    sp = [_softplus(x) for x in z]
    later = [_suffix_sum(x, su2) for x in sp]
    carry = carry_ref[...]
    acc = acc_ref[...]
    for x, p, w, vp in zip(z, sp, later, page_refs[ppb:]):
        a = jnp.exp((x - p) - (w + carry)).astype(BF16)
        acc = acc + weighted(a, vp)
        carry = carry + jnp.sum(p, axis=-1, keepdims=True)
    acc_ref[...] = acc
    carry_ref[...] = carry

    @pl.when(j == pl.num_programs(1) - 1)
    def _():
        acc = acc_ref[...]
        y_ref[...] = jnp.concatenate([acc[h * lq:(h + 1) * lq, :] for h in range(H_B)], axis=1)


def sb_sample(proj, sb_bias, cache_k, cache_v, page_table, layer, bsz, length):
    dh = SEC // H_B
    n_pages = page_table.shape[1]
    page = cache_k.shape[2]
    rows = H_B * length
    bias_rows = jnp.repeat(sb_bias.astype(F32), length).reshape(rows, 1)

    def new_sec(s):
        return pl.BlockSpec((length, SEC), lambda b, j, pt, s=s: (b, s))

    ppb = SB_PAGES_PER_STEP
    while n_pages % ppb:
        ppb -= 1

    def past(r):
        return pl.BlockSpec((None, None, page * H_B, dh),
                            lambda b, j, pt, r=r: (layer, pt[b, n_pages - 1 - (j * ppb + r)], 0, 0))

    page_specs = [past(r) for r in range(ppb)]
    grid_spec = pltpu.PrefetchScalarGridSpec(
        num_scalar_prefetch=1,
        grid=(bsz, n_pages // ppb),
        in_specs=[new_sec(QB), new_sec(KB), new_sec(VB),
                  pl.BlockSpec((rows, 1), lambda b, j, pt: (0, 0))] + page_specs + page_specs,
        out_specs=pl.BlockSpec((length, SEC), lambda b, j, pt: (b, 0)),
        scratch_shapes=[pltpu.VMEM((rows, dh), BF16), pltpu.VMEM((rows, dh), F32),
                        pltpu.VMEM((rows, 1), F32)],
    )
    k2d = cache_k.reshape(cache_k.shape[0], cache_k.shape[1], page * H_B, dh)
    v2d = cache_v.reshape(cache_v.shape[0], cache_v.shape[1], page * H_B, dh)
    return pl.pallas_call(
        functools.partial(_sb_sample_body, lq=length, dh=dh, page=page, ppb=ppb),
        out_shape=jax.ShapeDtypeStruct((bsz * length, SEC), F32),
        grid_spec=grid_spec,
        compiler_params=_params(("parallel", "arbitrary"), 56),
        name="sb_sample",
    )(page_table, proj, proj, proj, bias_rows, *([k2d] * ppb), *([v2d] * ppb))


def _conv_body(a_ref, g_ref, buf_ref, w_ref, b_ref, lg_ref, lb_ref, y_ref, new_ref, ext_ref, *, tl):
    hist = CONV_W - 1
    pad = 32 - hist
    step = pl.program_id(1)

    @pl.when(step == 0)
    def _():
        ext_ref[0:SUBLANES, :] = jnp.zeros((SUBLANES, SEC), F32)
        ext_ref[pad:32, :] = buf_ref[0]

    ext_ref[32:32 + tl, :] = a_ref[...] * jax.nn.sigmoid(g_ref[...])
    cols = []
    for c in range(SEC // LANES):
        lanes = slice(c * LANES, (c + 1) * LANES)
        slab = ext_ref[:, lanes]
        nrow = slab.shape[0]
        acc = jnp.zeros((tl, LANES), F32)
        for phase in range(SUBLANES):
            taps = [j for j in range(CONV_W) if (pad + j) % SUBLANES == phase]
            shifted = slab if phase == 0 else pltpu.roll(slab, nrow - phase, axis=0)
            for j in taps:
                start = pad + j - phase
                acc = acc + w_ref[j:j + 1, lanes] * shifted[start:start + tl, :]
        cols.append(acc)
    y = jnp.concatenate(cols, axis=1) + b_ref[...]
    yc = y - jnp.mean(y, axis=-1, keepdims=True)
    yn = yc * lax.rsqrt(jnp.mean(yc * yc, axis=-1, keepdims=True) + EPS)
    yn = yn * lg_ref[...] + lb_ref[...]
    y_ref[...] = yn * jax.nn.sigmoid(yn)
    new_ref[0] = ext_ref[tl + pad:tl + 32, :]
    if tl >= 32:
        ext_ref[0:32, :] = ext_ref[tl:tl + 32, :]


def conformer_conv(proj, buf, w, b, ln_g, ln_b, bsz, length):
    tl = _row_tile(length, CONV_TILE, SUBLANES)
    nt = length // tl
    assert nt == 1 or tl >= 32
    hist = CONV_W - 1
    vec = pl.BlockSpec((1, SEC), lambda bb, t: (0, 0))
    return pl.pallas_call(
        functools.partial(_conv_body, tl=tl),
        out_shape=(jax.ShapeDtypeStruct((bsz * length, SEC), F32),
                   jax.ShapeDtypeStruct((bsz, hist, SEC), F32)),
        grid=(bsz, nt),
        in_specs=[pl.BlockSpec((tl, SEC), lambda bb, t: (bb * nt + t, CA)),
                  pl.BlockSpec((tl, SEC), lambda bb, t: (bb * nt + t, CG)),
                  pl.BlockSpec((1, hist, SEC), lambda bb, t: (bb, 0, 0)),
                  pl.BlockSpec((CONV_W, SEC), lambda bb, t: (0, 0)),
                  vec, vec, vec],
        out_specs=(pl.BlockSpec((tl, SEC), lambda bb, t: (bb * nt + t, 0)),
                   pl.BlockSpec((1, hist, SEC), lambda bb, t: (bb, 0, 0))),
        scratch_shapes=[pltpu.VMEM((32 + tl, SEC), F32)],
        compiler_params=_params(("parallel", "arbitrary"), 32),
        name="conformer_conv",
    )(proj, proj, buf, w, b.reshape(1, SEC), ln_g.reshape(1, SEC), ln_b.reshape(1, SEC))


def _s5_drive(u_ref, br_ref, bi_ref, sr_ref, si_ref, perm):
    nblk, cin, nst = br_ref.shape
    ub = u_ref[...].astype(BF16)
    if perm is not None:
        ub = _dot(perm, ub).astype(BF16)
    for blk in range(nblk):
        ublk = ub[:, blk * cin:(blk + 1) * cin]
        sr_ref[:, blk * nst:(blk + 1) * nst] = _dot(ublk, br_ref[blk])
        si_ref[:, blk * nst:(blk + 1) * nst] = _dot(ublk, bi_ref[blk])


def _s5_readout(sr_ref, si_ref, cr_ref, ci_ref):
    nblk, nst, _ = cr_ref.shape
    ys = []
    for blk in range(nblk):
        lanes = slice(blk * nst, (blk + 1) * nst)
        ys.append(_dot(sr_ref[:, lanes].astype(BF16), cr_ref[blk])
                  - _dot(si_ref[:, lanes].astype(BF16), ci_ref[blk]))
    return jnp.concatenate(ys, axis=1)


def _s5_gate(y, gw_ref, gb_ref, y_ref):
    g = jax.nn.gelu(y)
    gate = _dot(g.astype(BF16), gw_ref[...]) + gb_ref[...]
    y_ref[...] = g * jax.nn.sigmoid(gate)


def _s5_segmented_body(u_ref, h0r_ref, h0i_ref, ar_ref, ai_ref, br_ref, bi_ref, cr_ref, ci_ref, d_ref,
                       gw_ref, gb_ref, y_ref, hr_ref, hi_ref, sr_ref, si_ref, pr_ref, pi_ref,
                       yp_ref, yt_ref, *, tl):
    step = pl.program_id(1)
    nseg = SUBLANES
    slen = tl // nseg
    nblk = br_ref.shape[0]
    nst = br_ref.shape[2]

    @pl.when(step == 0)
    def _():
        hr_ref[...] = h0r_ref[...]
        hi_ref[...] = h0i_ref[...]
        for blk in range(nblk):
            lanes = slice(blk * nst, (blk + 1) * nst)
            ar = ar_ref[:, lanes]
            ai = ai_ref[:, lanes]
            qr, qi = ar, ai
            for t in range(slen):
                pr_ref[t:t + 1, lanes] = qr
                pi_ref[t:t + 1, lanes] = qi
                qr, qi = qr * ar - qi * ai, qr * ai + qi * ar

    dst = _iota2((tl, tl), 0)
    src = (dst % nseg) * slen + dst // nseg
    perm = jnp.where(_iota2((tl, tl), 1) == src, 1.0, 0.0).astype(BF16)
    _s5_drive(u_ref, br_ref, bi_ref, sr_ref, si_ref, perm)

    first = _iota2((nseg, nst), 0) == 0
    for blk in range(nblk):
        lanes = slice(blk * nst, (blk + 1) * nst)
        ar = ar_ref[:, lanes]
        ai = ai_ref[:, lanes]

        def local_step(t, state, lanes=lanes, ar=ar, ai=ai):
            hr, hi = state
            rows = pl.ds(pl.multiple_of(t * nseg, nseg), nseg)
            nr = ar * hr - ai * hi + sr_ref[rows, lanes]
            ni = ar * hi + ai * hr + si_ref[rows, lanes]
            sr_ref[rows, lanes] = nr
            si_ref[rows, lanes] = ni
            return nr, ni

        start = (jnp.where(first, hr_ref[0, :, lanes], 0.0), jnp.where(first, hi_ref[0, :, lanes], 0.0))
        er, ei = lax.fori_loop(0, slen, local_step, start, unroll=4)

        wr = pr_ref[slen - 1:slen, lanes]
        wi = pi_ref[slen - 1:slen, lanes]
        tr, ti = er[0:1, :], ei[0:1, :]
        starts_r = [jnp.zeros_like(tr)]
        starts_i = [jnp.zeros_like(ti)]
        for s in range(1, nseg):
            starts_r.append(tr)
            starts_i.append(ti)
            tr, ti = er[s:s + 1, :] + wr * tr - wi * ti, ei[s:s + 1, :] + wr * ti + wi * tr
        hr_ref[0, :, lanes] = tr
        hi_ref[0, :, lanes] = ti
        gr = jnp.concatenate(starts_r, axis=0)
        gi = jnp.concatenate(starts_i, axis=0)

        def fix_step(t, carry, lanes=lanes, gr=gr, gi=gi):
            rows = pl.ds(pl.multiple_of(t * nseg, nseg), nseg)
            qr = pr_ref[pl.ds(t, 1), lanes]
            qi = pi_ref[pl.ds(t, 1), lanes]
            sr_ref[rows, lanes] += qr * gr - qi * gi
            si_ref[rows, lanes] += qr * gi + qi * gr
            return carry

        lax.fori_loop(0, slen, fix_step, 0, unroll=4)

    yp = _s5_readout(sr_ref, si_ref, cr_ref, ci_ref)
    for c in range(SEC // LANES):
        yp_ref[c] = yp[:, c * LANES:(c + 1) * LANES]
    for c in range(SEC // LANES):
        for s in range(nseg):
            yt_ref[s * slen:(s + 1) * slen, c * LANES:(c + 1) * LANES] = (
                yp_ref[c, pl.ds(s, slen, stride=nseg), :])
    _s5_gate(yt_ref[...] + d_ref[...] * u_ref[...], gw_ref, gb_ref, y_ref)


def _s5_body(u_ref, h0r_ref, h0i_ref, ar_ref, ai_ref, br_ref, bi_ref, cr_ref, ci_ref, d_ref,
             gw_ref, gb_ref, y_ref, hr_ref, hi_ref, sr_ref, si_ref, *, tl):
    step = pl.program_id(1)
    nblk = br_ref.shape[0]
    nst = br_ref.shape[2]

    @pl.when(step == 0)
    def _():
        hr_ref[...] = h0r_ref[...]
        hi_ref[...] = h0i_ref[...]

    _s5_drive(u_ref, br_ref, bi_ref, sr_ref, si_ref, None)

    for blk in range(nblk):
        lanes = slice(blk * nst, (blk + 1) * nst)
        ar = ar_ref[:, lanes]
        ai = ai_ref[:, lanes]

        def scan_step(t, state, lanes=lanes, ar=ar, ai=ai):
            hr, hi = state
            row = pl.ds(t, 1)
            nr = ar * hr - ai * hi + sr_ref[row, lanes]
            ni = ar * hi + ai * hr + si_ref[row, lanes]
            sr_ref[row, lanes] = nr
            si_ref[row, lanes] = ni
            return nr, ni

        hr, hi = lax.fori_loop(0, tl, scan_step, (hr_ref[0, :, lanes], hi_ref[0, :, lanes]),
                               unroll=SUBLANES)
        hr_ref[0, :, lanes] = hr
        hi_ref[0, :, lanes] = hi

    y = _s5_readout(sr_ref, si_ref, cr_ref, ci_ref) + d_ref[...] * u_ref[...]
    _s5_gate(y, gw_ref, gb_ref, y_ref)


def s5_discretize(a_re, a_im, log_dt, b_re, b_im, c_re, c_im):
    ngrp, nst = a_re.shape
    dt = jnp.exp(log_dt.astype(F32))[:, None]
    ar = a_re.astype(F32)
    ai = a_im.astype(F32)
    mag = jnp.exp(ar * dt)
    abar_re = mag * jnp.cos(ai * dt)
    abar_im = mag * jnp.sin(ai * dt)
    zr = abar_re - 1.0
    zi = abar_im
    den = ar * ar + ai * ai
    coef_re = (zr * ar + zi * ai) / den
    coef_im = (zi * ar - zr * ai) / den
    br = b_re.astype(F32)
    bi = b_im.astype(F32)
    bbar_re = coef_re[..., None] * br - coef_im[..., None] * bi
    bbar_im = coef_re[..., None] * bi + coef_im[..., None] * br
    gb = S5_GROUP_BLOCK
    nblk = ngrp // gb
    eye = jnp.eye(gb, dtype=F32)

    def in_blocks(bbar):
        t = bbar.reshape(nblk, gb, nst, S5_CH)
        return jnp.einsum('bgnc,gh->bgchn', t, eye).reshape(nblk, gb * S5_CH, gb * nst).astype(BF16)

    def out_blocks(c):
        t = c.astype(F32).reshape(nblk, gb, S5_CH, nst)
        return jnp.einsum('bgcn,gh->bgnhc', t, eye).reshape(nblk, gb * nst, gb * S5_CH).astype(BF16)

    return (abar_re.reshape(1, ngrp * nst), abar_im.reshape(1, ngrp * nst),
            in_blocks(bbar_re), in_blocks(bbar_im), out_blocks(c_re), out_blocks(c_im))


def s5(proj, h0_re, h0_im, disc, d, gate_w, gate_b, bsz, length):
    abar_re, abar_im, b_re, b_im, c_re, c_im = disc
    nstate = abar_re.shape[1]
    tl = _row_tile(length, S5_TILE, SUBLANES)
    nt = length // tl
    state = pl.BlockSpec((1, 1, nstate), lambda b, t: (b, 0, 0))

    def whole(a):
        return pl.BlockSpec(a.shape, lambda b, t, nd=a.ndim: (0,) * nd)

    d2 = d.reshape(1, SEC)
    gb2 = gate_b.reshape(1, SEC)
    scratch = [pltpu.VMEM((tl, nstate), F32), pltpu.VMEM((tl, nstate), F32)]
    body = _s5_body
    if tl % (SUBLANES * SUBLANES) == 0:
        body = _s5_segmented_body
        slen = tl // SUBLANES
        scratch += [pltpu.VMEM((slen, nstate), F32), pltpu.VMEM((slen, nstate), F32),
                    pltpu.VMEM((SEC // LANES, tl, LANES), F32), pltpu.VMEM((tl, SEC), F32)]
    y, hr, hi = pl.pallas_call(
        functools.partial(body, tl=tl),
        out_shape=(jax.ShapeDtypeStruct((bsz * length, SEC), F32),
                   jax.ShapeDtypeStruct((bsz, 1, nstate), F32),
                   jax.ShapeDtypeStruct((bsz, 1, nstate), F32)),
        grid=(bsz, nt),
        in_specs=[pl.BlockSpec((tl, SEC), lambda b, t: (b * nt + t, UD)), state, state,
                  whole(abar_re), whole(abar_im), whole(b_re), whole(b_im), whole(c_re), whole(c_im),
                  whole(d2), whole(gate_w), whole(gb2)],
        out_specs=(pl.BlockSpec((tl, SEC), lambda b, t: (b * nt + t, 0)), state, state),
        scratch_shapes=scratch,
        compiler_params=_params(("parallel", "arbitrary"), 48),
        name="s5",
    )(proj, h0_re.reshape(bsz, 1, nstate), h0_im.reshape(bsz, 1, nstate),
      abar_re, abar_im, b_re, b_im, c_re, c_im, d2, gate_w, gb2)
    return y, hr, hi


def _cross_body(q_ref, k_ref, v_ref, o_ref, *, dh):
    scale = dh ** -0.5
    for h in range(H_X):
        sl = slice(h * dh, (h + 1) * dh)
        s = _dot_nt(q_ref[:, sl].astype(BF16), k_ref[:, sl].astype(BF16)) * scale
        e = jnp.exp(s - jnp.max(s, axis=-1, keepdims=True))
        p = e / jnp.sum(e, axis=-1, keepdims=True)
        o_ref[:, sl] = _dot(p.astype(BF16), v_ref[:, sl].astype(BF16)).astype(o_ref.dtype)


def cross_attend(q, mem_k, mem_v, k_col, v_col, bsz, length, n_mem):
    wx = q.shape[1]
    tq = _row_tile(length, 512, SUBLANES)
    nq = length // tq
    return pl.pallas_call(
        functools.partial(_cross_body, dh=wx // H_X),
        out_shape=jax.ShapeDtypeStruct((bsz * length, wx), F32),
        grid=(bsz, nq),
        in_specs=[pl.BlockSpec((tq, wx), lambda b, i: (b * nq + i, 0)),
                  pl.BlockSpec((n_mem, wx), lambda b, i: (b, k_col)),
                  pl.BlockSpec((n_mem, wx), lambda b, i: (b, v_col))],
        out_specs=pl.BlockSpec((tq, wx), lambda b, i: (b * nq + i, 0)),
        compiler_params=_params(("parallel", "arbitrary"), 32),
        name="cross_attend",
    )(q, mem_k, mem_v)


def _decoder_layer(x, bsz, length, mem_k, mem_v, k_col, v_col, n_mem, sb_fn, mlstm_state, conv_buf,
                   s5_state, p, kv_stacked, layer, depth):
    h, gates = rmsnorm_gates(x, p['norm_mix_pre'], p['w_gates'])
    proj = matmul(h, p['w_in'], F32)
    y_a, c_new, n_new, m_new = mlstm(proj, gates, p['gate_bias'], *mlstm_state, bsz, length)
    y_b = sb_fn(proj)
    y_c, conv_new = conformer_conv(proj, conv_buf, p['conv_w'], p['conv_b'], p['conv_ln_g'],
                                   p['conv_ln_b'], bsz, length)
    y_d, s5_re, s5_im = s5(proj, s5_state[0], s5_state[1], p['s5_disc'], p['s5_d'], p['s5_gate_w'],
                           p['s5_gate_b'], bsz, length)
    x, hx = mix_out_proj(y_a, y_b, y_c, y_d, p['g_group'], p['w_out'], p['norm_mix_post'], x,
                         p['norm_x_pre'])
    q = matmul(hx, p['w_xq'], F32)
    o = cross_attend(q, mem_k, mem_v, k_col, v_col, bsz, length, n_mem)
    x, hf = matmul_norm_res(o, p['w_xo'], p['norm_x_post'], x, p['norm_ffn_pre'])
    kv_stacked = (head_split(proj, KB, H_B, kv_stacked[0], layer, depth),
                  head_split(proj, VB, H_B, kv_stacked[1], layer, depth))
    return x, hf, kv_stacked, (c_new, n_new, m_new), conv_new, (s5_re, s5_im)


def kernel(x_prompt, x_sample, cache_sb_k, cache_sb_v, page_table, state_mlstm_c, state_mlstm_n, state_mlstm_m, state_conv, state_s5_re, state_s5_im, cache_mem_k, cache_mem_v, mem_prompt, norm_mix_pre, w_in, b_mlstm_gates, sb_bias, conv_w, conv_b, conv_ln_g, conv_ln_b, s5_a_re, s5_a_im, s5_log_dt, s5_b_re, s5_b_im, s5_c_re, s5_c_im, s5_d, s5_gate_w, s5_gate_b, g_group, w_out, norm_mix_post, norm_mem, norm_x_pre, w_xq, w_xk, w_xv, w_xo, norm_x_post, norm_ffn_pre, w_up, w_down, norm_ffn_post):
    n_b, seq, d_model = x_prompt.shape
    n_db, dec_seq, _ = x_sample.shape
    depth = w_in.shape[0]
    n_mem = mem_prompt.shape[1]
    w_x = w_xq.shape[2]
    dh_a = SEC // H_A
    ngrp = s5_a_re.shape[1]
    n_gate = 2 * H_A
    gate_lo = 4 * SEC

    xp = x_prompt.reshape(n_b * seq, d_model)
    xs = x_sample.reshape(n_db * dec_seq, d_model)
    mem2d = mem_prompt.reshape(n_b * n_mem, d_model)
    w_in_t = jnp.transpose(w_in, (0, 2, 1))
    outs = [[] for _ in range(14)]
    kv_p = (None, None)
    kv_s = (None, None)
    for l in range(depth):
        w_in_main, w_in_gates = cast_w_in(w_in_t, l, gate_lo, n_gate)
        p = {
            'norm_mix_pre': norm_mix_pre[l],
            'w_in': w_in_main,
            'w_gates': w_in_gates,
            'gate_bias': jnp.pad(b_mlstm_gates[l].astype(F32), (0, LANES - n_gate)).reshape(1, LANES),
            'conv_w': conv_w[l], 'conv_b': conv_b[l], 'conv_ln_g': conv_ln_g[l], 'conv_ln_b': conv_ln_b[l],
            's5_disc': s5_discretize(s5_a_re[l], s5_a_im[l], s5_log_dt[l], s5_b_re[l], s5_b_im[l],
                                     s5_c_re[l], s5_c_im[l]),
            's5_d': s5_d[l], 's5_gate_w': s5_gate_w[l].astype(BF16), 's5_gate_b': s5_gate_b[l],
            'g_group': g_group[l], 'w_out': cast_bf16(w_out, l), 'norm_mix_post': norm_mix_post[l],
            'norm_x_pre': norm_x_pre[l], 'w_xq': w_xq[l].astype(BF16), 'w_xo': w_xo[l].astype(BF16),
            'norm_x_post': norm_x_post[l], 'norm_ffn_pre': norm_ffn_pre[l],
        }
        w_kv = jnp.concatenate([w_xk[l], w_xv[l]], axis=1).astype(BF16)
        mem_kv = matmul(rmsnorm_bf16(mem2d, norm_mem[l]), w_kv, F32, tm_cap=512)
        zero_mlstm = (jnp.zeros((n_b, H_A, dh_a, dh_a), F32), jnp.zeros((n_b, H_A, dh_a), F32),
                      jnp.zeros((n_b, H_A), F32))
        zero_s5 = (jnp.zeros((n_b, ngrp, S5_N), F32), jnp.zeros((n_b, ngrp, S5_N), F32))
        sb_s = functools.partial(sb_sample, sb_bias=sb_bias[l], cache_k=cache_sb_k, cache_v=cache_sb_v,
                                 page_table=page_table, layer=l, bsz=n_db, length=dec_seq)
        xs, hf_s, kv_s, mst_s, cnv_s, s5_s = _decoder_layer(
            xs, n_db, dec_seq, cache_mem_k[l].reshape(n_db * n_mem, w_x),
            cache_mem_v[l].reshape(n_db * n_mem, w_x), 0, 0, n_mem, sb_s,
            (state_mlstm_c[l], state_mlstm_n[l], state_mlstm_m[l]), state_conv[l],
            (state_s5_re[l], state_s5_im[l]), p, kv_s, l, depth)
        w_up_tiles, u_s = cast_up_with_rows(w_up, l, FFN_TILE, hf_s)
        w_down_bf, xs = cast_down_with_rows(w_down, l, u_s, norm_ffn_post[l], xs)
        sb_p = functools.partial(sb_prompt, sb_bias=sb_bias[l], bsz=n_b, length=seq)
        xp, hf_p, kv_p, mst_p, cnv_p, s5_p = _decoder_layer(
            xp, n_b, seq, mem_kv, mem_kv, 0, 1, n_mem, sb_p, zero_mlstm,
            jnp.zeros((n_b, CONV_W - 1, SEC), F32), zero_s5, p, kv_p, l, depth)
        xp = ffn(hf_p, w_up_tiles, w_down_bf, norm_ffn_post[l], xp)
        mk = mem_kv[:, :w_x].reshape(n_b, n_mem, H_X, w_x // H_X)
        mv = mem_kv[:, w_x:].reshape(n_b, n_mem, H_X, w_x // H_X)
        s5_shape = lambda a: a.reshape(a.shape[0], ngrp, S5_N)
        layer_out = (mst_p[0], mst_p[1], mst_p[2], mst_s[0], mst_s[1], mst_s[2],
                     cnv_p, cnv_s, s5_shape(s5_p[0]), s5_shape(s5_p[1]), s5_shape(s5_s[0]),
                     s5_shape(s5_s[1]), mk, mv)
        for acc, val in zip(outs, layer_out):
            acc.append(val)
    stacked = tuple(jnp.stack(vals, axis=0) for vals in outs)
    dh_b = SEC // H_B
    kv = tuple(a.reshape(depth, n, t, H_B, dh_b)
               for a, n, t in ((kv_p[0], n_b, seq), (kv_p[1], n_b, seq),
                               (kv_s[0], n_db, dec_seq), (kv_s[1], n_db, dec_seq)))
    return (xp.reshape(n_b, seq, d_model), xs.reshape(n_db, dec_seq, d_model)) + kv + stacked
```
